```python
import math
import jax, jax.numpy as jnp
from jax import lax
import numpy as np

D_MODEL = 2048
BATCH = 4
SEQ = 2048
DEPTH = 2

GRID_W = 64
CTX_LEN = 256
HEAD_DIM = 128
N_MIX_HEADS = D_MODEL // HEAD_DIM
NA_HEADS = N_MIX_HEADS // 2
NA_WIN_R = 8
NA_WIN_C = 16
NA_WIDTH = NA_HEADS * HEAD_DIM
HG_HEADS = N_MIX_HEADS // 2
HG_DK = 128
HG_DV = HEAD_DIM
HG_KWIDTH = HG_HEADS * HG_DK
HG_VWIDTH = HG_HEADS * HG_DV
HG_CHUNK = 16
EVEN_IN = 3 * NA_WIDTH + 3 * HG_KWIDTH + 2 * HG_VWIDTH
EVEN_OUT = NA_WIDTH + HG_VWIDTH
DIFF_HEADS = D_MODEL // (2 * HEAD_DIM)
DIFF_DK = HEAD_DIM
DIFF_DV = 2 * HEAD_DIM
DIFF_QK_WIDTH = DIFF_HEADS * 2 * DIFF_DK
DIFF_V_WIDTH = DIFF_HEADS * DIFF_DV
ODD_IN = 2 * DIFF_QK_WIDTH + DIFF_V_WIDTH
ODD_OUT = DIFF_V_WIDTH
D_FF = 4 * D_MODEL
N_MOD = 6
ROPE_THETA = 10000.0
Q_BLOCK = 128
EPS = 1e-6
N_EVEN = (DEPTH + 1) // 2
N_ODD = DEPTH // 2

kernel_name = "hybrid_natten_hgrn2_diffattn_dit"


def rms_norm(x, g):
    xf = x.astype(jnp.float32)
    y = xf * lax.rsqrt(jnp.mean(xf * xf, axis=-1, keepdims=True) + EPS)
    return (y * g).astype(x.dtype)


def modulate(h, mod, i):
    return h * (1.0 + mod[:, i + 1, None, :]) + mod[:, i, None, :]


def split_cols(p, sizes):
    out, start = [], 0
    for s in sizes:
        out.append(p[..., start:start + s])
        start += s
    return out


def merge_heads(o):
    b, h, n, d = o.shape
    return o.transpose(0, 2, 1, 3).reshape(b, n, h * d)


def flip_t(t):
    return jnp.flip(t, axis=2)


def sq_relu_mlp(h, w1, w2):
    return jnp.square(jax.nn.relu(h @ w1)) @ w2


def axial_rope_tables(n_tokens):
    t = jnp.arange(n_tokens)
    row = (t // GRID_W).astype(jnp.float32)
    col = (t % GRID_W).astype(jnp.float32)
    n_freq = DIFF_DK // 4
    inv = ROPE_THETA ** (-jnp.arange(n_freq, dtype=jnp.float32) / n_freq)
    ang = jnp.concatenate([row[:, None] * inv, col[:, None] * inv], axis=-1)
    return jnp.cos(ang), jnp.sin(ang)


def apply_axial_rope(x, cos, sin):
    half = x.shape[-1] // 2
    x1, x2 = x[..., :half], x[..., half:]
    return jnp.concatenate([x1 * cos - x2 * sin, x1 * sin + x2 * cos], axis=-1).astype(x.dtype)


def dense_attention(q, k, v):
    s = jnp.einsum('bhqd,bhkd->bhqk', q, k).astype(jnp.float32) * q.shape[-1] ** -0.5
    return jnp.einsum('bhqk,bhkd->bhqd', jax.nn.softmax(s, axis=-1).astype(v.dtype), v)


def neighbourhood_attention(q, k, v, kc, vc, rpb):
    b, H, S, dh = q.shape
    rows = S // GRID_W
    wr = min(NA_WIN_R, rows)
    r = jnp.arange(rows)
    r0 = jnp.clip(r - wr // 2, 0, rows - wr)
    key_rows = r0[:, None] + jnp.arange(wr)[None, :]
    col = jnp.arange(GRID_W)
    c0 = jnp.clip(col - NA_WIN_C // 2, 0, GRID_W - NA_WIN_C)
    in_win = (col[None, :] >= c0[:, None]) & (col[None, :] < c0[:, None] + NA_WIN_C)
    dr = key_rows - r[:, None] + (NA_WIN_R - 1)
    dc = jnp.clip(col[None, :] - col[:, None] + (NA_WIN_C - 1), 0, 2 * NA_WIN_C - 2)
    bias = rpb[:, dr[:, None, :, None], dc[None, :, None, :]].astype(jnp.float32)
    bias = jnp.where(in_win[None, None, :, None, :], bias, -jnp.inf).reshape(H, rows, GRID_W, wr * GRID_W)
    qg = q.reshape(b, H, rows, GRID_W, dh)
    kb = k.reshape(b, H, rows, GRID_W, dh)[:, :, key_rows].reshape(b, H, rows, wr * GRID_W, dh)
    vb = v.reshape(b, H, rows, GRID_W, dh)[:, :, key_rows].reshape(b, H, rows, wr * GRID_W, dh)
    scale = dh ** -0.5
    s_lat = jnp.einsum('bhrqd,bhrkd->bhrqk', qg, kb).astype(jnp.float32) * scale + bias[None]
    s_ctx = jnp.einsum('bhrqd,bhld->bhrql', qg, kc).astype(jnp.float32) * scale
    p = jax.nn.softmax(jnp.concatenate([s_lat, s_ctx], axis=-1), axis=-1).astype(v.dtype)
    nk = wr * GRID_W
    o = (jnp.einsum('bhrqk,bhrkd->bhrqd', p[..., :nk], vb)
         + jnp.einsum('bhrql,bhld->bhrqd', p[..., nk:], vc))
    return o.reshape(b, H, S, dh)


def hgrn2_forget(z, lb):
    lb = lb.reshape(HG_HEADS, 1, HG_DK)
    f = lb + (1.0 - lb) * jax.nn.sigmoid(z.astype(jnp.float32))
    return jnp.log(f), 1.0 - f


def hgrn2_chunk_scan(q, k, log_f, v, s0, with_output):
    b, H, T, dk = q.shape
    dv = v.shape[-1]
    C = HG_CHUNK
    n = T // C
    f32 = jnp.float32
    qc = q.astype(f32).reshape(b, H, n, C, dk)
    kc = k.astype(f32).reshape(b, H, n, C, dk)
    vc = v.astype(f32).reshape(b, H, n, C, dv)
    cum = jnp.cumsum(log_f.astype(f32).reshape(b, H, n, C, dk), axis=3)
    last = cum[..., -1:, :]
    u = jnp.einsum('bhnsk,bhnsv->bhnkv', kc * jnp.exp(last - cum), vc)
    g = jnp.exp(last[..., 0, :])

    def step(S, inp):
        g_n, u_n = inp
        return g_n[..., None] * S + u_n, (S if with_output else None)

    s_final, s_prev = lax.scan(step, s0, (jnp.moveaxis(g, 2, 0), jnp.moveaxis(u, 2, 0)))
    if not with_output:
        return None, s_final
    s_prev = jnp.moveaxis(s_prev, 0, 2)
    tri = jnp.tril(jnp.ones((C, C), dtype=bool))[:, :, None]
    decay = jnp.exp(jnp.where(tri, cum[..., :, None, :] - cum[..., None, :, :], -jnp.inf))
    att = jnp.einsum('bhntk,bhnsk,bhntsk->bhnts', qc, kc, decay)
    o = (jnp.einsum('bhnts,bhnsv->bhntv', att, vc)
         + jnp.einsum('bhntk,bhnkv->bhntv', qc * jnp.exp(cum), s_prev))
    return o.reshape(b, H, T, dv).astype(v.dtype), s_final


def even_mixer(h, hc, w_in, w_out, q_g, k_g, rpb, lb_fwd, lb_bwd, gn_g, need_ctx):
    def project(t):
        p = t @ w_in
        bsz, n, _ = p.shape
        qa, ka, va, qb, zf, zb, ib, gb = split_cols(
            p, (NA_WIDTH,) * 3 + (HG_KWIDTH,) * 3 + (HG_VWIDTH,) * 2)
        hd = lambda z, H: z.reshape(bsz, n, H, -1).transpose(0, 2, 1, 3)
        return (rms_norm(hd(qa, NA_HEADS), q_g), rms_norm(hd(ka, NA_HEADS), k_g), hd(va, NA_HEADS),
                jax.nn.silu(hd(qb, HG_HEADS)), hd(zf, HG_HEADS), hd(zb, HG_HEADS),
                hd(ib, HG_HEADS), hd(gb, HG_HEADS))

    qa, ka, va, qb, zf, zb, ib, gb = project(h)
    qac, kac, vac, qbc, zfc, zbc, ibc, gbc = project(hc)
    o_na = neighbourhood_attention(qa, ka, va, kac, vac, rpb)
    logf_f, k_f = hgrn2_forget(zf, lb_fwd)
    logf_b, k_b = hgrn2_forget(zb, lb_bwd)
    logfc_f, kc_f = hgrn2_forget(zfc, lb_fwd)
    logfc_b, kc_b = hgrn2_forget(zbc, lb_bwd)
    s0 = jnp.zeros((hc.shape[0], HG_HEADS, HG_DK, HG_DV), jnp.float32)
    oc_f, sc_f = hgrn2_chunk_scan(qbc, kc_f, logfc_f, ibc, s0, need_ctx)
    oc_b, sc_b = hgrn2_chunk_scan(flip_t(qbc), flip_t(kc_b), flip_t(logfc_b), flip_t(ibc), s0, need_ctx)
    o_f, _ = hgrn2_chunk_scan(qb, k_f, logf_f, ib, sc_f, True)
    o_b, _ = hgrn2_chunk_scan(flip_t(qb), flip_t(k_b), flip_t(logf_b), flip_t(ib), sc_b, True)
    o_hg = rms_norm(o_f + flip_t(o_b), gn_g) * jax.nn.silu(gb)
    y = jnp.concatenate([merge_heads(o_na), merge_heads(o_hg)], axis=-1) @ w_out
    if not need_ctx:
        return y, None
    oc_na = dense_attention(qac, kac, vac)
    oc_hg = rms_norm(oc_f + flip_t(oc_b), gn_g) * jax.nn.silu(gbc)
    yc = jnp.concatenate([merge_heads(oc_na), merge_heads(oc_hg)], axis=-1) @ w_out
    return y, yc


def diff_weights(s, lam, dtype):
    p = jax.nn.softmax(s, axis=-1)
    return (p[:, :, 0] - lam * p[:, :, 1]).astype(dtype)


def diff_attention_latent(q, k, v, kc, vc, lam):
    b, H, _, S, dk = q.shape
    dv = v.shape[-1]
    nb = S // Q_BLOCK
    qb = jnp.moveaxis(q.reshape(b, H, 2, nb, Q_BLOCK, dk), 3, 0)
    scale = dk ** -0.5

    def one_block(qblk):
        s = jnp.concatenate([jnp.einsum('bhiqd,bhikd->bhiqk', qblk, k),
                             jnp.einsum('bhiqd,bhikd->bhiqk', qblk, kc)], axis=-1).astype(jnp.float32) * scale
        a = diff_weights(s, lam, v.dtype)
        return (jnp.einsum('bhqk,bhkd->bhqd', a[..., :S], v)
                + jnp.einsum('bhqk,bhkd->bhqd', a[..., S:], vc))

    o = lax.map(one_block, qb)
    return jnp.moveaxis(o, 0, 2).reshape(b, H, S, dv)


def odd_mixer(h, hc, w_in, w_out, q_g, k_g, lam_p, subln_g, layer_idx, cos, sin, need_ctx):
    def project(t):
        p = t @ w_in
        bsz, n, _ = p.shape
        q, k, v = split_cols(p, (DIFF_QK_WIDTH, DIFF_QK_WIDTH, DIFF_V_WIDTH))
        qk = lambda z, g: rms_norm(z.reshape(bsz, n, DIFF_HEADS, 2, DIFF_DK), g).transpose(0, 2, 3, 1, 4)
        return qk(q, q_g), qk(k, k_g), v.reshape(bsz, n, DIFF_HEADS, DIFF_DV).transpose(0, 2, 1, 3)

    q, k, v = project(h)
    q, k = apply_axial_rope(q, cos, sin), apply_axial_rope(k, cos, sin)
    qc, kc, vc = project(hc)
    lam_init = 0.8 - 0.6 * math.exp(-0.3 * layer_idx)
    lp = lam_p.astype(jnp.float32)
    lam = jnp.exp(jnp.sum(lp[0] * lp[1])) - jnp.exp(jnp.sum(lp[2] * lp[3])) + lam_init

    def finish(o):
        return merge_heads(rms_norm(o, subln_g) * (1.0 - lam_init)) @ w_out

    y = finish(diff_attention_latent(q, k, v, kc, vc, lam))
    if not need_ctx:
        return y, None
    s_c = jnp.einsum('bhiqd,bhikd->bhiqk', qc, kc).astype(jnp.float32) * DIFF_DK ** -0.5
    yc = finish(jnp.einsum('bhqk,bhkd->bhqd', diff_weights(s_c, lam, vc.dtype), vc))
    return y, yc


def setup_inputs(seed: int = 0) -> dict:
    key = jax.random.key(seed)
    ks = jax.random.split(key, 24)
    nrm = lambda k, shape, s: s * jax.random.normal(k, shape, jnp.float32)
    gain = lambda k, shape: 1.0 + 0.02 * jax.random.normal(k, shape, jnp.float32)
    return {
        "x": nrm(ks[0], (BATCH, SEQ, D_MODEL), 1.0),
        "c": nrm(ks[1], (BATCH, D_MODEL), 1.0),
        "ctx": nrm(ks[2], (BATCH, CTX_LEN, D_MODEL), 1.0),
        "c_ctx": nrm(ks[3], (D_MODEL,), 1.0),
        "ada_w": nrm(ks[4], (DEPTH, D_MODEL, N_MOD * D_MODEL), 0.5 * D_MODEL ** -0.5),
        "ada_b": nrm(ks[5], (DEPTH, N_MOD * D_MODEL), 0.01),
        "norm_mix_g": gain(ks[6], (DEPTH, D_MODEL)),
        "norm_mlp_g": gain(ks[7], (DEPTH, D_MODEL)),
        "mlp_w1": nrm(ks[8], (DEPTH, D_MODEL, D_FF), D_MODEL ** -0.5),
        "mlp_w2": nrm(ks[9], (DEPTH, D_FF, D_MODEL), D_FF ** -0.5),
        "ev_w_in": nrm(ks[10], (N_EVEN, D_MODEL, EVEN_IN), D_MODEL ** -0.5),
        "ev_w_out": nrm(ks[11], (N_EVEN, EVEN_OUT, D_MODEL), EVEN_OUT ** -0.5),
        "na_q_g": gain(ks[12], (N_EVEN, HEAD_DIM)),
        "na_k_g": gain(ks[13], (N_EVEN, HEAD_DIM)),
        "na_rpb": nrm(ks[14], (N_EVEN, NA_HEADS, 2 * NA_WIN_R - 1, 2 * NA_WIN_C - 1), 0.1),
        "hg_lb_logits": nrm(ks[15], (2, DEPTH + 1, HG_KWIDTH), 0.5),
        "hg_gnorm_g": gain(ks[16], (N_EVEN, HG_DV)),
        "od_w_in": nrm(ks[17], (N_ODD, D_MODEL, ODD_IN), D_MODEL ** -0.5),
        "od_w_out": nrm(ks[18], (N_ODD, ODD_OUT, D_MODEL), ODD_OUT ** -0.5),
        "df_q_g": gain(ks[19], (N_ODD, DIFF_DK)),
        "df_k_g": gain(ks[20], (N_ODD, DIFF_DK)),
        "df_lambda": nrm(ks[21], (N_ODD, 4, DIFF_DK), 0.1),
        "df_subln_g": gain(ks[22], (N_ODD, DIFF_DV)),
    }


def reference(x, c, ctx, c_ctx, ada_w, ada_b, norm_mix_g, norm_mlp_g, mlp_w1, mlp_w2,
              ev_w_in, ev_w_out, na_q_g, na_k_g, na_rpb, hg_lb_logits, hg_gnorm_g,
              od_w_in, od_w_out, df_q_g, df_k_g, df_lambda, df_subln_g):
    b, n_lat, _ = x.shape
    cos, sin = axial_rope_tables(n_lat)
    lb_all = jnp.cumsum(jax.nn.softmax(hg_lb_logits.astype(jnp.float32), axis=1), axis=1)
    silu_c = jax.nn.silu(c)
    silu_cc = jax.nn.silu(c_ctx)[None]
    for l in range(DEPTH):
        need_ctx = l < DEPTH - 1
        mod = (silu_c @ ada_w[l] + ada_b[l]).reshape(b, N_MOD, D_MODEL)
        modc = (silu_cc @ ada_w[l] + ada_b[l]).reshape(1, N_MOD, D_MODEL)
        h = modulate(rms_norm(x, norm_mix_g[l]), mod, 0)
        hc = modulate(rms_norm(ctx, norm_mix_g[l]), modc, 0)
        if l % 2 == 0:
            e = l // 2
            y, yc = even_mixer(h, hc, ev_w_in[e], ev_w_out[e], na_q_g[e], na_k_g[e], na_rpb[e],
                               lb_all[0, l], lb_all[1, l], hg_gnorm_g[e], need_ctx)
        else:
            o = l // 2
            y, yc = odd_mixer(h, hc, od_w_in[o], od_w_out[o], df_q_g[o], df_k_g[o], df_lambda[o],
                              df_subln_g[o], l, cos, sin, need_ctx)
        x = x + mod[:, 2, None, :] * y
        h2 = modulate(rms_norm(x, norm_mlp_g[l]), mod, 3)
        x = x + mod[:, 5, None, :] * sq_relu_mlp(h2, mlp_w1[l], mlp_w2[l])
        if need_ctx:
            ctx = ctx + modc[:, 2, None, :] * yc
            hc2 = modulate(rms_norm(ctx, norm_mlp_g[l]), modc, 3)
            ctx = ctx + modc[:, 5, None, :] * sq_relu_mlp(hc2, mlp_w1[l], mlp_w2[l])
    return x
```

```python
import functools
import math

import jax
import jax.numpy as jnp
from jax import lax
from jax.experimental import pallas as pl
from jax.experimental.pallas import tpu as pltpu

F32 = jnp.float32
BF16 = jnp.bfloat16

D_MODEL = 2048
BATCH = 4
SEQ = 2048
DEPTH = 2
GRID_W = 64
GRID_ROWS = SEQ // GRID_W
CTX_LEN = 256
HEAD_DIM = 128
NA_HEADS = 8
NA_WIN_R = 8
NA_WIN_C = 16
NA_WIDTH = NA_HEADS * HEAD_DIM
HG_HEADS = 8
HG_CHUNK = 16
EVEN_IN = 8 * NA_WIDTH
DIFF_HEADS = 8
DIFF_DK = HEAD_DIM
DIFF_DV = 2 * HEAD_DIM
ODD_IN = 3 * D_MODEL
D_FF = 4 * D_MODEL
N_MOD = 6
ROPE_THETA = 10000.0
EPS = 1e-6

N_LAT = BATCH * SEQ
N_CTX = BATCH * CTX_LEN
N_TOK = N_LAT + N_CTX
MOD_ROWS = 8
CTX_MOD_ROW = BATCH
NEG_BIG = -1e30

VMEM_LIMIT_V7X = 56 * 1024 * 1024


def _cparams(sem, vmem=VMEM_LIMIT_V7X):
    return pltpu.CompilerParams(dimension_semantics=sem, vmem_limit_bytes=vmem)


def _dot(a, b):
    return jnp.dot(a, b, preferred_element_type=F32)


def _dot_nt(a, b):
    return lax.dot_general(a, b, (((1,), (1,)), ((), ())), preferred_element_type=F32)


def _mod_row(tile_rows):
    return lambda i: jnp.minimum((i * tile_rows) // SEQ, CTX_MOD_ROW)


def _ada_body(c_ref, w_ref, b_ref, o_ref):
    c = c_ref[...]
    s = (c * jax.nn.sigmoid(c)).astype(BF16)
    o_ref[0] = _dot(s, w_ref[0].astype(BF16)) + b_ref[0]


def ada_table(cond, ada_w, ada_b):
    tn = 1024
    n = N_MOD * D_MODEL
    out = pl.pallas_call(
        _ada_body,
        out_shape=jax.ShapeDtypeStruct((DEPTH, MOD_ROWS, n), F32),
        grid=(DEPTH, n // tn),
        in_specs=[
            pl.BlockSpec((MOD_ROWS, D_MODEL), lambda l, j: (0, 0)),
            pl.BlockSpec((1, D_MODEL, tn), lambda l, j: (l, 0, j)),
            pl.BlockSpec((1, 1, tn), lambda l, j: (l, 0, j)),
        ],
        out_specs=pl.BlockSpec((1, MOD_ROWS, tn), lambda l, j: (l, 0, j)),
        compiler_params=_cparams(("arbitrary", "arbitrary")),
        name="ada_table",
    )(cond, ada_w, ada_b.reshape(DEPTH, 1, n))
    return out.reshape(DEPTH, MOD_ROWS, N_MOD, D_MODEL)


def _normmod_body(x_ref, g_ref, mod_ref, o_ref, *, shift_idx):
    x = x_ref[...]
    y = x * lax.rsqrt(jnp.mean(x * x, axis=-1, keepdims=True) + EPS) * g_ref[...]
    shift = mod_ref[0, shift_idx:shift_idx + 1, :]
    scale = mod_ref[0, shift_idx + 1:shift_idx + 2, :]
    o_ref[...] = (y * (1.0 + scale) + shift).astype(o_ref.dtype)


def norm_modulate(xs, g, mod, shift_idx, n_rows):
    tm = 256
    return pl.pallas_call(
        functools.partial(_normmod_body, shift_idx=shift_idx),
        out_shape=jax.ShapeDtypeStruct((n_rows, D_MODEL), BF16),
        grid=(n_rows // tm,),
        in_specs=[
            pl.BlockSpec((tm, D_MODEL), lambda i: (i, 0)),
            pl.BlockSpec((1, D_MODEL), lambda i: (0, 0)),
            pl.BlockSpec((1, N_MOD, D_MODEL), lambda i: (_mod_row(tm)(i), 0, 0)),
        ],
        out_specs=pl.BlockSpec((tm, D_MODEL), lambda i: (i, 0)),
        compiler_params=_cparams(("arbitrary",)),
        name="norm_modulate",
    )(xs, g.reshape(1, D_MODEL), mod)


def _mm_body(*refs, nk, epi, gate_idx):
    if epi == "res":
        x_ref, w_ref, res_ref, mod_ref, o_ref = refs[:5]
        scratch = refs[5:]
    else:
        x_ref, w_ref, o_ref = refs[:3]
        scratch = refs[3:]

    def finish(acc):
        if epi == "relu2":
            r = jnp.maximum(acc, 0.0)
            o_ref[...] = (r * r).astype(o_ref.dtype)
        elif epi == "res":
            gate = mod_ref[0, gate_idx:gate_idx + 1, :]
            o_ref[...] = res_ref[...] + gate * acc
        else:
            o_ref[...] = acc.astype(o_ref.dtype)

    if nk == 1:
        finish(_dot(x_ref[...], w_ref[...]))
    else:
        acc_ref = scratch[0]
        k = pl.program_id(2)

        @pl.when(k == 0)
        def _():
            acc_ref[...] = jnp.zeros_like(acc_ref)

        acc_ref[...] += _dot(x_ref[...], w_ref[...])

        @pl.when(k == nk - 1)
        def _():
            finish(acc_ref[...])


def matmul(x, w, *, n_rows, out_dtype, epi="plain", res=None, mod=None, gate_idx=0,
           tm=1024, tn=1024, tk=2048, name="matmul"):
    k_dim, n_dim = w.shape
    nk = k_dim // tk
    in_specs = [
        pl.BlockSpec((tm, tk), lambda i, j, k: (i, k)),
        pl.BlockSpec((tk, tn), lambda i, j, k: (k, j)),
    ]
    args = [x, w]
    if epi == "res":
        in_specs += [
            pl.BlockSpec((tm, tn), lambda i, j, k: (i, j)),
            pl.BlockSpec((1, N_MOD, tn), lambda i, j, k: (_mod_row(tm)(i), 0, j)),
        ]
        args += [res, mod]
    scratch = [pltpu.VMEM((tm, tn), F32)] if nk > 1 else []
    return pl.pallas_call(
        functools.partial(_mm_body, nk=nk, epi=epi, gate_idx=gate_idx),
        out_shape=jax.ShapeDtypeStruct((n_rows, n_dim), out_dtype),
        grid=(n_rows // tm, n_dim // tn, nk),
        in_specs=in_specs,
        out_specs=pl.BlockSpec((tm, tn), lambda i, j, k: (i, j)),
        scratch_shapes=scratch,
        compiler_params=_cparams(("arbitrary", "arbitrary", "arbitrary")),
        name=name,
    )(*args)


NA_PAIR_Q = 2 * GRID_W
NA_PAIR_KROWS = 10
NA_PAIR_K = NA_PAIR_KROWS * GRID_W
NA_PAIR_KINDS = (0, 2, 4, GRID_ROWS - 4, GRID_ROWS - 2)


def _na_pair_key_row0(r):
    return min(max(r - NA_WIN_R // 2, 0), GRID_ROWS - NA_PAIR_KROWS)


def na_bias_table(rpb):
    col = jnp.arange(GRID_W)
    c0 = jnp.clip(col - NA_WIN_C // 2, 0, GRID_W - NA_WIN_C)
    in_win = (col[None, :] >= c0[:, None]) & (col[None, :] < c0[:, None] + NA_WIN_C)
    dc = jnp.clip(col[None, :] - col[:, None] + (NA_WIN_C - 1), 0, 2 * NA_WIN_C - 2)
    kinds = []
    for r in NA_PAIR_KINDS:
        rq = r + jnp.arange(2)
        r0 = jnp.clip(rq - NA_WIN_R // 2, 0, GRID_ROWS - NA_WIN_R)
        kr = _na_pair_key_row0(r) + jnp.arange(NA_PAIR_KROWS)
        dr = kr[None, :] - rq[:, None] + (NA_WIN_R - 1)
        row_ok = (kr[None, :] >= r0[:, None]) & (kr[None, :] < r0[:, None] + NA_WIN_R)
        b = rpb[:, jnp.clip(dr, 0, 2 * NA_WIN_R - 2)[:, None, :, None], dc[None, :, None, :]]
        ok = row_ok[:, None, :, None] & in_win[None, :, None, :]
        b = jnp.where(ok[None], b.astype(F32), NEG_BIG)
        kinds.append(b.reshape(NA_HEADS, NA_PAIR_Q, NA_PAIR_K))
    return jnp.stack(kinds, axis=1)


def _rms_rows(x, g):
    return x * lax.rsqrt(jnp.mean(x * x, axis=-1, keepdims=True) + EPS) * g


def _na_body(q_ref, k_ref, v_ref, qc_ref, kc_ref, vc_ref, qg_ref, kg_ref, bias_ref,
             o_ref, oc_ref, qn_ref, kn_ref, vn_ref, kcn_ref, vcn_ref):
    scale = HEAD_DIM ** -0.5
    qg = qg_ref[...]
    kg = kg_ref[...]

    def prep(i, _):
        rows = pl.ds(pl.multiple_of(i * 256, 256), 256)
        qn_ref[rows, :] = (_rms_rows(q_ref[rows, :], qg) * scale).astype(BF16)
        kn_ref[rows, :] = _rms_rows(k_ref[rows, :], kg).astype(BF16)
        vn_ref[rows, :] = v_ref[rows, :].astype(BF16)
        return 0

    lax.fori_loop(0, SEQ // 256, prep, 0)
    kcn = _rms_rows(kc_ref[...], kg).astype(BF16)
    vcn = vc_ref[...].astype(BF16)
    kcn_ref[...] = kcn
    vcn_ref[...] = vcn

    n_pairs = GRID_ROWS // 2

    def pair(rp, _):
        r = 2 * rp
        u0 = jnp.clip(r - NA_WIN_R // 2, 0, GRID_ROWS - NA_PAIR_KROWS)
        kind = jnp.where(rp < 2, rp, jnp.where(rp >= n_pairs - 2, rp - (n_pairs - 5), 2))
        qrows = pl.ds(pl.multiple_of(rp * NA_PAIR_Q, NA_PAIR_Q), NA_PAIR_Q)
        krows = pl.ds(pl.multiple_of(u0 * GRID_W, 128), NA_PAIR_K)
        qb = qn_ref[qrows, :]
        s1 = _dot_nt(qb, kn_ref[krows, :]) + bias_ref[0, kind]
        s2 = _dot_nt(qb, kcn_ref[...])
        m = jnp.maximum(jnp.max(s1, axis=-1, keepdims=True), jnp.max(s2, axis=-1, keepdims=True))
        p1 = jnp.exp(s1 - m)
        p2 = jnp.exp(s2 - m)
        l = jnp.sum(p1, axis=-1, keepdims=True) + jnp.sum(p2, axis=-1, keepdims=True)
        o = _dot(p1.astype(BF16), vn_ref[krows, :]) + _dot(p2.astype(BF16), vcn_ref[...])
        o_ref[qrows, :] = (o / l).astype(o_ref.dtype)
        return 0

    lax.fori_loop(0, n_pairs, pair, 0)

    qcn = (_rms_rows(qc_ref[...], qg) * scale).astype(BF16)
    s = _dot_nt(qcn, kcn)
    m = jnp.max(s, axis=-1, keepdims=True)
    p = jnp.exp(s - m)
    l = jnp.sum(p, axis=-1, keepdims=True)
    oc_ref[...] = (_dot(p.astype(BF16), vcn) / l).astype(oc_ref.dtype)


def na_attention(p, q_g, k_g, bias):
    hd = HEAD_DIM
    ctx_blk0 = N_LAT // CTX_LEN
    lat = lambda seg: pl.BlockSpec((SEQ, hd), lambda h, b: (b, seg * NA_HEADS + h))
    ctx = lambda seg: pl.BlockSpec((CTX_LEN, hd), lambda h, b: (ctx_blk0 + b, seg * NA_HEADS + h))
    return pl.pallas_call(
        _na_body,
        out_shape=(jax.ShapeDtypeStruct((N_LAT, NA_WIDTH), BF16),
                   jax.ShapeDtypeStruct((N_CTX, NA_WIDTH), BF16)),
        grid=(NA_HEADS, BATCH),
        in_specs=[lat(0), lat(1), lat(2), ctx(0), ctx(1), ctx(2),
                  pl.BlockSpec((1, hd), lambda h, b: (0, 0)),
                  pl.BlockSpec((1, hd), lambda h, b: (0, 0)),
                  pl.BlockSpec((1, len(NA_PAIR_KINDS), NA_PAIR_Q, NA_PAIR_K), lambda h, b: (h, 0, 0, 0))],
        out_specs=(pl.BlockSpec((SEQ, hd), lambda h, b: (b, h)),
                   pl.BlockSpec((CTX_LEN, hd), lambda h, b: (b, h))),
        scratch_shapes=[pltpu.VMEM((SEQ, hd), BF16), pltpu.VMEM((SEQ, hd), BF16), pltpu.VMEM((SEQ, hd), BF16),
                        pltpu.VMEM((CTX_LEN, hd), BF16), pltpu.VMEM((CTX_LEN, hd), BF16)],
        compiler_params=_cparams(("arbitrary", "arbitrary")),
        name="na_attention",
    )(p, p, p, p, p, p, q_g.reshape(1, hd), k_g.reshape(1, hd), bias)


HG_BLK = 128
HG_CPB = HG_BLK // HG_CHUNK
HG_CTX_BLKS = CTX_LEN // HG_BLK
HG_LAT_BLKS = SEQ // HG_BLK
HG_BLKS = HG_CTX_BLKS + HG_LAT_BLKS


def _chunk_scans(x, pos):
    pre = x
    suf = x
    d = 1
    while d < HG_CHUNK:
        pre = pre + jnp.where(pos >= d, pltpu.roll(pre, d, axis=0), 0.0)
        suf = suf + jnp.where(pos + d < HG_CHUNK, pltpu.roll(suf, HG_BLK - d, axis=0), 0.0)
        d *= 2
    return pre, suf


def _hg_body(q_ref, zf_ref, zb_ref, i_ref, g_ref, qc_ref, zfc_ref, zbc_ref, ic_ref, gc_ref,
             lbf_ref, lbb_ref, gn_ref, o_ref, oc_ref,
             qf_s, kf_s, df_s, gf_s, qb_s, kb_s, db_s, gb_s, v_s, vt_s, of_s, ob_s, sf_s, sb_s):
    lbf = lbf_ref[...]
    lbb = lbb_ref[...]
    pos = lax.broadcasted_iota(jnp.int32, (HG_BLK, HEAD_DIM), 0) & (HG_CHUNK - 1)

    def prep_block(blk, q, zf, zb, v):
        qs = q * jax.nn.sigmoid(q)
        ff = lbf + (1.0 - lbf) * jax.nn.sigmoid(zf)
        fb = lbb + (1.0 - lbb) * jax.nn.sigmoid(zb)
        lf = jnp.log(ff)
        lb_ = jnp.log(fb)
        pre_f, suf_f = _chunk_scans(lf, pos)
        pre_b, suf_b = _chunk_scans(lb_, pos)
        kf = 1.0 - ff
        kb = 1.0 - fb
        qf_s[blk] = (qs * jnp.exp(pre_f)).astype(BF16)
        kf_s[blk] = (kf * jnp.exp(-pre_f)).astype(BF16)
        df_s[blk] = (kf * jnp.exp(suf_f - lf)).astype(BF16)
        gf_s[blk] = jnp.exp(pre_f + suf_f - lf)
        qb_s[blk] = (qs * jnp.exp(suf_b)).astype(BF16)
        kb_s[blk] = (kb * jnp.exp(-suf_b)).astype(BF16)
        db_s[blk] = (kb * jnp.exp(pre_b - lb_)).astype(BF16)
        gb_s[blk] = jnp.exp(pre_b + suf_b - lb_)
        v_s[blk] = v.astype(BF16)
        vt_s[blk] = v.T

    for cb in range(HG_CTX_BLKS):
        rows = slice(cb * HG_BLK, (cb + 1) * HG_BLK)
        prep_block(cb, qc_ref[rows, :], zfc_ref[rows, :], zbc_ref[rows, :], ic_ref[rows, :])

    def prep_lat(n, _):
        rows = pl.ds(pl.multiple_of(n * HG_BLK, HG_BLK), HG_BLK)
        prep_block(HG_CTX_BLKS + n, q_ref[rows, :], zf_ref[rows, :], zb_ref[rows, :], i_ref[rows, :])
        return 0

    lax.fori_loop(0, HG_LAT_BLKS, prep_lat, 0)

    sf_s[...] = jnp.zeros_like(sf_s)
    sb_s[...] = jnp.zeros_like(sb_s)

    ri = lax.broadcasted_iota(jnp.int32, (HG_BLK, HG_BLK), 0)
    ci = lax.broadcasted_iota(jnp.int32, (HG_BLK, HG_BLK), 1)
    chunk_of_col = ci >> int(math.log2(HG_CHUNK))
    same_chunk = (ri >> int(math.log2(HG_CHUNK))) == chunk_of_col
    mask_f = same_chunk & (ci <= ri)
    mask_b = same_chunk & (ci >= ri)

    def scan_block(blk, q_s, k_s, d_s, g_s, s_ref, out_s, mask, reverse):
        qb = q_s[blk]
        vb = v_s[blk]
        vtb = vt_s[blk]
        att = jnp.where(mask, _dot_nt(qb, k_s[blk]), 0.0).astype(BF16)
        o_intra = _dot(att, vb)
        vt_masked = jnp.concatenate(
            [jnp.where(chunk_of_col == c, vtb, 0.0).astype(BF16) for c in range(HG_CPB)], axis=0)
        u_all = _dot(vt_masked, d_s[blk])
        order = range(HG_CPB - 1, -1, -1) if reverse else range(HG_CPB)
        for c in order:
            rows = slice(c * HG_CHUNK, (c + 1) * HG_CHUNK)
            st = s_ref[...]
            o_c = o_intra[rows, :] + _dot_nt(qb[rows, :], st.astype(BF16))
            out_s[blk, rows, :] = o_c
            g_c = g_s[blk, c * HG_CHUNK:c * HG_CHUNK + 1, :]
            s_ref[...] = g_c * st + u_all[c * HG_BLK:(c + 1) * HG_BLK, :]

    def scan_step(n, _):
        fblk = n
        bblk = jnp.where(n < HG_CTX_BLKS, HG_CTX_BLKS - 1 - n, HG_BLKS + HG_CTX_BLKS - 1 - n)
        scan_block(fblk, qf_s, kf_s, df_s, gf_s, sf_s, of_s, mask_f, False)
        scan_block(bblk, qb_s, kb_s, db_s, gb_s, sb_s, ob_s, mask_b, True)
        return 0

    lax.fori_loop(0, HG_BLKS, scan_step, 0)

    gn = gn_ref[...]

    def finish(blk, gate):
        o = of_s[blk] + ob_s[blk]
        return (_rms_rows(o, gn) * (gate * jax.nn.sigmoid(gate))).astype(BF16)

    for cb in range(HG_CTX_BLKS):
        rows = slice(cb * HG_BLK, (cb + 1) * HG_BLK)
        oc_ref[rows, :] = finish(cb, gc_ref[rows, :])

    def fin_lat(n, _):
        rows = pl.ds(pl.multiple_of(n * HG_BLK, HG_BLK), HG_BLK)
        o_ref[rows, :] = finish(HG_CTX_BLKS + n, g_ref[rows, :])
        return 0

    lax.fori_loop(0, HG_LAT_BLKS, fin_lat, 0)


def hgrn2(p, lb_fwd, lb_bwd, gn_g):
    hd = HEAD_DIM
    ctx_blk0 = N_LAT // CTX_LEN
    lat = lambda seg: pl.BlockSpec((SEQ, hd), lambda b, h: (b, seg * HG_HEADS + h))
    ctx = lambda seg: pl.BlockSpec((CTX_LEN, hd), lambda b, h: (ctx_blk0 + b, seg * HG_HEADS + h))
    head_vec = pl.BlockSpec((1, hd), lambda b, h: (0, h))
    blk_bf = lambda: pltpu.VMEM((HG_BLKS, HG_BLK, hd), BF16)
    blk_f32 = lambda: pltpu.VMEM((HG_BLKS, HG_BLK, hd), F32)
    return pl.pallas_call(
        _hg_body,
        out_shape=(jax.ShapeDtypeStruct((N_LAT, HG_HEADS * hd), BF16),
                   jax.ShapeDtypeStruct((N_CTX, HG_HEADS * hd), BF16)),
        grid=(BATCH, HG_HEADS),
        in_specs=[lat(3), lat(4), lat(5), lat(6), lat(7), ctx(3), ctx(4), ctx(5), ctx(6), ctx(7),
                  head_vec, head_vec, pl.BlockSpec((1, hd), lambda b, h: (0, 0))],
        out_specs=(pl.BlockSpec((SEQ, hd), lambda b, h: (b, h)),
                   pl.BlockSpec((CTX_LEN, hd), lambda b, h: (b, h))),
        scratch_shapes=[blk_bf(), blk_bf(), blk_bf(), blk_f32(),
                        blk_bf(), blk_bf(), blk_bf(), blk_f32(),
                        blk_bf(), blk_f32(), blk_f32(), blk_f32(),
                        pltpu.VMEM((hd, hd), F32), pltpu.VMEM((hd, hd), F32)],
        compiler_params=_cparams(("arbitrary", "arbitrary")),
        name="hgrn2",
    )(p, p, p, p, p, p, p, p, p, p, lb_fwd.reshape(1, -1), lb_bwd.reshape(1, -1), gn_g.reshape(1, hd))


DF_TQ = 256
DF_NK = SEQ + CTX_LEN


def _rope(x, cos2, sin2):
    return x * cos2 + pltpu.roll(x, DIFF_DK // 2, axis=1) * sin2


def _diff_body(lam_ref, q_ref, k_ref, kc_ref, v_ref, vc_ref, cosq_ref, sinq_ref, cosk_ref, sink_ref,
               qg_ref, kg_ref, sg_ref, o_ref, k0_s, k1_s, v_s, *, out_scale):
    dk = DIFF_DK
    qg = qg_ref[...]
    kg = kg_ref[...]

    @pl.when(pl.program_id(2) == 0)
    def _():
        def prep(i, _):
            rows = pl.ds(pl.multiple_of(i * 256, 256), 256)
            c2 = cosk_ref[rows, :]
            s2 = sink_ref[rows, :]
            kk = k_ref[rows, :]
            k0_s[rows, :] = _rope(_rms_rows(kk[:, :dk], kg), c2, s2).astype(BF16)
            k1_s[rows, :] = _rope(_rms_rows(kk[:, dk:], kg), c2, s2).astype(BF16)
            v_s[rows, :] = v_ref[rows, :].astype(BF16)
            return 0

        lax.fori_loop(0, SEQ // 256, prep, 0)
        kc = kc_ref[...]
        k0_s[SEQ:, :] = _rms_rows(kc[:, :dk], kg).astype(BF16)
        k1_s[SEQ:, :] = _rms_rows(kc[:, dk:], kg).astype(BF16)
        v_s[SEQ:, :] = vc_ref[...].astype(BF16)

    scale = dk ** -0.5
    lam = lam_ref[0, 0]
    q = q_ref[...]
    c2 = cosq_ref[...]
    s2 = sinq_ref[...]
    q0 = (_rope(_rms_rows(q[:, :dk], qg), c2, s2) * scale).astype(BF16)
    q1 = (_rope(_rms_rows(q[:, dk:], qg), c2, s2) * scale).astype(BF16)

    def probs(qn, k_s):
        s = _dot_nt(qn, k_s[...])
        e = jnp.exp(s - jnp.max(s, axis=-1, keepdims=True))
        return e, jnp.sum(e, axis=-1, keepdims=True)

    e0, l0 = probs(q0, k0_s)
    e1, l1 = probs(q1, k1_s)
    a = (e0 * (1.0 / l0) - e1 * (lam / l1)).astype(BF16)
    o = _dot(a, v_s[...])
    o_ref[...] = (_rms_rows(o, sg_ref[...]) * out_scale).astype(o_ref.dtype)


def diff_attention(p, lam, cos2, sin2, q_g, k_g, subln_g, out_scale):
    nq = SEQ // DF_TQ
    w = 2 * DIFF_DK
    ctx_blk0 = N_LAT // CTX_LEN
    kcol0 = D_MODEL // w
    vcol0 = 2 * D_MODEL // w
    vec = lambda n: pl.BlockSpec((1, n), lambda b, h, i: (0, 0))
    return pl.pallas_call(
        functools.partial(_diff_body, out_scale=out_scale),
        out_shape=jax.ShapeDtypeStruct((N_LAT, D_MODEL), BF16),
        grid=(BATCH, DIFF_HEADS, nq),
        in_specs=[
            pl.BlockSpec(memory_space=pltpu.SMEM),
            pl.BlockSpec((DF_TQ, w), lambda b, h, i: (b * nq + i, h)),
            pl.BlockSpec((SEQ, w), lambda b, h, i: (b, kcol0 + h)),
            pl.BlockSpec((CTX_LEN, w), lambda b, h, i: (ctx_blk0 + b, kcol0 + h)),
            pl.BlockSpec((SEQ, w), lambda b, h, i: (b, vcol0 + h)),
            pl.BlockSpec((CTX_LEN, w), lambda b, h, i: (ctx_blk0 + b, vcol0 + h)),
            pl.BlockSpec((DF_TQ, DIFF_DK), lambda b, h, i: (i, 0)),
            pl.BlockSpec((DF_TQ, DIFF_DK), lambda b, h, i: (i, 0)),
            pl.BlockSpec((SEQ, DIFF_DK), lambda b, h, i: (0, 0)),
            pl.BlockSpec((SEQ, DIFF_DK), lambda b, h, i: (0, 0)),
            vec(DIFF_DK), vec(DIFF_DK), vec(DIFF_DV),
        ],
        out_specs=pl.BlockSpec((DF_TQ, w), lambda b, h, i: (b * nq + i, h)),
        scratch_shapes=[pltpu.VMEM((DF_NK, DIFF_DK), BF16), pltpu.VMEM((DF_NK, DIFF_DK), BF16),
                        pltpu.VMEM((DF_NK, DIFF_DV), BF16)],
        compiler_params=_cparams(("arbitrary", "arbitrary", "arbitrary")),
        name="diff_attention",
    )(lam, p, p, p, p, p, cos2, sin2, cos2, sin2,
      q_g.reshape(1, DIFF_DK), k_g.reshape(1, DIFF_DK), subln_g.reshape(1, DIFF_DV))


def _rope_tables():
    t = jnp.arange(SEQ)
    row = (t // GRID_W).astype(F32)
    col = (t % GRID_W).astype(F32)
    n_freq = DIFF_DK // 4
    inv = ROPE_THETA ** (-jnp.arange(n_freq, dtype=F32) / n_freq)
    ang = jnp.concatenate([row[:, None] * inv, col[:, None] * inv], axis=-1)
    cos, sin = jnp.cos(ang), jnp.sin(ang)
    return jnp.concatenate([cos, cos], axis=-1), jnp.concatenate([-sin, sin], axis=-1)


def _mlp(xs, mod, g, w1, w2, n_rows):
    h2 = norm_modulate(xs, g, mod, 3, n_rows)
    a = matmul(h2, w1, n_rows=n_rows, out_dtype=BF16, epi="relu2", name="mlp_up")
    return matmul(a, w2, n_rows=n_rows, out_dtype=F32, epi="res", res=xs, mod=mod, gate_idx=5, name="mlp_down")


def kernel(x, c, ctx, c_ctx, ada_w, ada_b, norm_mix_g, norm_mlp_g, mlp_w1, mlp_w2, ev_w_in, ev_w_out, na_q_g, na_k_g, na_rpb, hg_lb_logits, hg_gnorm_g, od_w_in, od_w_out, df_q_g, df_k_g, df_lambda, df_subln_g):
    xs = jnp.concatenate([x.reshape(N_LAT, D_MODEL), ctx.reshape(N_CTX, D_MODEL)], axis=0)
    cond = jnp.concatenate([c, c_ctx[None], jnp.zeros((MOD_ROWS - BATCH - 1, D_MODEL), F32)], axis=0)
    mods = ada_table(cond, ada_w, ada_b)
    lb_all = jnp.cumsum(jax.nn.softmax(hg_lb_logits.astype(F32), axis=1), axis=1)
    cos2, sin2 = _rope_tables()

    for l in range(DEPTH):
        last = l == DEPTH - 1
        mod = mods[l]
        h = norm_modulate(xs, norm_mix_g[l], mod, 0, N_TOK)
        n_out = N_LAT if last else N_TOK
        if l % 2 == 0:
            e = l // 2
            p = matmul(h, ev_w_in[e].astype(BF16), n_rows=N_TOK, out_dtype=F32, name="even_in")
            na_lat, na_ctx = na_attention(p, na_q_g[e], na_k_g[e], na_bias_table(na_rpb[e]))
            hg_lat, hg_ctx = hgrn2(p, lb_all[0, l], lb_all[1, l], hg_gnorm_g[e])
            om = jnp.concatenate([jnp.concatenate([na_lat, hg_lat], axis=1),
                                  jnp.concatenate([na_ctx, hg_ctx], axis=1)], axis=0)
            w_out = ev_w_out[e]
        else:
            assert last, "an odd layer followed by another layer needs context outputs"
            o = l // 2
            p = matmul(h, od_w_in[o].astype(BF16), n_rows=N_TOK, out_dtype=F32, name="odd_in")
            lam_init = 0.8 - 0.6 * math.exp(-0.3 * l)
            lp = df_lambda[o].astype(F32)
            lam = jnp.exp(jnp.sum(lp[0] * lp[1])) - jnp.exp(jnp.sum(lp[2] * lp[3])) + lam_init
            om = diff_attention(p, lam.reshape(1, 1), cos2, sin2, df_q_g[o], df_k_g[o], df_subln_g[o],
                                1.0 - lam_init)
            w_out = od_w_out[o]
        xs = matmul(om, w_out.astype(BF16), n_rows=n_out, out_dtype=F32, epi="res", res=xs, mod=mod,
                    gate_idx=2, name="mix_out")
        xs = _mlp(xs, mod, norm_mlp_g[l], mlp_w1[l].astype(BF16), mlp_w2[l].astype(BF16), n_out)
    return xs.reshape(BATCH, SEQ, D_MODEL)
```

```python
import functools
import math

import jax
import jax.numpy as jnp
import numpy as np
from jax import lax
from jax.experimental import pallas as pl
from jax.experimental.pallas import tpu as pltpu

F32 = jnp.float32
BF16 = jnp.bfloat16

D_MODEL = 2048
BATCH = 4
SEQ = 2048
DEPTH = 2
GRID_W = 64
GRID_ROWS = SEQ // GRID_W
CTX_LEN = 256
HEAD_DIM = 128
NA_HEADS = 8
NA_WIN_R = 8
NA_WIN_C = 16
NA_WIDTH = NA_HEADS * HEAD_DIM
HG_HEADS = 8
HG_CHUNK = 16
HG_WIDTH = HG_HEADS * HEAD_DIM
EVEN_IN = 8 * NA_WIDTH
DIFF_HEADS = 8
DIFF_DK = HEAD_DIM
DIFF_DV = 2 * HEAD_DIM
ODD_IN = 3 * D_MODEL
D_FF = 4 * D_MODEL
N_MOD = 6
ROPE_THETA = 10000.0
EPS = 1e-6

N_LAT = BATCH * SEQ
N_CTX = BATCH * CTX_LEN
MOD_ROWS = 8
CTX_MOD_ROW = BATCH
NEG_BIG = -1e30

VMEM_LIMIT_V7X = 56 * 1024 * 1024
MM_TILE = 1024
MM_TK = 2048


def _cparams(sem, vmem=VMEM_LIMIT_V7X):
    return pltpu.CompilerParams(dimension_semantics=sem, vmem_limit_bytes=vmem)


def _dot(a, b):
    return jnp.dot(a, b, preferred_element_type=F32)


def _dot_nt(a, b):
    return lax.dot_general(a, b, (((1,), (1,)), ((), ())), preferred_element_type=F32)


def _rms_rows(x, g):
    return x * lax.rsqrt(jnp.mean(x * x, axis=-1, keepdims=True) + EPS) * g


def _mod_row_of_tile(is_ctx, tile_rows):
    if is_ctx:
        return lambda i: CTX_MOD_ROW
    return lambda i: (i * tile_rows) // SEQ


def _ada_body(c_ref, w_ref, b_ref, o_ref):
    c = c_ref[...]
    s = (c * jax.nn.sigmoid(c)).astype(BF16)
    o_ref[0] = _dot(s, w_ref[0].astype(BF16)) + b_ref[0]


def ada_table(cond, ada_w, ada_b):
    tn = 1024
    n = N_MOD * D_MODEL
    out = pl.pallas_call(
        _ada_body,
        out_shape=jax.ShapeDtypeStruct((DEPTH, MOD_ROWS, n), F32),
        grid=(DEPTH, n // tn),
        in_specs=[
            pl.BlockSpec((MOD_ROWS, D_MODEL), lambda l, j: (0, 0)),
            pl.BlockSpec((1, D_MODEL, tn), lambda l, j: (l, 0, j)),
            pl.BlockSpec((1, 1, tn), lambda l, j: (l, 0, j)),
        ],
        out_specs=pl.BlockSpec((1, MOD_ROWS, tn), lambda l, j: (l, 0, j)),
        compiler_params=_cparams(("arbitrary", "arbitrary")),
        name="ada_table",
    )(cond, ada_w, ada_b.reshape(DEPTH, 1, n))
    return out.reshape(DEPTH, MOD_ROWS, N_MOD, D_MODEL)


def _norm_mm_body(x_ref, g_ref, mod_ref, w_ref, o_ref, h_ref, *, shift_idx, relu2):
    @pl.when(pl.program_id(1) == 0)
    def _():
        shift = mod_ref[0, shift_idx:shift_idx + 1, :]
        scale = mod_ref[0, shift_idx + 1:shift_idx + 2, :]
        h_ref[...] = (_rms_rows(x_ref[...], g_ref[...]) * (1.0 + scale) + shift).astype(BF16)

    acc = _dot(h_ref[...], w_ref[...])
    if relu2:
        acc = jnp.maximum(acc, 0.0)
        acc = acc * acc
    o_ref[...] = acc.astype(o_ref.dtype)


def norm_matmul(x, g, mod, w, *, shift_idx, is_ctx, out_dtype, relu2=False, col_block0=0, n_cols=None, name):
    n_rows = x.shape[0]
    tm = tn = MM_TILE
    n_cols = w.shape[1] if n_cols is None else n_cols
    mod_row = _mod_row_of_tile(is_ctx, tm)
    return pl.pallas_call(
        functools.partial(_norm_mm_body, shift_idx=shift_idx, relu2=relu2),
        out_shape=jax.ShapeDtypeStruct((n_rows, n_cols), out_dtype),
        grid=(n_rows // tm, n_cols // tn),
        in_specs=[
            pl.BlockSpec((tm, D_MODEL), lambda i, j: (i, 0)),
            pl.BlockSpec((1, D_MODEL), lambda i, j: (0, 0)),
            pl.BlockSpec((1, N_MOD, D_MODEL), lambda i, j: (mod_row(i), 0, 0)),
            pl.BlockSpec((D_MODEL, tn), lambda i, j: (0, col_block0 + j)),
        ],
        out_specs=pl.BlockSpec((tm, tn), lambda i, j: (i, j)),
        scratch_shapes=[pltpu.VMEM((tm, D_MODEL), BF16)],
        compiler_params=_cparams(("arbitrary", "arbitrary")),
        name=name,
    )(x, g.reshape(1, D_MODEL), mod, w)


def _res_mm_body(*refs, n_x, nk, gate_idx):
    x_refs = refs[:n_x]
    w_ref, res_ref, mod_ref, o_ref = refs[n_x:n_x + 4]
    scratch = refs[n_x + 4:]

    def partial_product():
        acc = None
        k0 = 0
        for x_ref in x_refs:
            kw = x_ref.shape[1]
            d = _dot(x_ref[...], w_ref[k0:k0 + kw, :])
            acc = d if acc is None else acc + d
            k0 += kw
        return acc

    def finish(acc):
        gate = mod_ref[0, gate_idx:gate_idx + 1, :]
        o_ref[...] = res_ref[...] + gate * acc

    if nk == 1:
        finish(partial_product())
    else:
        acc_ref = scratch[0]
        k = pl.program_id(2)

        @pl.when(k == 0)
        def _():
            acc_ref[...] = jnp.zeros_like(acc_ref)

        acc_ref[...] += partial_product()

        @pl.when(k == nk - 1)
        def _():
            finish(acc_ref[...])


def res_matmul(xs, w, res, mod, *, gate_idx, is_ctx, name):
    n_rows = res.shape[0]
    k_dim, n_dim = w.shape
    tm = tn = MM_TILE
    if len(xs) == 1:
        tk = MM_TK
        x_specs = [pl.BlockSpec((tm, tk), lambda i, j, k: (i, k))]
    else:
        tk = k_dim
        x_specs = [pl.BlockSpec((tm, x.shape[1]), lambda i, j, k: (i, 0)) for x in xs]
    nk = k_dim // tk
    mod_row = _mod_row_of_tile(is_ctx, tm)
    return pl.pallas_call(
        functools.partial(_res_mm_body, n_x=len(xs), nk=nk, gate_idx=gate_idx),
        out_shape=jax.ShapeDtypeStruct((n_rows, n_dim), F32),
        grid=(n_rows // tm, n_dim // tn, nk),
        in_specs=x_specs + [
            pl.BlockSpec((tk, tn), lambda i, j, k: (k, j)),
            pl.BlockSpec((tm, tn), lambda i, j, k: (i, j)),
            pl.BlockSpec((1, N_MOD, tn), lambda i, j, k: (mod_row(i), 0, j)),
        ],
        out_specs=pl.BlockSpec((tm, tn), lambda i, j, k: (i, j)),
        scratch_shapes=[pltpu.VMEM((tm, tn), F32)] if nk > 1 else [],
        compiler_params=_cparams(("arbitrary", "arbitrary", "arbitrary")),
        name=name,
    )(*xs, w, res, mod)


NA_PAIR_Q = 2 * GRID_W
NA_PAIR_KROWS = 10
NA_PAIR_K = NA_PAIR_KROWS * GRID_W
NA_PAIR_KINDS = (0, 2, 4, GRID_ROWS - 4, GRID_ROWS - 2)
NA_PAIR_UNROLL = 4


def _na_pair_key_row0(r):
    return min(max(r - NA_WIN_R // 2, 0), GRID_ROWS - NA_PAIR_KROWS)


def na_bias_table(rpb):
    w = GRID_W
    col = np.arange(w)
    c0 = np.clip(col - NA_WIN_C // 2, 0, w - NA_WIN_C)
    in_win = (col[None, :] >= c0[:, None]) & (col[None, :] < c0[:, None] + NA_WIN_C)
    padded = jnp.pad(rpb.astype(F32), ((0, 0), (0, 0), (w, w)))
    toep = jnp.stack([padded[:, :, NA_WIN_C - 1 - qc + w:NA_WIN_C - 1 - qc + 2 * w] for qc in range(w)], axis=2)
    toep = jnp.where(in_win[None, None], toep, NEG_BIG)
    masked = jnp.full((NA_HEADS, w, w), NEG_BIG, F32)
    kinds = []
    for r in NA_PAIR_KINDS:
        q_rows = []
        for rq in (r, r + 1):
            r0 = min(max(rq - NA_WIN_R // 2, 0), GRID_ROWS - NA_WIN_R)
            blocks = []
            for kr in range(_na_pair_key_row0(r), _na_pair_key_row0(r) + NA_PAIR_KROWS):
                blocks.append(toep[:, kr - rq + NA_WIN_R - 1] if r0 <= kr < r0 + NA_WIN_R else masked)
            q_rows.append(jnp.concatenate(blocks, axis=2))
        kinds.append(jnp.concatenate(q_rows, axis=1))
    return jnp.stack(kinds, axis=1)


def _na_body(q_ref, k_ref, v_ref, qc_ref, kc_ref, vc_ref, qg_ref, kg_ref, bias_ref,
             o_ref, oc_ref, qn_ref, kn_ref, vn_ref, kcn_ref, vcn_ref):
    scale = HEAD_DIM ** -0.5
    qg = qg_ref[...]
    kg = kg_ref[...]

    def prep(i, _):
        rows = pl.ds(pl.multiple_of(i * 256, 256), 256)
        qn_ref[rows, :] = (_rms_rows(q_ref[rows, :], qg) * scale).astype(BF16)
        kn_ref[rows, :] = _rms_rows(k_ref[rows, :], kg).astype(BF16)
        vn_ref[rows, :] = v_ref[rows, :].astype(BF16)
        return 0

    lax.fori_loop(0, SEQ // 256, prep, 0)
    kcn = _rms_rows(kc_ref[...], kg).astype(BF16)
    vcn = vc_ref[...].astype(BF16)
    kcn_ref[...] = kcn
    vcn_ref[...] = vcn

    n_pairs = GRID_ROWS // 2

    def pair(rp, _):
        r = 2 * rp
        u0 = jnp.clip(r - NA_WIN_R // 2, 0, GRID_ROWS - NA_PAIR_KROWS)
        kind = jnp.where(rp < 2, rp, jnp.where(rp >= n_pairs - 2, rp - (n_pairs - 5), 2))
        qrows = pl.ds(pl.multiple_of(rp * NA_PAIR_Q, NA_PAIR_Q), NA_PAIR_Q)
        krows = pl.ds(pl.multiple_of(u0 * GRID_W, 128), NA_PAIR_K)
        qb = qn_ref[qrows, :]
        s1 = _dot_nt(qb, kn_ref[krows, :]) + bias_ref[0, kind]
        s2 = _dot_nt(qb, kcn_ref[...])
        m = jnp.maximum(jnp.max(s1, axis=-1, keepdims=True), jnp.max(s2, axis=-1, keepdims=True))
        p1 = jnp.exp(s1 - m)
        p2 = jnp.exp(s2 - m)
        l = jnp.sum(p1, axis=-1, keepdims=True) + jnp.sum(p2, axis=-1, keepdims=True)
        o = _dot(p1.astype(BF16), vn_ref[krows, :]) + _dot(p2.astype(BF16), vcn_ref[...])
        o_ref[qrows, :] = (o / l).astype(o_ref.dtype)
        return 0

    lax.fori_loop(0, n_pairs, pair, 0, unroll=NA_PAIR_UNROLL)

    qcn = (_rms_rows(qc_ref[...], qg) * scale).astype(BF16)
    s = _dot_nt(qcn, kcn)
    m = jnp.max(s, axis=-1, keepdims=True)
    p = jnp.exp(s - m)
    l = jnp.sum(p, axis=-1, keepdims=True)
    oc_ref[...] = (_dot(p.astype(BF16), vcn) / l).astype(oc_ref.dtype)


def na_attention(p_lat, p_ctx, q_g, k_g, bias):
    hd = HEAD_DIM
    lat = lambda seg: pl.BlockSpec((SEQ, hd), lambda h, b: (b, seg * NA_HEADS + h))
    ctx = lambda seg: pl.BlockSpec((CTX_LEN, hd), lambda h, b: (b, seg * NA_HEADS + h))
    return pl.pallas_call(
        _na_body,
        out_shape=(jax.ShapeDtypeStruct((N_LAT, NA_WIDTH), BF16),
                   jax.ShapeDtypeStruct((N_CTX, NA_WIDTH), BF16)),
        grid=(NA_HEADS, BATCH),
        in_specs=[lat(0), lat(1), lat(2), ctx(0), ctx(1), ctx(2),
                  pl.BlockSpec((1, hd), lambda h, b: (0, 0)),
                  pl.BlockSpec((1, hd), lambda h, b: (0, 0)),
                  pl.BlockSpec((1, len(NA_PAIR_KINDS), NA_PAIR_Q, NA_PAIR_K), lambda h, b: (h, 0, 0, 0))],
        out_specs=(pl.BlockSpec((SEQ, hd), lambda h, b: (b, h)),
                   pl.BlockSpec((CTX_LEN, hd), lambda h, b: (b, h))),
        scratch_shapes=[pltpu.VMEM((SEQ, hd), BF16), pltpu.VMEM((SEQ, hd), BF16), pltpu.VMEM((SEQ, hd), BF16),
                        pltpu.VMEM((CTX_LEN, hd), BF16), pltpu.VMEM((CTX_LEN, hd), BF16)],
        compiler_params=_cparams(("arbitrary", "arbitrary")),
        name="na_attention",
    )(p_lat, p_lat, p_lat, p_ctx, p_ctx, p_ctx, q_g.reshape(1, hd), k_g.reshape(1, hd), bias)


HG_BLK = 128
HG_CPB = HG_BLK // HG_CHUNK
HG_CTX_BLKS = CTX_LEN // HG_BLK
HG_LAT_BLKS = SEQ // HG_BLK
HG_BLKS = HG_CTX_BLKS + HG_LAT_BLKS


def _chunk_scans(x, pos):
    pre = x
    suf = x
    d = 1
    while d < HG_CHUNK:
        pre = pre + jnp.where(pos >= d, pltpu.roll(pre, d, axis=0), 0.0)
        suf = suf + jnp.where(pos + d < HG_CHUNK, pltpu.roll(suf, HG_BLK - d, axis=0), 0.0)
        d *= 2
    return pre, suf


def _hg_body(q_ref, zf_ref, zb_ref, i_ref, g_ref, qc_ref, zfc_ref, zbc_ref, ic_ref, gc_ref,
             lbf_ref, lbb_ref, gn_ref, o_ref, oc_ref,
             qf_s, kf_s, df_s, gf_s, qb_s, kb_s, db_s, gb_s, v_s, vt_s, of_s, ob_s, sf_s, sb_s):
    lbf = lbf_ref[...]
    lbb = lbb_ref[...]
    pos = lax.broadcasted_iota(jnp.int32, (HG_BLK, HEAD_DIM), 0) & (HG_CHUNK - 1)

    def prep_block(blk, q, zf, zb, v):
        qs = q * jax.nn.sigmoid(q)
        ff = lbf + (1.0 - lbf) * jax.nn.sigmoid(zf)
        fb = lbb + (1.0 - lbb) * jax.nn.sigmoid(zb)
        lf = jnp.log(ff)
        lb_ = jnp.log(fb)
        pre_f, suf_f = _chunk_scans(lf, pos)
        pre_b, suf_b = _chunk_scans(lb_, pos)
        kf = 1.0 - ff
        kb = 1.0 - fb
        qf_s[blk] = (qs * jnp.exp(pre_f)).astype(BF16)
        kf_s[blk] = (kf * jnp.exp(-pre_f)).astype(BF16)
        df_s[blk] = (kf * jnp.exp(suf_f - lf)).astype(BF16)
        gf_s[blk] = jnp.exp(pre_f + suf_f - lf)
        qb_s[blk] = (qs * jnp.exp(suf_b)).astype(BF16)
        kb_s[blk] = (kb * jnp.exp(-suf_b)).astype(BF16)
        db_s[blk] = (kb * jnp.exp(pre_b - lb_)).astype(BF16)
        gb_s[blk] = jnp.exp(pre_b + suf_b - lb_)
        v_s[blk] = v.astype(BF16)
        vt_s[blk] = v.T

    for cb in range(HG_CTX_BLKS):
        rows = slice(cb * HG_BLK, (cb + 1) * HG_BLK)
        prep_block(cb, qc_ref[rows, :], zfc_ref[rows, :], zbc_ref[rows, :], ic_ref[rows, :])

    def prep_lat(n, _):
        rows = pl.ds(pl.multiple_of(n * HG_BLK, HG_BLK), HG_BLK)
        prep_block(HG_CTX_BLKS + n, q_ref[rows, :], zf_ref[rows, :], zb_ref[rows, :], i_ref[rows, :])
        return 0

    lax.fori_loop(0, HG_LAT_BLKS, prep_lat, 0)

    sf_s[...] = jnp.zeros_like(sf_s)
    sb_s[...] = jnp.zeros_like(sb_s)

    ri = lax.broadcasted_iota(jnp.int32, (HG_BLK, HG_BLK), 0)
    ci = lax.broadcasted_iota(jnp.int32, (HG_BLK, HG_BLK), 1)
    chunk_of_col = ci >> int(math.log2(HG_CHUNK))
    same_chunk = (ri >> int(math.log2(HG_CHUNK))) == chunk_of_col
    mask_f = same_chunk & (ci <= ri)
    mask_b = same_chunk & (ci >= ri)

    def scan_block(blk, q_s, k_s, d_s, g_s, s_ref, out_s, mask, reverse):
        qb = q_s[blk]
        vb = v_s[blk]
        vtb = vt_s[blk]
        att = jnp.where(mask, _dot_nt(qb, k_s[blk]), 0.0).astype(BF16)
        o_intra = _dot(att, vb)
        vt_masked = jnp.concatenate(
            [jnp.where(chunk_of_col == c, vtb, 0.0).astype(BF16) for c in range(HG_CPB)], axis=0)
        u_all = _dot(vt_masked, d_s[blk])
        order = range(HG_CPB - 1, -1, -1) if reverse else range(HG_CPB)
        for c in order:
            rows = slice(c * HG_CHUNK, (c + 1) * HG_CHUNK)
            st = s_ref[...]
            o_c = o_intra[rows, :] + _dot_nt(qb[rows, :], st.astype(BF16))
            out_s[blk, rows, :] = o_c
            g_c = g_s[blk, c * HG_CHUNK:c * HG_CHUNK + 1, :]
            s_ref[...] = g_c * st + u_all[c * HG_BLK:(c + 1) * HG_BLK, :]

    def scan_step(n, _):
        fblk = n
        bblk = jnp.where(n < HG_CTX_BLKS, HG_CTX_BLKS - 1 - n, HG_BLKS + HG_CTX_BLKS - 1 - n)
        scan_block(fblk, qf_s, kf_s, df_s, gf_s, sf_s, of_s, mask_f, False)
        scan_block(bblk, qb_s, kb_s, db_s, gb_s, sb_s, ob_s, mask_b, True)
        return 0

    lax.fori_loop(0, HG_BLKS, scan_step, 0)

    gn = gn_ref[...]

    def finish(blk, gate):
        o = of_s[blk] + ob_s[blk]
        return (_rms_rows(o, gn) * (gate * jax.nn.sigmoid(gate))).astype(BF16)

    for cb in range(HG_CTX_BLKS):
        rows = slice(cb * HG_BLK, (cb + 1) * HG_BLK)
        oc_ref[rows, :] = finish(cb, gc_ref[rows, :])

    def fin_lat(n, _):
        rows = pl.ds(pl.multiple_of(n * HG_BLK, HG_BLK), HG_BLK)
        o_ref[rows, :] = finish(HG_CTX_BLKS + n, g_ref[rows, :])
        return 0

    lax.fori_loop(0, HG_LAT_BLKS, fin_lat, 0)


def hgrn2(p_lat, p_ctx, lb_fwd, lb_bwd, gn_g):
    hd = HEAD_DIM
    lat = lambda seg: pl.BlockSpec((SEQ, hd), lambda b, h: (b, seg * HG_HEADS + h))
    ctx = lambda seg: pl.BlockSpec((CTX_LEN, hd), lambda b, h: (b, seg * HG_HEADS + h))
    head_vec = pl.BlockSpec((1, hd), lambda b, h: (0, h))
    blk_bf = lambda: pltpu.VMEM((HG_BLKS, HG_BLK, hd), BF16)
    blk_f32 = lambda: pltpu.VMEM((HG_BLKS, HG_BLK, hd), F32)
    return pl.pallas_call(
        _hg_body,
        out_shape=(jax.ShapeDtypeStruct((N_LAT, HG_WIDTH), BF16),
                   jax.ShapeDtypeStruct((N_CTX, HG_WIDTH), BF16)),
        grid=(BATCH, HG_HEADS),
        in_specs=[lat(3), lat(4), lat(5), lat(6), lat(7), ctx(3), ctx(4), ctx(5), ctx(6), ctx(7),
                  head_vec, head_vec, pl.BlockSpec((1, hd), lambda b, h: (0, 0))],
        out_specs=(pl.BlockSpec((SEQ, hd), lambda b, h: (b, h)),
                   pl.BlockSpec((CTX_LEN, hd), lambda b, h: (b, h))),
        scratch_shapes=[blk_bf(), blk_bf(), blk_bf(), blk_f32(),
                        blk_bf(), blk_bf(), blk_bf(), blk_f32(),
                        blk_bf(), blk_f32(), blk_f32(), blk_f32(),
                        pltpu.VMEM((hd, hd), F32), pltpu.VMEM((hd, hd), F32)],
        compiler_params=_cparams(("arbitrary", "arbitrary")),
        name="hgrn2",
    )(p_lat, p_lat, p_lat, p_lat, p_lat, p_ctx, p_ctx, p_ctx, p_ctx, p_ctx,
      lb_fwd.reshape(1, -1), lb_bwd.reshape(1, -1), gn_g.reshape(1, hd))


DF_TQ = 256
DF_NK = SEQ + CTX_LEN
DF_TK = 768
DF_KTILES = DF_NK // DF_TK


def _rope(x, cos2, sin2):
    return x * cos2 + pltpu.roll(x, DIFF_DK // 2, axis=1) * sin2


def _diff_body(lam_ref, q_ref, k_ref, kc_ref, v_ref, vc_ref, cosq_ref, sinq_ref, cosk_ref, sink_ref,
               qg_ref, kg_ref, sg_ref, o_ref, k0_s, k1_s, v_s, *, out_scale):
    dk = DIFF_DK
    qg = qg_ref[...]
    kg = kg_ref[...]

    @pl.when(pl.program_id(2) == 0)
    def _():
        def prep(i, _):
            rows = pl.ds(pl.multiple_of(i * 256, 256), 256)
            c2 = cosk_ref[rows, :]
            s2 = sink_ref[rows, :]
            kk = k_ref[rows, :]
            k0_s[rows, :] = _rope(_rms_rows(kk[:, :dk], kg), c2, s2).astype(BF16)
            k1_s[rows, :] = _rope(_rms_rows(kk[:, dk:], kg), c2, s2).astype(BF16)
            v_s[rows, :] = v_ref[rows, :].astype(BF16)
            return 0

        lax.fori_loop(0, SEQ // 256, prep, 0)
        kc = kc_ref[...]
        k0_s[SEQ:, :] = _rms_rows(kc[:, :dk], kg).astype(BF16)
        k1_s[SEQ:, :] = _rms_rows(kc[:, dk:], kg).astype(BF16)
        v_s[SEQ:, :] = vc_ref[...].astype(BF16)

    scale = dk ** -0.5
    lam = lam_ref[0, 0]
    q = q_ref[...]
    c2 = cosq_ref[...]
    s2 = sinq_ref[...]
    qn = ((_rope(_rms_rows(q[:, :dk], qg), c2, s2) * scale).astype(BF16),
          (_rope(_rms_rows(q[:, dk:], qg), c2, s2) * scale).astype(BF16))
    k_s = (k0_s, k1_s)

    m = [None, None]
    l = [None, None]
    acc = [None, None]
    for t in range(DF_KTILES):
        rows = slice(t * DF_TK, (t + 1) * DF_TK)
        vt = v_s[rows, :]
        for i in range(2):
            s = _dot_nt(qn[i], k_s[i][rows, :])
            tile_max = jnp.max(s, axis=-1, keepdims=True)
            if t == 0:
                m[i] = tile_max
                e = jnp.exp(s - m[i])
                l[i] = jnp.sum(e, axis=-1, keepdims=True)
                acc[i] = _dot(e.astype(BF16), vt)
            else:
                m_new = jnp.maximum(m[i], tile_max)
                alpha = jnp.exp(m[i] - m_new)
                e = jnp.exp(s - m_new)
                l[i] = alpha * l[i] + jnp.sum(e, axis=-1, keepdims=True)
                acc[i] = alpha * acc[i] + _dot(e.astype(BF16), vt)
                m[i] = m_new
    o = acc[0] * (1.0 / l[0]) - acc[1] * (lam / l[1])
    o_ref[...] = (_rms_rows(o, sg_ref[...]) * out_scale).astype(o_ref.dtype)


def diff_attention(p_lat, p_ctx_kv, lam, cos2, sin2, q_g, k_g, subln_g, out_scale):
    nq = SEQ // DF_TQ
    w = 2 * DIFF_DK
    kcol0 = D_MODEL // w
    vcol0 = 2 * D_MODEL // w
    vec = lambda n: pl.BlockSpec((1, n), lambda b, h, i: (0, 0))
    return pl.pallas_call(
        functools.partial(_diff_body, out_scale=out_scale),
        out_shape=jax.ShapeDtypeStruct((N_LAT, D_MODEL), BF16),
        grid=(BATCH, DIFF_HEADS, nq),
        in_specs=[
            pl.BlockSpec(memory_space=pltpu.SMEM),
            pl.BlockSpec((DF_TQ, w), lambda b, h, i: (b * nq + i, h)),
            pl.BlockSpec((SEQ, w), lambda b, h, i: (b, kcol0 + h)),
            pl.BlockSpec((CTX_LEN, w), lambda b, h, i: (b, h)),
            pl.BlockSpec((SEQ, w), lambda b, h, i: (b, vcol0 + h)),
            pl.BlockSpec((CTX_LEN, w), lambda b, h, i: (b, kcol0 + h)),
            pl.BlockSpec((DF_TQ, DIFF_DK), lambda b, h, i: (i, 0)),
            pl.BlockSpec((DF_TQ, DIFF_DK), lambda b, h, i: (i, 0)),
            pl.BlockSpec((SEQ, DIFF_DK), lambda b, h, i: (0, 0)),
            pl.BlockSpec((SEQ, DIFF_DK), lambda b, h, i: (0, 0)),
            vec(DIFF_DK), vec(DIFF_DK), vec(DIFF_DV),
        ],
        out_specs=pl.BlockSpec((DF_TQ, w), lambda b, h, i: (b * nq + i, h)),
        scratch_shapes=[pltpu.VMEM((DF_NK, DIFF_DK), BF16), pltpu.VMEM((DF_NK, DIFF_DK), BF16),
                        pltpu.VMEM((DF_NK, DIFF_DV), BF16)],
        compiler_params=_cparams(("arbitrary", "arbitrary", "arbitrary")),
        name="diff_attention",
    )(lam, p_lat, p_lat, p_ctx_kv, p_lat, p_ctx_kv, cos2, sin2, cos2, sin2,
      q_g.reshape(1, DIFF_DK), k_g.reshape(1, DIFF_DK), subln_g.reshape(1, DIFF_DV))


def _rope_tables():
    t = jnp.arange(SEQ)
    row = (t // GRID_W).astype(F32)
    col = (t % GRID_W).astype(F32)
    n_freq = DIFF_DK // 4
    inv = ROPE_THETA ** (-jnp.arange(n_freq, dtype=F32) / n_freq)
    ang = jnp.concatenate([row[:, None] * inv, col[:, None] * inv], axis=-1)
    cos, sin = jnp.cos(ang), jnp.sin(ang)
    return jnp.concatenate([cos, cos], axis=-1), jnp.concatenate([-sin, sin], axis=-1)


def _mlp(x, mod, g, w1, w2, is_ctx):
    a = norm_matmul(x, g, mod, w1, shift_idx=3, is_ctx=is_ctx, out_dtype=BF16, relu2=True, name="mlp_up")
    return res_matmul([a], w2, x, mod, gate_idx=5, is_ctx=is_ctx, name="mlp_down")


def kernel(x, c, ctx, c_ctx, ada_w, ada_b, norm_mix_g, norm_mlp_g, mlp_w1, mlp_w2, ev_w_in, ev_w_out, na_q_g, na_k_g, na_rpb, hg_lb_logits, hg_gnorm_g, od_w_in, od_w_out, df_q_g, df_k_g, df_lambda, df_subln_g):
    xl = x.reshape(N_LAT, D_MODEL)
    xc = ctx.reshape(N_CTX, D_MODEL)
    cond = jnp.concatenate([c, c_ctx[None], jnp.zeros((MOD_ROWS - BATCH - 1, D_MODEL), F32)], axis=0)
    mods = ada_table(cond, ada_w, ada_b)
    lb_all = jnp.cumsum(jax.nn.softmax(hg_lb_logits.astype(F32), axis=1), axis=1)
    cos2, sin2 = _rope_tables()

    for l in range(DEPTH):
        need_ctx = l < DEPTH - 1
        mod = mods[l]
        g_mix = norm_mix_g[l]
        if l % 2 == 0:
            e = l // 2
            w_in = ev_w_in[e].astype(BF16)
            w_out = ev_w_out[e].astype(BF16)
            p_lat = norm_matmul(xl, g_mix, mod, w_in, shift_idx=0, is_ctx=False, out_dtype=F32, name="even_in")
            p_ctx = norm_matmul(xc, g_mix, mod, w_in, shift_idx=0, is_ctx=True, out_dtype=F32, name="even_in_ctx")
            na_lat, na_ctx = na_attention(p_lat, p_ctx, na_q_g[e], na_k_g[e], na_bias_table(na_rpb[e]))
            hg_lat, hg_ctx = hgrn2(p_lat, p_ctx, lb_all[0, l], lb_all[1, l], hg_gnorm_g[e])
            xl = res_matmul([na_lat, hg_lat], w_out, xl, mod, gate_idx=2, is_ctx=False, name="mix_out")
            if need_ctx:
                xc = res_matmul([na_ctx, hg_ctx], w_out, xc, mod, gate_idx=2, is_ctx=True, name="mix_out_ctx")
        else:
            assert not need_ctx, "an odd layer followed by another layer needs context outputs"
            o = l // 2
            w_in = od_w_in[o].astype(BF16)
            p_lat = norm_matmul(xl, g_mix, mod, w_in, shift_idx=0, is_ctx=False, out_dtype=F32, name="odd_in")
            p_ctx = norm_matmul(xc, g_mix, mod, w_in, shift_idx=0, is_ctx=True, out_dtype=F32,
                                col_block0=D_MODEL // MM_TILE, n_cols=2 * D_MODEL, name="odd_in_ctx")
            lam_init = 0.8 - 0.6 * math.exp(-0.3 * l)
            lp = df_lambda[o].astype(F32)
            lam = jnp.exp(jnp.sum(lp[0] * lp[1])) - jnp.exp(jnp.sum(lp[2] * lp[3])) + lam_init
            om = diff_attention(p_lat, p_ctx, lam.reshape(1, 1), cos2, sin2, df_q_g[o], df_k_g[o], df_subln_g[o],
                                1.0 - lam_init)
            xl = res_matmul([om], od_w_out[o].astype(BF16), xl, mod, gate_idx=2, is_ctx=False, name="mix_out")
        w1 = mlp_w1[l].astype(BF16)
        w2 = mlp_w2[l].astype(BF16)
        xl = _mlp(xl, mod, norm_mlp_g[l], w1, w2, False)
        if need_ctx:
            xc = _mlp(xc, mod, norm_mlp_g[l], w1, w2, True)
    return xl.reshape(BATCH, SEQ, D_MODEL)
```

```python
import functools
import math

import jax
import jax.numpy as jnp
import numpy as np
from jax import lax
from jax.experimental import pallas as pl
from jax.experimental.pallas import tpu as pltpu

F32 = jnp.float32
BF16 = jnp.bfloat16

D_MODEL = 2048
BATCH = 4
SEQ = 2048
DEPTH = 2
GRID_W = 64
GRID_ROWS = SEQ // GRID_W
CTX_LEN = 256
HEAD_DIM = 128
NA_HEADS = 8
NA_WIN_R = 8
NA_WIN_C = 16
NA_WIDTH = NA_HEADS * HEAD_DIM
HG_HEADS = 8
HG_CHUNK = 16
HG_WIDTH = HG_HEADS * HEAD_DIM
EVEN_IN = 8 * NA_WIDTH
DIFF_HEADS = 8
DIFF_DK = HEAD_DIM
DIFF_DV = 2 * HEAD_DIM
ODD_IN = 3 * D_MODEL
D_FF = 4 * D_MODEL
N_MOD = 6
ROPE_THETA = 10000.0
EPS = 1e-6

N_LAT = BATCH * SEQ
N_CTX = BATCH * CTX_LEN
MOD_ROWS = 8
CTX_MOD_ROW = BATCH
NEG_BIG = -1e30

VMEM_LIMIT_V7X = 56 * 1024 * 1024
MM_TILE = 1024
MM_TK = 2048


def _cparams(sem, vmem=VMEM_LIMIT_V7X):
    return pltpu.CompilerParams(dimension_semantics=sem, vmem_limit_bytes=vmem)


def _dot(a, b):
    return jnp.dot(a, b, preferred_element_type=F32)


def _dot_nt(a, b):
    return lax.dot_general(a, b, (((1,), (1,)), ((), ())), preferred_element_type=F32)


def _rms_rows(x, g):
    return x * lax.rsqrt(jnp.mean(x * x, axis=-1, keepdims=True) + EPS) * g


def _mod_row_of_tile(is_ctx, tile_rows):
    if is_ctx:
        return lambda i: CTX_MOD_ROW
    return lambda i: (i * tile_rows) // SEQ


def _ada_body(c_ref, w_ref, b_ref, o_ref):
    c = c_ref[...]
    s = (c * jax.nn.sigmoid(c)).astype(BF16)
    o_ref[0] = _dot(s, w_ref[0].astype(BF16)) + b_ref[0]


def ada_table(cond, ada_w, ada_b):
    tn = 1024
    n = N_MOD * D_MODEL
    out = pl.pallas_call(
        _ada_body,
        out_shape=jax.ShapeDtypeStruct((DEPTH, MOD_ROWS, n), F32),
        grid=(DEPTH, n // tn),
        in_specs=[
            pl.BlockSpec((MOD_ROWS, D_MODEL), lambda l, j: (0, 0)),
            pl.BlockSpec((1, D_MODEL, tn), lambda l, j: (l, 0, j)),
            pl.BlockSpec((1, 1, tn), lambda l, j: (l, 0, j)),
        ],
        out_specs=pl.BlockSpec((1, MOD_ROWS, tn), lambda l, j: (l, 0, j)),
        compiler_params=_cparams(("arbitrary", "arbitrary")),
        name="ada_table",
    )(cond, ada_w, ada_b.reshape(DEPTH, 1, n))
    return out.reshape(DEPTH, MOD_ROWS, N_MOD, D_MODEL)


def _norm_mm_body(x_ref, g_ref, mod_ref, w_ref, o_ref, h_ref, *, shift_idx, relu2):
    @pl.when(pl.program_id(1) == 0)
    def _():
        shift = mod_ref[0, shift_idx:shift_idx + 1, :]
        scale = mod_ref[0, shift_idx + 1:shift_idx + 2, :]
        h_ref[...] = (_rms_rows(x_ref[...], g_ref[...]) * (1.0 + scale) + shift).astype(BF16)

    acc = _dot(h_ref[...], w_ref[...])
    if relu2:
        acc = jnp.maximum(acc, 0.0)
        acc = acc * acc
    o_ref[...] = acc.astype(o_ref.dtype)


def norm_matmul(x, g, mod, w, w_idx, *, shift_idx, is_ctx, out_dtype, relu2=False, col_block0=0, n_cols=None, name):
    n_rows = x.shape[0]
    tm = tn = MM_TILE
    n_cols = w.shape[2] if n_cols is None else n_cols
    mod_row = _mod_row_of_tile(is_ctx, tm)
    return pl.pallas_call(
        functools.partial(_norm_mm_body, shift_idx=shift_idx, relu2=relu2),
        out_shape=jax.ShapeDtypeStruct((n_rows, n_cols), out_dtype),
        grid=(n_rows // tm, n_cols // tn),
        in_specs=[
            pl.BlockSpec((tm, D_MODEL), lambda i, j: (i, 0)),
            pl.BlockSpec((1, D_MODEL), lambda i, j: (0, 0)),
            pl.BlockSpec((1, N_MOD, D_MODEL), lambda i, j: (mod_row(i), 0, 0)),
            pl.BlockSpec((None, D_MODEL, tn), lambda i, j: (w_idx, 0, col_block0 + j)),
        ],
        out_specs=pl.BlockSpec((tm, tn), lambda i, j: (i, j)),
        scratch_shapes=[pltpu.VMEM((tm, D_MODEL), BF16)],
        compiler_params=_cparams(("arbitrary", "arbitrary")),
        name=name,
    )(x, g.reshape(1, D_MODEL), mod, w)


def _res_mm_body(*refs, n_x, nk, gate_idx):
    x_refs = refs[:n_x]
    w_ref, res_ref, mod_ref, o_ref = refs[n_x:n_x + 4]
    scratch = refs[n_x + 4:]

    def partial_product():
        acc = None
        k0 = 0
        for x_ref in x_refs:
            kw = x_ref.shape[1]
            d = _dot(x_ref[...], w_ref[k0:k0 + kw, :])
            acc = d if acc is None else acc + d
            k0 += kw
        return acc

    def finish(acc):
        gate = mod_ref[0, gate_idx:gate_idx + 1, :]
        o_ref[...] = res_ref[...] + gate * acc

    if nk == 1:
        finish(partial_product())
    else:
        acc_ref = scratch[0]
        k = pl.program_id(2)

        @pl.when(k == 0)
        def _():
            acc_ref[...] = jnp.zeros_like(acc_ref)

        acc_ref[...] += partial_product()

        @pl.when(k == nk - 1)
        def _():
            finish(acc_ref[...])


def res_matmul(xs, w, w_idx, res, mod, *, gate_idx, is_ctx, name):
    n_rows = res.shape[0]
    _, k_dim, n_dim = w.shape
    tm = tn = MM_TILE
    if len(xs) == 1:
        tk = MM_TK
        x_specs = [pl.BlockSpec((tm, tk), lambda i, j, k: (i, k))]
    else:
        tk = k_dim
        x_specs = [pl.BlockSpec((tm, x.shape[1]), lambda i, j, k: (i, 0)) for x in xs]
    nk = k_dim // tk
    mod_row = _mod_row_of_tile(is_ctx, tm)
    return pl.pallas_call(
        functools.partial(_res_mm_body, n_x=len(xs), nk=nk, gate_idx=gate_idx),
        out_shape=jax.ShapeDtypeStruct((n_rows, n_dim), F32),
        grid=(n_rows // tm, n_dim // tn, nk),
        in_specs=x_specs + [
            pl.BlockSpec((None, tk, tn), lambda i, j, k: (w_idx, k, j)),
            pl.BlockSpec((tm, tn), lambda i, j, k: (i, j)),
            pl.BlockSpec((1, N_MOD, tn), lambda i, j, k: (mod_row(i), 0, j)),
        ],
        out_specs=pl.BlockSpec((tm, tn), lambda i, j, k: (i, j)),
        scratch_shapes=[pltpu.VMEM((tm, tn), F32)] if nk > 1 else [],
        compiler_params=_cparams(("arbitrary", "arbitrary", "arbitrary")),
        name=name,
    )(*xs, w, res, mod)


NA_PAIR_Q = 2 * GRID_W
NA_PAIR_KROWS = 10
NA_PAIR_K = NA_PAIR_KROWS * GRID_W
NA_PAIR_KINDS = (0, 2, 4, GRID_ROWS - 4, GRID_ROWS - 2)
NA_PAIR_UNROLL = 4


def _na_pair_key_row0(r):
    return min(max(r - NA_WIN_R // 2, 0), GRID_ROWS - NA_PAIR_KROWS)


def na_bias_table(rpb):
    w = GRID_W
    col = np.arange(w)
    c0 = np.clip(col - NA_WIN_C // 2, 0, w - NA_WIN_C)
    in_win = (col[None, :] >= c0[:, None]) & (col[None, :] < c0[:, None] + NA_WIN_C)
    padded = jnp.pad(rpb.astype(F32), ((0, 0), (0, 0), (w, w)))
    toep = jnp.stack([padded[:, :, NA_WIN_C - 1 - qc + w:NA_WIN_C - 1 - qc + 2 * w] for qc in range(w)], axis=2)
    toep = jnp.where(in_win[None, None], toep, NEG_BIG)
    masked = jnp.full((NA_HEADS, w, w), NEG_BIG, F32)
    kinds = []
    for r in NA_PAIR_KINDS:
        q_rows = []
        for rq in (r, r + 1):
            r0 = min(max(rq - NA_WIN_R // 2, 0), GRID_ROWS - NA_WIN_R)
            blocks = []
            for kr in range(_na_pair_key_row0(r), _na_pair_key_row0(r) + NA_PAIR_KROWS):
                blocks.append(toep[:, kr - rq + NA_WIN_R - 1] if r0 <= kr < r0 + NA_WIN_R else masked)
            q_rows.append(jnp.concatenate(blocks, axis=2))
        kinds.append(jnp.concatenate(q_rows, axis=1))
    return jnp.stack(kinds, axis=1)


def _na_body(q_ref, k_ref, v_ref, qc_ref, kc_ref, vc_ref, qg_ref, kg_ref, bias_ref,
             o_ref, oc_ref, qn_ref, kn_ref, vn_ref, kcn_ref, vcn_ref):
    scale = HEAD_DIM ** -0.5
    qg = qg_ref[...]
    kg = kg_ref[...]

    def prep(i, _):
        rows = pl.ds(pl.multiple_of(i * 256, 256), 256)
        qn_ref[rows, :] = (_rms_rows(q_ref[rows, :], qg) * scale).astype(BF16)
        kn_ref[rows, :] = _rms_rows(k_ref[rows, :], kg).astype(BF16)
        vn_ref[rows, :] = v_ref[rows, :].astype(BF16)
        return 0

    lax.fori_loop(0, SEQ // 256, prep, 0)
    kcn = _rms_rows(kc_ref[...], kg).astype(BF16)
    vcn = vc_ref[...].astype(BF16)
    kcn_ref[...] = kcn
    vcn_ref[...] = vcn

    n_pairs = GRID_ROWS // 2

    def pair(rp, _):
        r = 2 * rp
        u0 = jnp.clip(r - NA_WIN_R // 2, 0, GRID_ROWS - NA_PAIR_KROWS)
        kind = jnp.where(rp < 2, rp, jnp.where(rp >= n_pairs - 2, rp - (n_pairs - 5), 2))
        qrows = pl.ds(pl.multiple_of(rp * NA_PAIR_Q, NA_PAIR_Q), NA_PAIR_Q)
        krows = pl.ds(pl.multiple_of(u0 * GRID_W, 128), NA_PAIR_K)
        qb = qn_ref[qrows, :]
        s1 = _dot_nt(qb, kn_ref[krows, :]) + bias_ref[0, kind]
        s2 = _dot_nt(qb, kcn_ref[...])
        m = jnp.maximum(jnp.max(s1, axis=-1, keepdims=True), jnp.max(s2, axis=-1, keepdims=True))
        p1 = jnp.exp(s1 - m)
        p2 = jnp.exp(s2 - m)
        l = jnp.sum(p1, axis=-1, keepdims=True) + jnp.sum(p2, axis=-1, keepdims=True)
        o = _dot(p1.astype(BF16), vn_ref[krows, :]) + _dot(p2.astype(BF16), vcn_ref[...])
        o_ref[qrows, :] = (o / l).astype(o_ref.dtype)
        return 0

    lax.fori_loop(0, n_pairs, pair, 0, unroll=NA_PAIR_UNROLL)

    qcn = (_rms_rows(qc_ref[...], qg) * scale).astype(BF16)
    s = _dot_nt(qcn, kcn)
    m = jnp.max(s, axis=-1, keepdims=True)
    p = jnp.exp(s - m)
    l = jnp.sum(p, axis=-1, keepdims=True)
    oc_ref[...] = (_dot(p.astype(BF16), vcn) / l).astype(oc_ref.dtype)


def na_attention(p_lat, p_ctx, q_g, k_g, bias):
    hd = HEAD_DIM
    lat = lambda seg: pl.BlockSpec((SEQ, hd), lambda h, b: (b, seg * NA_HEADS + h))
    ctx = lambda seg: pl.BlockSpec((CTX_LEN, hd), lambda h, b: (b, seg * NA_HEADS + h))
    return pl.pallas_call(
        _na_body,
        out_shape=(jax.ShapeDtypeStruct((N_LAT, NA_WIDTH), BF16),
                   jax.ShapeDtypeStruct((N_CTX, NA_WIDTH), BF16)),
        grid=(NA_HEADS, BATCH),
        in_specs=[lat(0), lat(1), lat(2), ctx(0), ctx(1), ctx(2),
                  pl.BlockSpec((1, hd), lambda h, b: (0, 0)),
                  pl.BlockSpec((1, hd), lambda h, b: (0, 0)),
                  pl.BlockSpec((1, len(NA_PAIR_KINDS), NA_PAIR_Q, NA_PAIR_K), lambda h, b: (h, 0, 0, 0))],
        out_specs=(pl.BlockSpec((SEQ, hd), lambda h, b: (b, h)),
                   pl.BlockSpec((CTX_LEN, hd), lambda h, b: (b, h))),
        scratch_shapes=[pltpu.VMEM((SEQ, hd), BF16), pltpu.VMEM((SEQ, hd), BF16), pltpu.VMEM((SEQ, hd), BF16),
                        pltpu.VMEM((CTX_LEN, hd), BF16), pltpu.VMEM((CTX_LEN, hd), BF16)],
        compiler_params=_cparams(("arbitrary", "arbitrary")),
        name="na_attention",
    )(p_lat, p_lat, p_lat, p_ctx, p_ctx, p_ctx, q_g.reshape(1, hd), k_g.reshape(1, hd), bias)


HG_BLK = 128
HG_CPB = HG_BLK // HG_CHUNK
HG_CTX_BLKS = CTX_LEN // HG_BLK
HG_LAT_BLKS = SEQ // HG_BLK
HG_BLKS = HG_CTX_BLKS + HG_LAT_BLKS


def _chunk_scans(x, pos):
    pre = x
    suf = x
    d = 1
    while d < HG_CHUNK:
        pre = pre + jnp.where(pos >= d, pltpu.roll(pre, d, axis=0), 0.0)
        suf = suf + jnp.where(pos + d < HG_CHUNK, pltpu.roll(suf, HG_BLK - d, axis=0), 0.0)
        d *= 2
    return pre, suf


def _hg_body(q_ref, zf_ref, zb_ref, i_ref, g_ref, qc_ref, zfc_ref, zbc_ref, ic_ref, gc_ref,
             lbf_ref, lbb_ref, gn_ref, o_ref, oc_ref,
             qf_s, kf_s, df_s, gf_s, qb_s, kb_s, db_s, gb_s, v_s, vx_s, of_s, ob_s, sf_s, sb_s):
    lbf = lbf_ref[...]
    lbb = lbb_ref[...]
    row = lax.broadcasted_iota(jnp.int32, (HG_BLK, HEAD_DIM), 0)
    pos = row & (HG_CHUNK - 1)
    row_chunk = row >> int(math.log2(HG_CHUNK))

    def prep_block(blk, q, zf, zb, v):
        qs = q * jax.nn.sigmoid(q)
        ff = lbf + (1.0 - lbf) * jax.nn.sigmoid(zf)
        fb = lbb + (1.0 - lbb) * jax.nn.sigmoid(zb)
        lf = jnp.log(ff)
        lb_ = jnp.log(fb)
        pre_f, suf_f = _chunk_scans(lf, pos)
        pre_b, suf_b = _chunk_scans(lb_, pos)
        kf = 1.0 - ff
        kb = 1.0 - fb
        qf_s[blk] = (qs * jnp.exp(pre_f)).astype(BF16)
        kf_s[blk] = (kf * jnp.exp(-pre_f)).T.astype(BF16)
        df_s[blk] = (kf * jnp.exp(suf_f - lf)).T.astype(BF16)
        gf_s[blk] = jnp.exp(pre_f + suf_f - lf).T
        qb_s[blk] = (qs * jnp.exp(suf_b)).astype(BF16)
        kb_s[blk] = (kb * jnp.exp(-suf_b)).T.astype(BF16)
        db_s[blk] = (kb * jnp.exp(pre_b - lb_)).T.astype(BF16)
        gb_s[blk] = jnp.exp(pre_b + suf_b - lb_).T
        v_s[blk] = v.astype(BF16)
        vx_s[blk] = jnp.concatenate(
            [jnp.where(row_chunk == c, v, 0.0).astype(BF16) for c in range(HG_CPB)], axis=1)

    for cb in range(HG_CTX_BLKS):
        rows = slice(cb * HG_BLK, (cb + 1) * HG_BLK)
        prep_block(cb, qc_ref[rows, :], zfc_ref[rows, :], zbc_ref[rows, :], ic_ref[rows, :])

    def prep_lat(n, _):
        rows = pl.ds(pl.multiple_of(n * HG_BLK, HG_BLK), HG_BLK)
        prep_block(HG_CTX_BLKS + n, q_ref[rows, :], zf_ref[rows, :], zb_ref[rows, :], i_ref[rows, :])
        return 0

    lax.fori_loop(0, HG_LAT_BLKS, prep_lat, 0)

    sf_s[...] = jnp.zeros_like(sf_s)
    sb_s[...] = jnp.zeros_like(sb_s)

    ri = lax.broadcasted_iota(jnp.int32, (HG_BLK, HG_BLK), 0)
    ci = lax.broadcasted_iota(jnp.int32, (HG_BLK, HG_BLK), 1)
    same_chunk = (ri >> int(math.log2(HG_CHUNK))) == (ci >> int(math.log2(HG_CHUNK)))
    masks = (same_chunk & (ci <= ri), same_chunk & (ci >= ri))
    dirs = ((qf_s, kf_s, df_s, gf_s, sf_s, of_s), (qb_s, kb_s, db_s, gb_s, sb_s, ob_s))

    def scan_step(n, _):
        blks = (n, jnp.where(n < HG_CTX_BLKS, HG_CTX_BLKS - 1 - n, HG_BLKS + HG_CTX_BLKS - 1 - n))
        qd, o_intra, u_all, g_t, state = [], [], [], [], []
        for d in range(2):
            q_s, k_s, d_s, g_s, s_ref, _ = dirs[d]
            blk = blks[d]
            qd.append(q_s[blk])
            att = jnp.where(masks[d], _dot(qd[d], k_s[blk]), 0.0).astype(BF16)
            o_intra.append(_dot(att, v_s[blk]))
            u_all.append(_dot(d_s[blk], vx_s[blk]))
            g_t.append(g_s[blk])
            state.append(s_ref[...])
        for step in range(HG_CPB):
            for d in range(2):
                c = step if d == 0 else HG_CPB - 1 - step
                rows = slice(c * HG_CHUNK, (c + 1) * HG_CHUNK)
                o_c = o_intra[d][rows, :] + _dot(qd[d][rows, :], state[d].astype(BF16))
                dirs[d][5][blks[d], rows, :] = o_c
                g_col = g_t[d][:, c * HG_CHUNK:c * HG_CHUNK + 1]
                state[d] = g_col * state[d] + u_all[d][:, c * HEAD_DIM:(c + 1) * HEAD_DIM]
        for d in range(2):
            dirs[d][4][...] = state[d]
        return 0

    lax.fori_loop(0, HG_BLKS, scan_step, 0)

    gn = gn_ref[...]

    def finish(blk, gate):
        o = of_s[blk] + ob_s[blk]
        return (_rms_rows(o, gn) * (gate * jax.nn.sigmoid(gate))).astype(BF16)

    for cb in range(HG_CTX_BLKS):
        rows = slice(cb * HG_BLK, (cb + 1) * HG_BLK)
        oc_ref[rows, :] = finish(cb, gc_ref[rows, :])

    def fin_lat(n, _):
        rows = pl.ds(pl.multiple_of(n * HG_BLK, HG_BLK), HG_BLK)
        o_ref[rows, :] = finish(HG_CTX_BLKS + n, g_ref[rows, :])
        return 0

    lax.fori_loop(0, HG_LAT_BLKS, fin_lat, 0)


def hgrn2(p_lat, p_ctx, lb_fwd, lb_bwd, gn_g):
    hd = HEAD_DIM
    lat = lambda seg: pl.BlockSpec((SEQ, hd), lambda b, h: (b, seg * HG_HEADS + h))
    ctx = lambda seg: pl.BlockSpec((CTX_LEN, hd), lambda b, h: (b, seg * HG_HEADS + h))
    head_vec = pl.BlockSpec((1, hd), lambda b, h: (0, h))
    blk_bf = lambda: pltpu.VMEM((HG_BLKS, HG_BLK, hd), BF16)
    blk_f32 = lambda: pltpu.VMEM((HG_BLKS, HG_BLK, hd), F32)
    return pl.pallas_call(
        _hg_body,
        out_shape=(jax.ShapeDtypeStruct((N_LAT, HG_WIDTH), BF16),
                   jax.ShapeDtypeStruct((N_CTX, HG_WIDTH), BF16)),
        grid=(BATCH, HG_HEADS),
        in_specs=[lat(3), lat(4), lat(5), lat(6), lat(7), ctx(3), ctx(4), ctx(5), ctx(6), ctx(7),
                  head_vec, head_vec, pl.BlockSpec((1, hd), lambda b, h: (0, 0))],
        out_specs=(pl.BlockSpec((SEQ, hd), lambda b, h: (b, h)),
                   pl.BlockSpec((CTX_LEN, hd), lambda b, h: (b, h))),
        scratch_shapes=[blk_bf(), blk_bf(), blk_bf(), blk_f32(),
                        blk_bf(), blk_bf(), blk_bf(), blk_f32(),
                        blk_bf(), pltpu.VMEM((HG_BLKS, HG_BLK, HG_CPB * hd), BF16), blk_f32(), blk_f32(),
                        pltpu.VMEM((hd, hd), F32), pltpu.VMEM((hd, hd), F32)],
        compiler_params=_cparams(("arbitrary", "arbitrary")),
        name="hgrn2",
    )(p_lat, p_lat, p_lat, p_lat, p_lat, p_ctx, p_ctx, p_ctx, p_ctx, p_ctx,
      lb_fwd.reshape(1, -1), lb_bwd.reshape(1, -1), gn_g.reshape(1, hd))


DF_TQ = 256
DF_NK = SEQ + CTX_LEN
DF_TK = 768
DF_KTILES = DF_NK // DF_TK


def _rope(x, cos2, sin2):
    return x * cos2 + pltpu.roll(x, DIFF_DK // 2, axis=1) * sin2


def _diff_body(lam_ref, q_ref, k_ref, kc_ref, v_ref, vc_ref, cosq_ref, sinq_ref, cosk_ref, sink_ref,
               qg_ref, kg_ref, sg_ref, o_ref, k0_s, k1_s, v_s, *, out_scale):
    dk = DIFF_DK
    qg = qg_ref[...]
    kg = kg_ref[...]

    @pl.when(pl.program_id(2) == 0)
    def _():
        def prep(i, _):
            rows = pl.ds(pl.multiple_of(i * 256, 256), 256)
            c2 = cosk_ref[rows, :]
            s2 = sink_ref[rows, :]
            kk = k_ref[rows, :]
            k0_s[rows, :] = _rope(_rms_rows(kk[:, :dk], kg), c2, s2).astype(BF16)
            k1_s[rows, :] = _rope(_rms_rows(kk[:, dk:], kg), c2, s2).astype(BF16)
            v_s[rows, :] = v_ref[rows, :].astype(BF16)
            return 0

        lax.fori_loop(0, SEQ // 256, prep, 0)
        kc = kc_ref[...]
        k0_s[SEQ:, :] = _rms_rows(kc[:, :dk], kg).astype(BF16)
        k1_s[SEQ:, :] = _rms_rows(kc[:, dk:], kg).astype(BF16)
        v_s[SEQ:, :] = vc_ref[...].astype(BF16)

    scale = dk ** -0.5
    lam = lam_ref[0, 0]
    q = q_ref[...]
    c2 = cosq_ref[...]
    s2 = sinq_ref[...]
    qn = ((_rope(_rms_rows(q[:, :dk], qg), c2, s2) * scale).astype(BF16),
          (_rope(_rms_rows(q[:, dk:], qg), c2, s2) * scale).astype(BF16))
    k_s = (k0_s, k1_s)

    m = [None, None]
    l = [None, None]
    acc = [None, None]
    for t in range(DF_KTILES):
        rows = slice(t * DF_TK, (t + 1) * DF_TK)
        vt = v_s[rows, :]
        for i in range(2):
            s = _dot_nt(qn[i], k_s[i][rows, :])
            tile_max = jnp.max(s, axis=-1, keepdims=True)
            if t == 0:
                m[i] = tile_max
                e = jnp.exp(s - m[i])
                l[i] = jnp.sum(e, axis=-1, keepdims=True)
                acc[i] = _dot(e.astype(BF16), vt)
            else:
                m_new = jnp.maximum(m[i], tile_max)
                alpha = jnp.exp(m[i] - m_new)
                e = jnp.exp(s - m_new)
                l[i] = alpha * l[i] + jnp.sum(e, axis=-1, keepdims=True)
                acc[i] = alpha * acc[i] + _dot(e.astype(BF16), vt)
                m[i] = m_new
    o = acc[0] * (1.0 / l[0]) - acc[1] * (lam / l[1])
    o_ref[...] = (_rms_rows(o, sg_ref[...]) * out_scale).astype(o_ref.dtype)


def diff_attention(p_lat, p_ctx_kv, lam, cos2, sin2, q_g, k_g, subln_g, out_scale):
    nq = SEQ // DF_TQ
    w = 2 * DIFF_DK
    kcol0 = D_MODEL // w
    vcol0 = 2 * D_MODEL // w
    vec = lambda n: pl.BlockSpec((1, n), lambda b, h, i: (0, 0))
    return pl.pallas_call(
        functools.partial(_diff_body, out_scale=out_scale),
        out_shape=jax.ShapeDtypeStruct((N_LAT, D_MODEL), BF16),
        grid=(BATCH, DIFF_HEADS, nq),
        in_specs=[
            pl.BlockSpec(memory_space=pltpu.SMEM),
            pl.BlockSpec((DF_TQ, w), lambda b, h, i: (b * nq + i, h)),
            pl.BlockSpec((SEQ, w), lambda b, h, i: (b, kcol0 + h)),
            pl.BlockSpec((CTX_LEN, w), lambda b, h, i: (b, h)),
            pl.BlockSpec((SEQ, w), lambda b, h, i: (b, vcol0 + h)),
            pl.BlockSpec((CTX_LEN, w), lambda b, h, i: (b, kcol0 + h)),
            pl.BlockSpec((DF_TQ, DIFF_DK), lambda b, h, i: (i, 0)),
            pl.BlockSpec((DF_TQ, DIFF_DK), lambda b, h, i: (i, 0)),
            pl.BlockSpec((SEQ, DIFF_DK), lambda b, h, i: (0, 0)),
            pl.BlockSpec((SEQ, DIFF_DK), lambda b, h, i: (0, 0)),
            vec(DIFF_DK), vec(DIFF_DK), vec(DIFF_DV),
        ],
        out_specs=pl.BlockSpec((DF_TQ, w), lambda b, h, i: (b * nq + i, h)),
        scratch_shapes=[pltpu.VMEM((DF_NK, DIFF_DK), BF16), pltpu.VMEM((DF_NK, DIFF_DK), BF16),
                        pltpu.VMEM((DF_NK, DIFF_DV), BF16)],
        compiler_params=_cparams(("arbitrary", "arbitrary", "arbitrary")),
        name="diff_attention",
    )(lam, p_lat, p_lat, p_ctx_kv, p_lat, p_ctx_kv, cos2, sin2, cos2, sin2,
      q_g.reshape(1, DIFF_DK), k_g.reshape(1, DIFF_DK), subln_g.reshape(1, DIFF_DV))


def _rope_tables():
    t = jnp.arange(SEQ)
    row = (t // GRID_W).astype(F32)
    col = (t % GRID_W).astype(F32)
    n_freq = DIFF_DK // 4
    inv = ROPE_THETA ** (-jnp.arange(n_freq, dtype=F32) / n_freq)
    ang = jnp.concatenate([row[:, None] * inv, col[:, None] * inv], axis=-1)
    cos, sin = jnp.cos(ang), jnp.sin(ang)
    return jnp.concatenate([cos, cos], axis=-1), jnp.concatenate([-sin, sin], axis=-1)


def _mlp(x, mod, g, w1, w2, layer, is_ctx):
    a = norm_matmul(x, g, mod, w1, layer, shift_idx=3, is_ctx=is_ctx, out_dtype=BF16, relu2=True, name="mlp_up")
    return res_matmul([a], w2, layer, x, mod, gate_idx=5, is_ctx=is_ctx, name="mlp_down")


def kernel(x, c, ctx, c_ctx, ada_w, ada_b, norm_mix_g, norm_mlp_g, mlp_w1, mlp_w2, ev_w_in, ev_w_out, na_q_g, na_k_g, na_rpb, hg_lb_logits, hg_gnorm_g, od_w_in, od_w_out, df_q_g, df_k_g, df_lambda, df_subln_g):
    xl = x.reshape(N_LAT, D_MODEL)
    xc = ctx.reshape(N_CTX, D_MODEL)
    cond = jnp.concatenate([c, c_ctx[None], jnp.zeros((MOD_ROWS - BATCH - 1, D_MODEL), F32)], axis=0)
    mods = ada_table(cond, ada_w, ada_b)
    lb_all = jnp.cumsum(jax.nn.softmax(hg_lb_logits.astype(F32), axis=1), axis=1)
    cos2, sin2 = _rope_tables()
    w1, w2 = mlp_w1.astype(BF16), mlp_w2.astype(BF16)
    ev_in, ev_out = ev_w_in.astype(BF16), ev_w_out.astype(BF16)
    od_in, od_out = od_w_in.astype(BF16), od_w_out.astype(BF16)

    for l in range(DEPTH):
        need_ctx = l < DEPTH - 1
        mod = mods[l]
        g_mix = norm_mix_g[l]
        if l % 2 == 0:
            e = l // 2
            p_lat = norm_matmul(xl, g_mix, mod, ev_in, e, shift_idx=0, is_ctx=False, out_dtype=F32, name="even_in")
            p_ctx = norm_matmul(xc, g_mix, mod, ev_in, e, shift_idx=0, is_ctx=True, out_dtype=F32,
                                name="even_in_ctx")
            na_lat, na_ctx = na_attention(p_lat, p_ctx, na_q_g[e], na_k_g[e], na_bias_table(na_rpb[e]))
            hg_lat, hg_ctx = hgrn2(p_lat, p_ctx, lb_all[0, l], lb_all[1, l], hg_gnorm_g[e])
            xl = res_matmul([na_lat, hg_lat], ev_out, e, xl, mod, gate_idx=2, is_ctx=False, name="mix_out")
            if need_ctx:
                xc = res_matmul([na_ctx, hg_ctx], ev_out, e, xc, mod, gate_idx=2, is_ctx=True, name="mix_out_ctx")
        else:
            assert not need_ctx, "an odd layer followed by another layer needs context outputs"
            o = l // 2
            p_lat = norm_matmul(xl, g_mix, mod, od_in, o, shift_idx=0, is_ctx=False, out_dtype=F32, name="odd_in")
            p_ctx = norm_matmul(xc, g_mix, mod, od_in, o, shift_idx=0, is_ctx=True, out_dtype=F32,
                                col_block0=D_MODEL // MM_TILE, n_cols=2 * D_MODEL, name="odd_in_ctx")
            lam_init = 0.8 - 0.6 * math.exp(-0.3 * l)
            lp = df_lambda[o].astype(F32)
            lam = jnp.exp(jnp.sum(lp[0] * lp[1])) - jnp.exp(jnp.sum(lp[2] * lp[3])) + lam_init
            om = diff_attention(p_lat, p_ctx, lam.reshape(1, 1), cos2, sin2, df_q_g[o], df_k_g[o], df_subln_g[o],
                                1.0 - lam_init)
            xl = res_matmul([om], od_out, o, xl, mod, gate_idx=2, is_ctx=False, name="mix_out")
        xl = _mlp(xl, mod, norm_mlp_g[l], w1, w2, l, False)
        if need_ctx:
            xc = _mlp(xc, mod, norm_mlp_g[l], w1, w2, l, True)
    return xl.reshape(BATCH, SEQ, D_MODEL)
```

```python
import functools
import math

import jax
import jax.numpy as jnp
import numpy as np
from jax import lax
from jax.experimental import pallas as pl
from jax.experimental.pallas import tpu as pltpu

F32 = jnp.float32
BF16 = jnp.bfloat16

D_MODEL = 2048
BATCH = 4
SEQ = 2048
DEPTH = 2
GRID_W = 64
GRID_ROWS = SEQ // GRID_W
CTX_LEN = 256
HEAD_DIM = 128
NA_HEADS = 8
NA_WIN_R = 8
NA_WIN_C = 16
NA_WIDTH = NA_HEADS * HEAD_DIM
HG_HEADS = 8
HG_CHUNK = 16
HG_WIDTH = HG_HEADS * HEAD_DIM
EVEN_IN = 8 * NA_WIDTH
DIFF_HEADS = 8
DIFF_DK = HEAD_DIM
DIFF_DV = 2 * HEAD_DIM
ODD_IN = 3 * D_MODEL
D_FF = 4 * D_MODEL
N_MOD = 6
ROPE_THETA = 10000.0
EPS = 1e-6

N_LAT = BATCH * SEQ
N_CTX = BATCH * CTX_LEN
MOD_ROWS = 8
CTX_MOD_ROW = BATCH
NEG_BIG = -1e30

VMEM_LIMIT_V7X = 56 * 1024 * 1024
MM_TILE = 1024
MM_TK = 2048


def _cparams(sem, vmem=VMEM_LIMIT_V7X):
    return pltpu.CompilerParams(dimension_semantics=sem, vmem_limit_bytes=vmem)


def _dot(a, b):
    return jnp.dot(a, b, preferred_element_type=F32)


def _dot_nt(a, b):
    return lax.dot_general(a, b, (((1,), (1,)), ((), ())), preferred_element_type=F32)


def _rms_rows(x, g):
    return x * lax.rsqrt(jnp.mean(x * x, axis=-1, keepdims=True) + EPS) * g


def _mod_row_of_tile(is_ctx, tile_rows):
    if is_ctx:
        return lambda i: CTX_MOD_ROW
    return lambda i: (i * tile_rows) // SEQ


def _ada_body(c_ref, w_ref, b_ref, o_ref):
    c = c_ref[...]
    s = (c * jax.nn.sigmoid(c)).astype(BF16)
    o_ref[0] = _dot(s, w_ref[0].astype(BF16)) + b_ref[0]


def ada_table(cond, ada_w, ada_b):
    tn = 1024
    n = N_MOD * D_MODEL
    out = pl.pallas_call(
        _ada_body,
        out_shape=jax.ShapeDtypeStruct((DEPTH, MOD_ROWS, n), F32),
        grid=(DEPTH, n // tn),
        in_specs=[
            pl.BlockSpec((MOD_ROWS, D_MODEL), lambda l, j: (0, 0)),
            pl.BlockSpec((1, D_MODEL, tn), lambda l, j: (l, 0, j)),
            pl.BlockSpec((1, 1, tn), lambda l, j: (l, 0, j)),
        ],
        out_specs=pl.BlockSpec((1, MOD_ROWS, tn), lambda l, j: (l, 0, j)),
        compiler_params=_cparams(("arbitrary", "arbitrary")),
        name="ada_table",
    )(cond, ada_w, ada_b.reshape(DEPTH, 1, n))
    return out.reshape(DEPTH, MOD_ROWS, N_MOD, D_MODEL)


def _norm_mm_body(x_ref, g_ref, mod_ref, w_ref, o_ref, h_ref, *, shift_idx, relu2):
    @pl.when(pl.program_id(1) == 0)
    def _():
        shift = mod_ref[0, shift_idx:shift_idx + 1, :]
        scale = mod_ref[0, shift_idx + 1:shift_idx + 2, :]
        h_ref[...] = (_rms_rows(x_ref[...], g_ref[...]) * (1.0 + scale) + shift).astype(BF16)

    acc = _dot(h_ref[...], w_ref[...])
    if relu2:
        acc = jnp.maximum(acc, 0.0)
        acc = acc * acc
    o_ref[...] = acc.astype(o_ref.dtype)


def norm_matmul(x, g, mod, w, w_idx, *, shift_idx, is_ctx, out_dtype, relu2=False, col_block0=0, n_cols=None, name):
    n_rows = x.shape[0]
    tm = tn = MM_TILE
    n_cols = w.shape[2] if n_cols is None else n_cols
    mod_row = _mod_row_of_tile(is_ctx, tm)
    return pl.pallas_call(
        functools.partial(_norm_mm_body, shift_idx=shift_idx, relu2=relu2),
        out_shape=jax.ShapeDtypeStruct((n_rows, n_cols), out_dtype),
        grid=(n_rows // tm, n_cols // tn),
        in_specs=[
            pl.BlockSpec((tm, D_MODEL), lambda i, j: (i, 0)),
            pl.BlockSpec((1, D_MODEL), lambda i, j: (0, 0)),
            pl.BlockSpec((1, N_MOD, D_MODEL), lambda i, j: (mod_row(i), 0, 0)),
            pl.BlockSpec((None, D_MODEL, tn), lambda i, j: (w_idx, 0, col_block0 + j)),
        ],
        out_specs=pl.BlockSpec((tm, tn), lambda i, j: (i, j)),
        scratch_shapes=[pltpu.VMEM((tm, D_MODEL), BF16)],
        compiler_params=_cparams(("arbitrary", "arbitrary")),
        name=name,
    )(x, g.reshape(1, D_MODEL), mod, w)


def _res_mm_body(*refs, n_x, nk, gate_idx):
    x_refs = refs[:n_x]
    w_ref, res_ref, mod_ref, o_ref = refs[n_x:n_x + 4]
    scratch = refs[n_x + 4:]

    def partial_product():
        acc = None
        k0 = 0
        for x_ref in x_refs:
            kw = x_ref.shape[1]
            d = _dot(x_ref[...], w_ref[k0:k0 + kw, :])
            acc = d if acc is None else acc + d
            k0 += kw
        return acc

    def finish(acc):
        gate = mod_ref[0, gate_idx:gate_idx + 1, :]
        o_ref[...] = res_ref[...] + gate * acc

    if nk == 1:
        finish(partial_product())
    else:
        acc_ref = scratch[0]
        k = pl.program_id(2)

        @pl.when(k == 0)
        def _():
            acc_ref[...] = jnp.zeros_like(acc_ref)

        acc_ref[...] += partial_product()

        @pl.when(k == nk - 1)
        def _():
            finish(acc_ref[...])


def res_matmul(xs, w, w_idx, res, mod, *, gate_idx, is_ctx, name):
    n_rows = res.shape[0]
    _, k_dim, n_dim = w.shape
    tm = tn = MM_TILE
    if len(xs) == 1:
        tk = MM_TK
        x_specs = [pl.BlockSpec((tm, tk), lambda i, j, k: (i, k))]
    else:
        tk = k_dim
        x_specs = [pl.BlockSpec((tm, x.shape[1]), lambda i, j, k: (i, 0)) for x in xs]
    nk = k_dim // tk
    mod_row = _mod_row_of_tile(is_ctx, tm)
    return pl.pallas_call(
        functools.partial(_res_mm_body, n_x=len(xs), nk=nk, gate_idx=gate_idx),
        out_shape=jax.ShapeDtypeStruct((n_rows, n_dim), F32),
        grid=(n_rows // tm, n_dim // tn, nk),
        in_specs=x_specs + [
            pl.BlockSpec((None, tk, tn), lambda i, j, k: (w_idx, k, j)),
            pl.BlockSpec((tm, tn), lambda i, j, k: (i, j)),
            pl.BlockSpec((1, N_MOD, tn), lambda i, j, k: (mod_row(i), 0, j)),
        ],
        out_specs=pl.BlockSpec((tm, tn), lambda i, j, k: (i, j)),
        scratch_shapes=[pltpu.VMEM((tm, tn), F32)] if nk > 1 else [],
        compiler_params=_cparams(("arbitrary", "arbitrary", "arbitrary")),
        name=name,
    )(*xs, w, res, mod)


NA_PAIR_Q = 2 * GRID_W
NA_PAIR_KROWS = 10
NA_PAIR_K = NA_PAIR_KROWS * GRID_W
NA_PAIR_KINDS = (0, 2, 4, GRID_ROWS - 4, GRID_ROWS - 2)
NA_PAIR_UNROLL = 4


def _na_pair_key_row0(r):
    return min(max(r - NA_WIN_R // 2, 0), GRID_ROWS - NA_PAIR_KROWS)


def na_bias_table(rpb):
    w = GRID_W
    col = np.arange(w)
    c0 = np.clip(col - NA_WIN_C // 2, 0, w - NA_WIN_C)
    in_win = (col[None, :] >= c0[:, None]) & (col[None, :] < c0[:, None] + NA_WIN_C)
    padded = jnp.pad(rpb.astype(F32), ((0, 0), (0, 0), (w, w)))
    toep = jnp.stack([padded[:, :, NA_WIN_C - 1 - qc + w:NA_WIN_C - 1 - qc + 2 * w] for qc in range(w)], axis=2)
    toep = jnp.where(in_win[None, None], toep, NEG_BIG)
    masked = jnp.full((NA_HEADS, w, w), NEG_BIG, F32)
    kinds = []
    for r in NA_PAIR_KINDS:
        q_rows = []
        for rq in (r, r + 1):
            r0 = min(max(rq - NA_WIN_R // 2, 0), GRID_ROWS - NA_WIN_R)
            blocks = []
            for kr in range(_na_pair_key_row0(r), _na_pair_key_row0(r) + NA_PAIR_KROWS):
                blocks.append(toep[:, kr - rq + NA_WIN_R - 1] if r0 <= kr < r0 + NA_WIN_R else masked)
            q_rows.append(jnp.concatenate(blocks, axis=2))
        kinds.append(jnp.concatenate(q_rows, axis=1))
    return jnp.stack(kinds, axis=1)


def _na_body(q_ref, k_ref, v_ref, qc_ref, kc_ref, vc_ref, qg_ref, kg_ref, bias_ref,
             o_ref, oc_ref, qn_ref, kn_ref, vn_ref, kcn_ref, vcn_ref):
    scale = HEAD_DIM ** -0.5
    qg = qg_ref[...]
    kg = kg_ref[...]

    def prep(i, _):
        rows = pl.ds(pl.multiple_of(i * 256, 256), 256)
        qn_ref[rows, :] = (_rms_rows(q_ref[rows, :], qg) * scale).astype(BF16)
        kn_ref[rows, :] = _rms_rows(k_ref[rows, :], kg).astype(BF16)
        vn_ref[rows, :] = v_ref[rows, :].astype(BF16)
        return 0

    lax.fori_loop(0, SEQ // 256, prep, 0)
    kcn = _rms_rows(kc_ref[...], kg).astype(BF16)
    vcn = vc_ref[...].astype(BF16)
    kcn_ref[...] = kcn
    vcn_ref[...] = vcn

    n_pairs = GRID_ROWS // 2

    def pair(rp, _):
        r = 2 * rp
        u0 = jnp.clip(r - NA_WIN_R // 2, 0, GRID_ROWS - NA_PAIR_KROWS)
        kind = jnp.where(rp < 2, rp, jnp.where(rp >= n_pairs - 2, rp - (n_pairs - 5), 2))
        qrows = pl.ds(pl.multiple_of(rp * NA_PAIR_Q, NA_PAIR_Q), NA_PAIR_Q)
        krows = pl.ds(pl.multiple_of(u0 * GRID_W, 128), NA_PAIR_K)
        qb = qn_ref[qrows, :]
        s1 = _dot_nt(qb, kn_ref[krows, :]) + bias_ref[0, kind]
        s2 = _dot_nt(qb, kcn_ref[...])
        m = jnp.maximum(jnp.max(s1, axis=-1, keepdims=True), jnp.max(s2, axis=-1, keepdims=True))
        p1 = jnp.exp(s1 - m)
        p2 = jnp.exp(s2 - m)
        l = jnp.sum(p1, axis=-1, keepdims=True) + jnp.sum(p2, axis=-1, keepdims=True)
        o = _dot(p1.astype(BF16), vn_ref[krows, :]) + _dot(p2.astype(BF16), vcn_ref[...])
        o_ref[qrows, :] = (o / l).astype(o_ref.dtype)
        return 0

    lax.fori_loop(0, n_pairs, pair, 0, unroll=NA_PAIR_UNROLL)

    qcn = (_rms_rows(qc_ref[...], qg) * scale).astype(BF16)
    s = _dot_nt(qcn, kcn)
    m = jnp.max(s, axis=-1, keepdims=True)
    p = jnp.exp(s - m)
    l = jnp.sum(p, axis=-1, keepdims=True)
    oc_ref[...] = (_dot(p.astype(BF16), vcn) / l).astype(oc_ref.dtype)


def na_attention(p_lat, p_ctx, q_g, k_g, bias):
    hd = HEAD_DIM
    lat = lambda seg: pl.BlockSpec((SEQ, hd), lambda h, b: (b, seg * NA_HEADS + h))
    ctx = lambda seg: pl.BlockSpec((CTX_LEN, hd), lambda h, b: (b, seg * NA_HEADS + h))
    return pl.pallas_call(
        _na_body,
        out_shape=(jax.ShapeDtypeStruct((N_LAT, NA_WIDTH), BF16),
                   jax.ShapeDtypeStruct((N_CTX, NA_WIDTH), BF16)),
        grid=(NA_HEADS, BATCH),
        in_specs=[lat(0), lat(1), lat(2), ctx(0), ctx(1), ctx(2),
                  pl.BlockSpec((1, hd), lambda h, b: (0, 0)),
                  pl.BlockSpec((1, hd), lambda h, b: (0, 0)),
                  pl.BlockSpec((1, len(NA_PAIR_KINDS), NA_PAIR_Q, NA_PAIR_K), lambda h, b: (h, 0, 0, 0))],
        out_specs=(pl.BlockSpec((SEQ, hd), lambda h, b: (b, h)),
                   pl.BlockSpec((CTX_LEN, hd), lambda h, b: (b, h))),
        scratch_shapes=[pltpu.VMEM((SEQ, hd), BF16), pltpu.VMEM((SEQ, hd), BF16), pltpu.VMEM((SEQ, hd), BF16),
                        pltpu.VMEM((CTX_LEN, hd), BF16), pltpu.VMEM((CTX_LEN, hd), BF16)],
        compiler_params=_cparams(("arbitrary", "arbitrary")),
        name="na_attention",
    )(p_lat, p_lat, p_lat, p_ctx, p_ctx, p_ctx, q_g.reshape(1, hd), k_g.reshape(1, hd), bias)


HG_BLK = 128
HG_CPB = HG_BLK // HG_CHUNK
HG_CTX_BLKS = CTX_LEN // HG_BLK
HG_LAT_BLKS = SEQ // HG_BLK
HG_BLKS = HG_CTX_BLKS + HG_LAT_BLKS
HG_SCAN_STEPS = 3


def _hg_ref_rows(reverse):
    b, c = HG_BLK, HG_CHUNK
    if not reverse:
        return {c: [None] + [c * i - 1 for i in range(1, b // c)],
                32: [32 * j + 15 for j in range(b // 32)], 64: [64 * j + 31 for j in range(b // 64)],
                128: [63], "end": [b - 1]}
    return {c: [c * i + c for i in range(b // c - 1)] + [None],
            32: [32 * j + 16 for j in range(b // 32)], 64: [64 * j + 32 for j in range(b // 64)],
            128: [64], "end": [0]}


def _hg_level_codes(reverse):
    t = lax.broadcasted_iota(jnp.int32, (HG_BLK, HG_BLK), 0)
    s = lax.broadcasted_iota(jnp.int32, (HG_BLK, HG_BLK), 1)
    if reverse:
        t, s = s, t
    code = jnp.where((t >> 6) > (s >> 6), 4, 0)
    code = jnp.where(((t >> 6) == (s >> 6)) & ((t >> 5) > (s >> 5)), 3, code)
    code = jnp.where(((t >> 5) == (s >> 5)) & ((t >> 4) > (s >> 4)), 2, code)
    return jnp.where(((t >> 4) == (s >> 4)) & (s <= t), 1, code)


def _hg_body(q_ref, zf_ref, zb_ref, i_ref, g_ref, qc_ref, zfc_ref, zbc_ref, ic_ref, gc_ref,
             lbf_ref, lbb_ref, gn_ref, o_ref, oc_ref,
             qd_s, ka_s, q64_s, k64_s, q128_s, k128_s, qb_s, ke_s, gb_s, v_s, o_s, st_s, x_s, code_s):
    lbs = (lbf_ref[...], lbb_ref[...])
    ri = lax.broadcasted_iota(jnp.int32, (HG_BLK, HG_BLK), 0)
    ci = lax.broadcasted_iota(jnp.int32, (HG_BLK, HG_BLK), 1)
    tri = jnp.where(ci <= ri, 1.0, 0.0).astype(BF16)
    for d in range(2):
        code_s[d] = _hg_level_codes(d == 1)

    def prefix_rows(x):
        hi = x.astype(BF16)
        r1 = x - hi.astype(F32)
        mid = r1.astype(BF16)
        lo = (r1 - mid.astype(F32)).astype(BF16)
        y = _dot(tri, jnp.concatenate([hi, mid, lo], axis=1))
        return y[:, :HEAD_DIM] + y[:, HEAD_DIM:2 * HEAD_DIM] + y[:, 2 * HEAD_DIM:]

    def ref_rows(d, rows):
        group = HG_BLK // len(rows)
        parts = [jnp.zeros((group, HEAD_DIM), F32) if r is None else
                 jnp.broadcast_to(x_s[d, r:r + 1, :], (group, HEAD_DIM)) for r in rows]
        return parts[0] if len(parts) == 1 else jnp.concatenate(parts, axis=0)

    def prep_block(blk, q, zf, zb, v):
        qs = q * jax.nn.sigmoid(q)
        v_s[blk] = v.astype(BF16)
        for d, z in enumerate((zf, zb)):
            f = lbs[d] + (1.0 - lbs[d]) * jax.nn.sigmoid(z)
            logf = jnp.log2(f)
            k = 1.0 - f
            x = prefix_rows(logf)
            x_s[d] = x
            if d == 1:
                x = ref_rows(d, [HG_BLK - 1]) - x + logf
                x_s[d] = x
            rows = _hg_ref_rows(d == 1)
            r16, r32, r64, r128 = (ref_rows(d, rows[g]) for g in (HG_CHUNK, 32, 64, 128))
            x_end = ref_rows(d, rows["end"])
            e32 = jnp.exp2(-jnp.abs(x - r32))
            e64 = jnp.exp2(-jnp.abs(x - r64))
            e128 = jnp.exp2(-jnp.abs(x - r128))
            qd_s[d, blk] = (qs * jnp.exp2(x - r16)).astype(BF16)
            ka_s[d, blk] = jnp.concatenate([(k * jnp.exp2(r16 - x)).T, (k * e32).T], axis=1).astype(BF16)
            q64_s[d, blk] = (qs * e64).astype(BF16)
            k64_s[d, blk] = (k * e64).T.astype(BF16)
            q128_s[d, blk] = (qs * e128).astype(BF16)
            k128_s[d, blk] = (k * e128).T.astype(BF16)
            qb_s[d, blk] = (qs * jnp.exp2(x)).astype(BF16)
            ke_s[d, blk] = (k * jnp.exp2(x_end - x)).T.astype(BF16)
            gb_s[d, blk] = jnp.exp2(x_end).T

    for cb in range(HG_CTX_BLKS):
        rows = slice(cb * HG_BLK, (cb + 1) * HG_BLK)
        prep_block(cb, qc_ref[rows, :], zfc_ref[rows, :], zbc_ref[rows, :], ic_ref[rows, :])

    def prep_lat(n, _):
        rows = pl.ds(pl.multiple_of(n * HG_BLK, HG_BLK), HG_BLK)
        prep_block(HG_CTX_BLKS + n, q_ref[rows, :], zf_ref[rows, :], zb_ref[rows, :], i_ref[rows, :])
        return 0

    lax.fori_loop(0, HG_LAT_BLKS, prep_lat, 0, unroll=2)

    st_s[...] = jnp.zeros_like(st_s)

    def scan_steps(i, _):
        chains = []
        for n in [i * HG_SCAN_STEPS + j for j in range(HG_SCAN_STEPS)]:
            chains.append((0, n))
            chains.append((1, jnp.where(n < HG_CTX_BLKS, HG_CTX_BLKS - 1 - n, HG_BLKS + HG_CTX_BLKS - 1 - n)))
        ready = []
        for d, blk in chains:
            a1 = _dot(qd_s[d, blk], ka_s[d, blk])
            a64 = _dot(q64_s[d, blk], k64_s[d, blk])
            a128 = _dot(q128_s[d, blk], k128_s[d, blk])
            v = v_s[blk]
            inc = _dot(ke_s[d, blk], v)
            code = code_s[d]
            att = jnp.where(code == 1, a1[:, :HG_BLK],
                            jnp.where(code == 2, a1[:, HG_BLK:],
                                      jnp.where(code == 3, a64, jnp.where(code == 4, a128, 0.0))))
            ready.append((jnp.concatenate([att.astype(BF16), qb_s[d, blk]], axis=1), v, inc))
        for (d, blk), (lhs, v, inc) in zip(chains, ready):
            state = st_s[d]
            o_s[d, blk] = _dot(lhs, jnp.concatenate([v, state.astype(BF16)], axis=0))
            st_s[d] = gb_s[d, blk] * state + inc
        return 0

    lax.fori_loop(0, HG_BLKS // HG_SCAN_STEPS, scan_steps, 0)

    gn = gn_ref[...]

    def finish(blk, gate):
        o = o_s[0, blk] + o_s[1, blk]
        return (_rms_rows(o, gn) * (gate * jax.nn.sigmoid(gate))).astype(BF16)

    for cb in range(HG_CTX_BLKS):
        rows = slice(cb * HG_BLK, (cb + 1) * HG_BLK)
        oc_ref[rows, :] = finish(cb, gc_ref[rows, :])

    def fin_lat(n, _):
        rows = pl.ds(pl.multiple_of(n * HG_BLK, HG_BLK), HG_BLK)
        o_ref[rows, :] = finish(HG_CTX_BLKS + n, g_ref[rows, :])
        return 0

    lax.fori_loop(0, HG_LAT_BLKS, fin_lat, 0)


def hgrn2(p_lat, p_ctx, lb_fwd, lb_bwd, gn_g):
    hd = HEAD_DIM
    lat = lambda seg: pl.BlockSpec((SEQ, hd), lambda b, h: (b, seg * HG_HEADS + h))
    ctx = lambda seg: pl.BlockSpec((CTX_LEN, hd), lambda b, h: (b, seg * HG_HEADS + h))
    head_vec = pl.BlockSpec((1, hd), lambda b, h: (0, h))
    dir_bf = lambda width=hd: pltpu.VMEM((2, HG_BLKS, HG_BLK, width), BF16)
    dir_f32 = lambda: pltpu.VMEM((2, HG_BLKS, HG_BLK, hd), F32)
    return pl.pallas_call(
        _hg_body,
        out_shape=(jax.ShapeDtypeStruct((N_LAT, HG_WIDTH), BF16),
                   jax.ShapeDtypeStruct((N_CTX, HG_WIDTH), BF16)),
        grid=(BATCH, HG_HEADS),
        in_specs=[lat(3), lat(4), lat(5), lat(6), lat(7), ctx(3), ctx(4), ctx(5), ctx(6), ctx(7),
                  head_vec, head_vec, pl.BlockSpec((1, hd), lambda b, h: (0, 0))],
        out_specs=(pl.BlockSpec((SEQ, hd), lambda b, h: (b, h)),
                   pl.BlockSpec((CTX_LEN, hd), lambda b, h: (b, h))),
        scratch_shapes=[dir_bf(), dir_bf(2 * hd), dir_bf(), dir_bf(), dir_bf(), dir_bf(), dir_bf(), dir_bf(),
                        dir_f32(), pltpu.VMEM((HG_BLKS, HG_BLK, hd), BF16), dir_f32(),
                        pltpu.VMEM((2, hd, hd), F32), pltpu.VMEM((2, HG_BLK, hd), F32),
                        pltpu.VMEM((2, HG_BLK, HG_BLK), jnp.int32)],
        compiler_params=_cparams(("arbitrary", "arbitrary")),
        name="hgrn2",
    )(p_lat, p_lat, p_lat, p_lat, p_lat, p_ctx, p_ctx, p_ctx, p_ctx, p_ctx,
      lb_fwd.reshape(1, -1), lb_bwd.reshape(1, -1), gn_g.reshape(1, hd))


DF_TQ = 256
DF_NK = SEQ + CTX_LEN
DF_TK = 768
DF_KTILES = DF_NK // DF_TK


def _rope(x, cos2, sin2):
    return x * cos2 + pltpu.roll(x, DIFF_DK // 2, axis=1) * sin2


def _diff_body(lam_ref, q_ref, k_ref, kc_ref, v_ref, vc_ref, cosq_ref, sinq_ref, cosk_ref, sink_ref,
               qg_ref, kg_ref, sg_ref, o_ref, k0_s, k1_s, v_s, *, out_scale):
    dk = DIFF_DK
    qg = qg_ref[...]
    kg = kg_ref[...]

    @pl.when(pl.program_id(2) == 0)
    def _():
        def prep(i, _):
            rows = pl.ds(pl.multiple_of(i * 256, 256), 256)
            c2 = cosk_ref[rows, :]
            s2 = sink_ref[rows, :]
            kk = k_ref[rows, :]
            k0_s[rows, :] = _rope(_rms_rows(kk[:, :dk], kg), c2, s2).astype(BF16)
            k1_s[rows, :] = _rope(_rms_rows(kk[:, dk:], kg), c2, s2).astype(BF16)
            v_s[rows, :] = v_ref[rows, :].astype(BF16)
            return 0

        lax.fori_loop(0, SEQ // 256, prep, 0)
        kc = kc_ref[...]
        k0_s[SEQ:, :] = _rms_rows(kc[:, :dk], kg).astype(BF16)
        k1_s[SEQ:, :] = _rms_rows(kc[:, dk:], kg).astype(BF16)
        v_s[SEQ:, :] = vc_ref[...].astype(BF16)

    scale = dk ** -0.5
    lam = lam_ref[0, 0]
    q = q_ref[...]
    c2 = cosq_ref[...]
    s2 = sinq_ref[...]
    qn = ((_rope(_rms_rows(q[:, :dk], qg), c2, s2) * scale).astype(BF16),
          (_rope(_rms_rows(q[:, dk:], qg), c2, s2) * scale).astype(BF16))
    k_s = (k0_s, k1_s)

    m = [None, None]
    l = [None, None]
    acc = [None, None]
    for t in range(DF_KTILES):
        rows = slice(t * DF_TK, (t + 1) * DF_TK)
        vt = v_s[rows, :]
        for i in range(2):
            s = _dot_nt(qn[i], k_s[i][rows, :])
            tile_max = jnp.max(s, axis=-1, keepdims=True)
            if t == 0:
                m[i] = tile_max
                e = jnp.exp(s - m[i])
                l[i] = jnp.sum(e, axis=-1, keepdims=True)
                acc[i] = _dot(e.astype(BF16), vt)
            else:
                m_new = jnp.maximum(m[i], tile_max)
                alpha = jnp.exp(m[i] - m_new)
                e = jnp.exp(s - m_new)
                l[i] = alpha * l[i] + jnp.sum(e, axis=-1, keepdims=True)
                acc[i] = alpha * acc[i] + _dot(e.astype(BF16), vt)
                m[i] = m_new
    o = acc[0] * (1.0 / l[0]) - acc[1] * (lam / l[1])
    o_ref[...] = (_rms_rows(o, sg_ref[...]) * out_scale).astype(o_ref.dtype)


def diff_attention(p_lat, p_ctx_kv, lam, cos2, sin2, q_g, k_g, subln_g, out_scale):
    nq = SEQ // DF_TQ
    w = 2 * DIFF_DK
    kcol0 = D_MODEL // w
    vcol0 = 2 * D_MODEL // w
    vec = lambda n: pl.BlockSpec((1, n), lambda b, h, i: (0, 0))
    return pl.pallas_call(
        functools.partial(_diff_body, out_scale=out_scale),
        out_shape=jax.ShapeDtypeStruct((N_LAT, D_MODEL), BF16),
        grid=(BATCH, DIFF_HEADS, nq),
        in_specs=[
            pl.BlockSpec(memory_space=pltpu.SMEM),
            pl.BlockSpec((DF_TQ, w), lambda b, h, i: (b * nq + i, h)),
            pl.BlockSpec((SEQ, w), lambda b, h, i: (b, kcol0 + h)),
            pl.BlockSpec((CTX_LEN, w), lambda b, h, i: (b, h)),
            pl.BlockSpec((SEQ, w), lambda b, h, i: (b, vcol0 + h)),
            pl.BlockSpec((CTX_LEN, w), lambda b, h, i: (b, kcol0 + h)),
            pl.BlockSpec((DF_TQ, DIFF_DK), lambda b, h, i: (i, 0)),
            pl.BlockSpec((DF_TQ, DIFF_DK), lambda b, h, i: (i, 0)),
            pl.BlockSpec((SEQ, DIFF_DK), lambda b, h, i: (0, 0)),
            pl.BlockSpec((SEQ, DIFF_DK), lambda b, h, i: (0, 0)),
            vec(DIFF_DK), vec(DIFF_DK), vec(DIFF_DV),
        ],
        out_specs=pl.BlockSpec((DF_TQ, w), lambda b, h, i: (b * nq + i, h)),
        scratch_shapes=[pltpu.VMEM((DF_NK, DIFF_DK), BF16), pltpu.VMEM((DF_NK, DIFF_DK), BF16),
                        pltpu.VMEM((DF_NK, DIFF_DV), BF16)],
        compiler_params=_cparams(("arbitrary", "arbitrary", "arbitrary")),
        name="diff_attention",
    )(lam, p_lat, p_lat, p_ctx_kv, p_lat, p_ctx_kv, cos2, sin2, cos2, sin2,
      q_g.reshape(1, DIFF_DK), k_g.reshape(1, DIFF_DK), subln_g.reshape(1, DIFF_DV))


def _rope_tables():
    t = jnp.arange(SEQ)
    row = (t // GRID_W).astype(F32)
    col = (t % GRID_W).astype(F32)
    n_freq = DIFF_DK // 4
    inv = ROPE_THETA ** (-jnp.arange(n_freq, dtype=F32) / n_freq)
    ang = jnp.concatenate([row[:, None] * inv, col[:, None] * inv], axis=-1)
    cos, sin = jnp.cos(ang), jnp.sin(ang)
    return jnp.concatenate([cos, cos], axis=-1), jnp.concatenate([-sin, sin], axis=-1)


def _mlp(x, mod, g, w1, w2, layer, is_ctx):
    a = norm_matmul(x, g, mod, w1, layer, shift_idx=3, is_ctx=is_ctx, out_dtype=BF16, relu2=True, name="mlp_up")
    return res_matmul([a], w2, layer, x, mod, gate_idx=5, is_ctx=is_ctx, name="mlp_down")


def kernel(x, c, ctx, c_ctx, ada_w, ada_b, norm_mix_g, norm_mlp_g, mlp_w1, mlp_w2, ev_w_in, ev_w_out, na_q_g, na_k_g, na_rpb, hg_lb_logits, hg_gnorm_g, od_w_in, od_w_out, df_q_g, df_k_g, df_lambda, df_subln_g):
    xl = x.reshape(N_LAT, D_MODEL)
    xc = ctx.reshape(N_CTX, D_MODEL)
    cond = jnp.concatenate([c, c_ctx[None], jnp.zeros((MOD_ROWS - BATCH - 1, D_MODEL), F32)], axis=0)
    mods = ada_table(cond, ada_w, ada_b)
    lb_all = jnp.cumsum(jax.nn.softmax(hg_lb_logits.astype(F32), axis=1), axis=1)
    cos2, sin2 = _rope_tables()
    w1, w2 = mlp_w1.astype(BF16), mlp_w2.astype(BF16)
    ev_in, ev_out = ev_w_in.astype(BF16), ev_w_out.astype(BF16)
    od_in, od_out = od_w_in.astype(BF16), od_w_out.astype(BF16)

    for l in range(DEPTH):
        need_ctx = l < DEPTH - 1
        mod = mods[l]
        g_mix = norm_mix_g[l]
        if l % 2 == 0:
            e = l // 2
            p_lat = norm_matmul(xl, g_mix, mod, ev_in, e, shift_idx=0, is_ctx=False, out_dtype=F32, name="even_in")
            p_ctx = norm_matmul(xc, g_mix, mod, ev_in, e, shift_idx=0, is_ctx=True, out_dtype=F32,
                                name="even_in_ctx")
            na_lat, na_ctx = na_attention(p_lat, p_ctx, na_q_g[e], na_k_g[e], na_bias_table(na_rpb[e]))
            hg_lat, hg_ctx = hgrn2(p_lat, p_ctx, lb_all[0, l], lb_all[1, l], hg_gnorm_g[e])
            xl = res_matmul([na_lat, hg_lat], ev_out, e, xl, mod, gate_idx=2, is_ctx=False, name="mix_out")
            if need_ctx:
                xc = res_matmul([na_ctx, hg_ctx], ev_out, e, xc, mod, gate_idx=2, is_ctx=True, name="mix_out_ctx")
        else:
            assert not need_ctx, "an odd layer followed by another layer needs context outputs"
            o = l // 2
            p_lat = norm_matmul(xl, g_mix, mod, od_in, o, shift_idx=0, is_ctx=False, out_dtype=F32, name="odd_in")
            p_ctx = norm_matmul(xc, g_mix, mod, od_in, o, shift_idx=0, is_ctx=True, out_dtype=F32,
                                col_block0=D_MODEL // MM_TILE, n_cols=2 * D_MODEL, name="odd_in_ctx")
            lam_init = 0.8 - 0.6 * math.exp(-0.3 * l)
            lp = df_lambda[o].astype(F32)
            lam = jnp.exp(jnp.sum(lp[0] * lp[1])) - jnp.exp(jnp.sum(lp[2] * lp[3])) + lam_init
            om = diff_attention(p_lat, p_ctx, lam.reshape(1, 1), cos2, sin2, df_q_g[o], df_k_g[o], df_subln_g[o],
                                1.0 - lam_init)
            xl = res_matmul([om], od_out, o, xl, mod, gate_idx=2, is_ctx=False, name="mix_out")
        xl = _mlp(xl, mod, norm_mlp_g[l], w1, w2, l, False)
        if need_ctx:
            xc = _mlp(xc, mod, norm_mlp_g[l], w1, w2, l, True)
    return xl.reshape(BATCH, SEQ, D_MODEL)
```

```python
import functools
import math

import jax
import jax.numpy as jnp
import numpy as np
from jax import lax
from jax.experimental import pallas as pl
from jax.experimental.pallas import tpu as pltpu

F32 = jnp.float32
BF16 = jnp.bfloat16

D_MODEL = 2048
BATCH = 4
SEQ = 2048
DEPTH = 2
GRID_W = 64
GRID_ROWS = SEQ // GRID_W
CTX_LEN = 256
HEAD_DIM = 128
NA_HEADS = 8
NA_WIN_R = 8
NA_WIN_C = 16
NA_WIDTH = NA_HEADS * HEAD_DIM
HG_HEADS = 8
HG_CHUNK = 16
HG_WIDTH = HG_HEADS * HEAD_DIM
EVEN_IN = 8 * NA_WIDTH
DIFF_HEADS = 8
DIFF_DK = HEAD_DIM
DIFF_DV = 2 * HEAD_DIM
ODD_IN = 3 * D_MODEL
D_FF = 4 * D_MODEL
N_MOD = 6
ROPE_THETA = 10000.0
EPS = 1e-6

N_LAT = BATCH * SEQ
N_CTX = BATCH * CTX_LEN
MOD_ROWS = 8
CTX_MOD_ROW = BATCH
NEG_BIG = -1e30

VMEM_LIMIT_V7X = 56 * 1024 * 1024
MM_TILE = 1024
MM_TK = 2048


def _cparams(sem, vmem=VMEM_LIMIT_V7X):
    return pltpu.CompilerParams(dimension_semantics=sem, vmem_limit_bytes=vmem)


def _dot(a, b):
    return jnp.dot(a, b, preferred_element_type=F32)


def _dot_nt(a, b):
    return lax.dot_general(a, b, (((1,), (1,)), ((), ())), preferred_element_type=F32)


def _rms_rows(x, g):
    return x * lax.rsqrt(jnp.mean(x * x, axis=-1, keepdims=True) + EPS) * g


def _mod_row_of_tile(is_ctx, tile_rows):
    if is_ctx:
        return lambda i: CTX_MOD_ROW
    return lambda i: (i * tile_rows) // SEQ


def _ada_body(c_ref, w_ref, b_ref, o_ref):
    c = c_ref[...]
    s = (c * jax.nn.sigmoid(c)).astype(BF16)
    o_ref[0] = _dot(s, w_ref[0].astype(BF16)) + b_ref[0]


def ada_table(cond, ada_w, ada_b):
    tn = 1024
    n = N_MOD * D_MODEL
    out = pl.pallas_call(
        _ada_body,
        out_shape=jax.ShapeDtypeStruct((DEPTH, MOD_ROWS, n), F32),
        grid=(DEPTH, n // tn),
        in_specs=[
            pl.BlockSpec((MOD_ROWS, D_MODEL), lambda l, j: (0, 0)),
            pl.BlockSpec((1, D_MODEL, tn), lambda l, j: (l, 0, j)),
            pl.BlockSpec((1, 1, tn), lambda l, j: (l, 0, j)),
        ],
        out_specs=pl.BlockSpec((1, MOD_ROWS, tn), lambda l, j: (l, 0, j)),
        compiler_params=_cparams(("arbitrary", "arbitrary")),
        name="ada_table",
    )(cond, ada_w, ada_b.reshape(DEPTH, 1, n))
    return out.reshape(DEPTH, MOD_ROWS, N_MOD, D_MODEL)


def _norm_mm_body(x_ref, g_ref, mod_ref, w_ref, o_ref, h_ref, *, shift_idx, relu2):
    @pl.when(pl.program_id(1) == 0)
    def _():
        shift = mod_ref[0, shift_idx:shift_idx + 1, :]
        scale = mod_ref[0, shift_idx + 1:shift_idx + 2, :]
        h_ref[...] = (_rms_rows(x_ref[...], g_ref[...]) * (1.0 + scale) + shift).astype(BF16)

    acc = _dot(h_ref[...], w_ref[...])
    if relu2:
        acc = jnp.maximum(acc, 0.0)
        acc = acc * acc
    o_ref[...] = acc.astype(o_ref.dtype)


def norm_matmul(x, g, mod, w, w_idx, *, shift_idx, is_ctx, out_dtype, relu2=False, col_block0=0, n_cols=None, name):
    n_rows = x.shape[0]
    tm = MM_TILE
    tn = MM_TILE * (jnp.dtype(F32).itemsize // jnp.dtype(out_dtype).itemsize)
    col_block0 = col_block0 * MM_TILE // tn
    n_cols = w.shape[2] if n_cols is None else n_cols
    mod_row = _mod_row_of_tile(is_ctx, tm)
    return pl.pallas_call(
        functools.partial(_norm_mm_body, shift_idx=shift_idx, relu2=relu2),
        out_shape=jax.ShapeDtypeStruct((n_rows, n_cols), out_dtype),
        grid=(n_rows // tm, n_cols // tn),
        in_specs=[
            pl.BlockSpec((tm, D_MODEL), lambda i, j: (i, 0)),
            pl.BlockSpec((1, D_MODEL), lambda i, j: (0, 0)),
            pl.BlockSpec((1, N_MOD, D_MODEL), lambda i, j: (mod_row(i), 0, 0)),
            pl.BlockSpec((None, D_MODEL, tn), lambda i, j: (w_idx, 0, col_block0 + j)),
        ],
        out_specs=pl.BlockSpec((tm, tn), lambda i, j: (i, j)),
        scratch_shapes=[pltpu.VMEM((tm, D_MODEL), BF16)],
        compiler_params=_cparams(("arbitrary", "arbitrary")),
        name=name,
    )(x, g.reshape(1, D_MODEL), mod, w)


def _res_mm_body(*refs, n_x, nk, gate_idx):
    x_refs = refs[:n_x]
    w_ref, res_ref, mod_ref, o_ref = refs[n_x:n_x + 4]
    scratch = refs[n_x + 4:]

    def partial_product():
        acc = None
        k0 = 0
        for x_ref in x_refs:
            kw = x_ref.shape[1]
            d = _dot(x_ref[...], w_ref[k0:k0 + kw, :])
            acc = d if acc is None else acc + d
            k0 += kw
        return acc

    def finish(acc):
        gate = mod_ref[0, gate_idx:gate_idx + 1, :]
        o_ref[...] = res_ref[...] + gate * acc

    if nk == 1:
        finish(partial_product())
    else:
        acc_ref = scratch[0]
        k = pl.program_id(2)

        @pl.when(k == 0)
        def _():
            acc_ref[...] = jnp.zeros_like(acc_ref)

        acc_ref[...] += partial_product()

        @pl.when(k == nk - 1)
        def _():
            finish(acc_ref[...])


def res_matmul(xs, w, w_idx, res, mod, *, gate_idx, is_ctx, name):
    n_rows = res.shape[0]
    _, k_dim, n_dim = w.shape
    tm = tn = MM_TILE
    if len(xs) == 1:
        tk = MM_TK
        x_specs = [pl.BlockSpec((tm, tk), lambda i, j, k: (i, k))]
    else:
        tk = k_dim
        x_specs = [pl.BlockSpec((tm, x.shape[1]), lambda i, j, k: (i, 0)) for x in xs]
    nk = k_dim // tk
    mod_row = _mod_row_of_tile(is_ctx, tm)
    return pl.pallas_call(
        functools.partial(_res_mm_body, n_x=len(xs), nk=nk, gate_idx=gate_idx),
        out_shape=jax.ShapeDtypeStruct((n_rows, n_dim), F32),
        grid=(n_rows // tm, n_dim // tn, nk),
        in_specs=x_specs + [
            pl.BlockSpec((None, tk, tn), lambda i, j, k: (w_idx, k, j)),
            pl.BlockSpec((tm, tn), lambda i, j, k: (i, j)),
            pl.BlockSpec((1, N_MOD, tn), lambda i, j, k: (mod_row(i), 0, j)),
        ],
        out_specs=pl.BlockSpec((tm, tn), lambda i, j, k: (i, j)),
        scratch_shapes=[pltpu.VMEM((tm, tn), F32)] if nk > 1 else [],
        compiler_params=_cparams(("arbitrary", "arbitrary", "arbitrary")),
        name=name,
    )(*xs, w, res, mod)


NA_PAIR_Q = 2 * GRID_W
NA_PAIR_KROWS = 10
NA_PAIR_K = NA_PAIR_KROWS * GRID_W
NA_PAIR_KINDS = (0, 2, 4, GRID_ROWS - 4, GRID_ROWS - 2)
NA_SCORE_LOOKAHEAD = 2


def _na_pair_key_row0(r):
    return min(max(r - NA_WIN_R // 2, 0), GRID_ROWS - NA_PAIR_KROWS)


def na_bias_table(rpb):
    w = GRID_W
    col = np.arange(w)
    c0 = np.clip(col - NA_WIN_C // 2, 0, w - NA_WIN_C)
    in_win = (col[None, :] >= c0[:, None]) & (col[None, :] < c0[:, None] + NA_WIN_C)
    padded = jnp.pad(rpb.astype(F32), ((0, 0), (0, 0), (w, w)))
    toep = jnp.stack([padded[:, :, NA_WIN_C - 1 - qc + w:NA_WIN_C - 1 - qc + 2 * w] for qc in range(w)], axis=2)
    toep = jnp.where(in_win[None, None], toep, NEG_BIG)
    masked = jnp.full((NA_HEADS, w, w), NEG_BIG, F32)
    kinds = []
    for r in NA_PAIR_KINDS:
        q_rows = []
        for rq in (r, r + 1):
            r0 = min(max(rq - NA_WIN_R // 2, 0), GRID_ROWS - NA_WIN_R)
            blocks = []
            for kr in range(_na_pair_key_row0(r), _na_pair_key_row0(r) + NA_PAIR_KROWS):
                blocks.append(toep[:, kr - rq + NA_WIN_R - 1] if r0 <= kr < r0 + NA_WIN_R else masked)
            q_rows.append(jnp.concatenate(blocks, axis=2))
        kinds.append(jnp.concatenate(q_rows, axis=1))
    return jnp.stack(kinds, axis=1)


def _na_body(q_ref, k_ref, v_ref, qc_ref, kc_ref, vc_ref, qg_ref, kg_ref, bias_ref,
             o_ref, oc_ref, qn_ref, kn_ref, vn_ref, kcn_ref, vcn_ref):
    scale = HEAD_DIM ** -0.5
    qg = qg_ref[...]
    kg = kg_ref[...]

    def prep(i, _):
        rows = pl.ds(pl.multiple_of(i * 256, 256), 256)
        qn_ref[rows, :] = (_rms_rows(q_ref[rows, :], qg) * scale).astype(BF16)
        kn_ref[rows, :] = _rms_rows(k_ref[rows, :], kg).astype(BF16)
        vn_ref[rows, :] = v_ref[rows, :].astype(BF16)
        return 0

    lax.fori_loop(0, SEQ // 256, prep, 0)
    kcn = _rms_rows(kc_ref[...], kg).astype(BF16)
    vcn = vc_ref[...].astype(BF16)
    kcn_ref[...] = kcn
    vcn_ref[...] = vcn

    n_pairs = GRID_ROWS // 2

    def pair_rows(rp):
        k0 = _na_pair_key_row0(2 * rp) * GRID_W
        return slice(rp * NA_PAIR_Q, (rp + 1) * NA_PAIR_Q), slice(k0, k0 + NA_PAIR_K)

    def scores(rp):
        qrows, krows = pair_rows(rp)
        qb = qn_ref[qrows, :]
        return _dot_nt(qb, kn_ref[krows, :]), _dot_nt(qb, kcn_ref[...])

    pending = [scores(rp) for rp in range(NA_SCORE_LOOKAHEAD)]
    for rp in range(n_pairs):
        s1, s2 = pending.pop(0)
        if rp + NA_SCORE_LOOKAHEAD < n_pairs:
            pending.append(scores(rp + NA_SCORE_LOOKAHEAD))
        qrows, krows = pair_rows(rp)
        kind = NA_PAIR_KINDS.index(min(max(2 * rp, 0), 4) if rp < n_pairs - 2 else 2 * rp)
        s1 = s1 + bias_ref[0, kind]
        m = jnp.maximum(jnp.max(s1, axis=-1, keepdims=True), jnp.max(s2, axis=-1, keepdims=True))
        p1 = jnp.exp(s1 - m)
        p2 = jnp.exp(s2 - m)
        l = jnp.sum(p1, axis=-1, keepdims=True) + jnp.sum(p2, axis=-1, keepdims=True)
        o = _dot(p1.astype(BF16), vn_ref[krows, :]) + _dot(p2.astype(BF16), vcn_ref[...])
        o_ref[qrows, :] = (o / l).astype(o_ref.dtype)

    qcn = (_rms_rows(qc_ref[...], qg) * scale).astype(BF16)
    s = _dot_nt(qcn, kcn)
    m = jnp.max(s, axis=-1, keepdims=True)
    p = jnp.exp(s - m)
    l = jnp.sum(p, axis=-1, keepdims=True)
    oc_ref[...] = (_dot(p.astype(BF16), vcn) / l).astype(oc_ref.dtype)


def na_attention(p_lat, p_ctx, q_g, k_g, bias):
    hd = HEAD_DIM
    lat = lambda seg: pl.BlockSpec((SEQ, hd), lambda h, b: (b, seg * NA_HEADS + h))
    ctx = lambda seg: pl.BlockSpec((CTX_LEN, hd), lambda h, b: (b, seg * NA_HEADS + h))
    return pl.pallas_call(
        _na_body,
        out_shape=(jax.ShapeDtypeStruct((N_LAT, NA_WIDTH), BF16),
                   jax.ShapeDtypeStruct((N_CTX, NA_WIDTH), BF16)),
        grid=(NA_HEADS, BATCH),
        in_specs=[lat(0), lat(1), lat(2), ctx(0), ctx(1), ctx(2),
                  pl.BlockSpec((1, hd), lambda h, b: (0, 0)),
                  pl.BlockSpec((1, hd), lambda h, b: (0, 0)),
                  pl.BlockSpec((1, len(NA_PAIR_KINDS), NA_PAIR_Q, NA_PAIR_K), lambda h, b: (h, 0, 0, 0))],
        out_specs=(pl.BlockSpec((SEQ, hd), lambda h, b: (b, h)),
                   pl.BlockSpec((CTX_LEN, hd), lambda h, b: (b, h))),
        scratch_shapes=[pltpu.VMEM((SEQ, hd), BF16), pltpu.VMEM((SEQ, hd), BF16), pltpu.VMEM((SEQ, hd), BF16),
                        pltpu.VMEM((CTX_LEN, hd), BF16), pltpu.VMEM((CTX_LEN, hd), BF16)],
        compiler_params=_cparams(("arbitrary", "arbitrary")),
        name="na_attention",
    )(p_lat, p_lat, p_lat, p_ctx, p_ctx, p_ctx, q_g.reshape(1, hd), k_g.reshape(1, hd), bias)


HG_BLK = 128
HG_CPB = HG_BLK // HG_CHUNK
HG_CTX_BLKS = CTX_LEN // HG_BLK
HG_LAT_BLKS = SEQ // HG_BLK
HG_BLKS = HG_CTX_BLKS + HG_LAT_BLKS
HG_SCAN_STEPS = 3


def _hg_ref_rows(reverse):
    b, c = HG_BLK, HG_CHUNK
    if not reverse:
        return {c: [None] + [c * i - 1 for i in range(1, b // c)],
                32: [32 * j + 15 for j in range(b // 32)], 64: [64 * j + 31 for j in range(b // 64)],
                128: [63], "end": [b - 1]}
    return {c: [c * i + c for i in range(b // c - 1)] + [None],
            32: [32 * j + 16 for j in range(b // 32)], 64: [64 * j + 32 for j in range(b // 64)],
            128: [64], "end": [0]}


def _hg_level_codes(reverse):
    t = lax.broadcasted_iota(jnp.int32, (HG_BLK, HG_BLK), 0)
    s = lax.broadcasted_iota(jnp.int32, (HG_BLK, HG_BLK), 1)
    if reverse:
        t, s = s, t
    code = jnp.where((t >> 6) > (s >> 6), 4, 0)
    code = jnp.where(((t >> 6) == (s >> 6)) & ((t >> 5) > (s >> 5)), 3, code)
    code = jnp.where(((t >> 5) == (s >> 5)) & ((t >> 4) > (s >> 4)), 2, code)
    return jnp.where(((t >> 4) == (s >> 4)) & (s <= t), 1, code)


def _hg_body(q_ref, zf_ref, zb_ref, i_ref, g_ref, qc_ref, zfc_ref, zbc_ref, ic_ref, gc_ref,
             lbf_ref, lbb_ref, gn_ref, o_ref, oc_ref,
             qd_s, ka_s, q64_s, k64_s, q128_s, k128_s, qb_s, ke_s, gb_s, v_s, o_s, st_s, x_s, code_s):
    lbs = (lbf_ref[...], lbb_ref[...])
    ri = lax.broadcasted_iota(jnp.int32, (HG_BLK, HG_BLK), 0)
    ci = lax.broadcasted_iota(jnp.int32, (HG_BLK, HG_BLK), 1)
    tri = jnp.where(ci <= ri, 1.0, 0.0).astype(BF16)
    for d in range(2):
        code_s[d] = _hg_level_codes(d == 1)

    def prefix_rows(x):
        hi = x.astype(BF16)
        r1 = x - hi.astype(F32)
        mid = r1.astype(BF16)
        lo = (r1 - mid.astype(F32)).astype(BF16)
        y = _dot(tri, jnp.concatenate([hi, mid, lo], axis=1))
        return y[:, :HEAD_DIM] + y[:, HEAD_DIM:2 * HEAD_DIM] + y[:, 2 * HEAD_DIM:]

    def ref_rows(d, rows):
        group = HG_BLK // len(rows)
        parts = [jnp.zeros((group, HEAD_DIM), F32) if r is None else
                 jnp.broadcast_to(x_s[d, r:r + 1, :], (group, HEAD_DIM)) for r in rows]
        return parts[0] if len(parts) == 1 else jnp.concatenate(parts, axis=0)

    def prep_block(blk, q, zf, zb, v):
        qs = q * jax.nn.sigmoid(q)
        v_s[blk] = v.astype(BF16)
        for d, z in enumerate((zf, zb)):
            f = lbs[d] + (1.0 - lbs[d]) * jax.nn.sigmoid(z)
            logf = jnp.log2(f)
            k = 1.0 - f
            x = prefix_rows(logf)
            x_s[d] = x
            if d == 1:
                x = ref_rows(d, [HG_BLK - 1]) - x + logf
                x_s[d] = x
            rows = _hg_ref_rows(d == 1)
            r16, r32, r64, r128 = (ref_rows(d, rows[g]) for g in (HG_CHUNK, 32, 64, 128))
            x_end = ref_rows(d, rows["end"])
            e32 = jnp.exp2(-jnp.abs(x - r32))
            e64 = jnp.exp2(-jnp.abs(x - r64))
            e128 = jnp.exp2(-jnp.abs(x - r128))
            qd_s[d, blk] = (qs * jnp.exp2(x - r16)).astype(BF16)
            ka_s[d, blk] = jnp.concatenate([(k * jnp.exp2(r16 - x)).T, (k * e32).T], axis=1).astype(BF16)
            q64_s[d, blk] = (qs * e64).astype(BF16)
            k64_s[d, blk] = (k * e64).T.astype(BF16)
            q128_s[d, blk] = (qs * e128).astype(BF16)
            k128_s[d, blk] = (k * e128).T.astype(BF16)
            qb_s[d, blk] = (qs * jnp.exp2(x)).astype(BF16)
            ke_s[d, blk] = (k * jnp.exp2(x_end - x)).T.astype(BF16)
            gb_s[d, blk] = jnp.exp2(x_end).T

    for cb in range(HG_CTX_BLKS):
        rows = slice(cb * HG_BLK, (cb + 1) * HG_BLK)
        prep_block(cb, qc_ref[rows, :], zfc_ref[rows, :], zbc_ref[rows, :], ic_ref[rows, :])

    def prep_lat(n, _):
        rows = pl.ds(pl.multiple_of(n * HG_BLK, HG_BLK), HG_BLK)
        prep_block(HG_CTX_BLKS + n, q_ref[rows, :], zf_ref[rows, :], zb_ref[rows, :], i_ref[rows, :])
        return 0

    lax.fori_loop(0, HG_LAT_BLKS, prep_lat, 0, unroll=2)

    st_s[...] = jnp.zeros_like(st_s)

    def scan_steps(i, _):
        chains = []
        for n in [i * HG_SCAN_STEPS + j for j in range(HG_SCAN_STEPS)]:
            chains.append((0, n))
            chains.append((1, jnp.where(n < HG_CTX_BLKS, HG_CTX_BLKS - 1 - n, HG_BLKS + HG_CTX_BLKS - 1 - n)))
        ready = []
        for d, blk in chains:
            a1 = _dot(qd_s[d, blk], ka_s[d, blk])
            a64 = _dot(q64_s[d, blk], k64_s[d, blk])
            a128 = _dot(q128_s[d, blk], k128_s[d, blk])
            v = v_s[blk]
            inc = _dot(ke_s[d, blk], v)
            code = code_s[d]
            att = jnp.where(code == 1, a1[:, :HG_BLK],
                            jnp.where(code == 2, a1[:, HG_BLK:],
                                      jnp.where(code == 3, a64, jnp.where(code == 4, a128, 0.0))))
            ready.append((jnp.concatenate([att.astype(BF16), qb_s[d, blk]], axis=1), v, inc))
        for (d, blk), (lhs, v, inc) in zip(chains, ready):
            state = st_s[d]
            o_s[d, blk] = _dot(lhs, jnp.concatenate([v, state.astype(BF16)], axis=0))
            st_s[d] = gb_s[d, blk] * state + inc
        return 0

    lax.fori_loop(0, HG_BLKS // HG_SCAN_STEPS, scan_steps, 0)

    gn = gn_ref[...]

    def finish(blk, gate):
        o = o_s[0, blk] + o_s[1, blk]
        return (_rms_rows(o, gn) * (gate * jax.nn.sigmoid(gate))).astype(BF16)

    for cb in range(HG_CTX_BLKS):
        rows = slice(cb * HG_BLK, (cb + 1) * HG_BLK)
        oc_ref[rows, :] = finish(cb, gc_ref[rows, :])

    def fin_lat(n, _):
        rows = pl.ds(pl.multiple_of(n * HG_BLK, HG_BLK), HG_BLK)
        o_ref[rows, :] = finish(HG_CTX_BLKS + n, g_ref[rows, :])
        return 0

    lax.fori_loop(0, HG_LAT_BLKS, fin_lat, 0)


def hgrn2(p_lat, p_ctx, lb_fwd, lb_bwd, gn_g):
    hd = HEAD_DIM
    lat = lambda seg: pl.BlockSpec((SEQ, hd), lambda b, h: (b, seg * HG_HEADS + h))
    ctx = lambda seg: pl.BlockSpec((CTX_LEN, hd), lambda b, h: (b, seg * HG_HEADS + h))
    head_vec = pl.BlockSpec((1, hd), lambda b, h: (0, h))
    dir_bf = lambda width=hd: pltpu.VMEM((2, HG_BLKS, HG_BLK, width), BF16)
    dir_f32 = lambda: pltpu.VMEM((2, HG_BLKS, HG_BLK, hd), F32)
    return pl.pallas_call(
        _hg_body,
        out_shape=(jax.ShapeDtypeStruct((N_LAT, HG_WIDTH), BF16),
                   jax.ShapeDtypeStruct((N_CTX, HG_WIDTH), BF16)),
        grid=(BATCH, HG_HEADS),
        in_specs=[lat(3), lat(4), lat(5), lat(6), lat(7), ctx(3), ctx(4), ctx(5), ctx(6), ctx(7),
                  head_vec, head_vec, pl.BlockSpec((1, hd), lambda b, h: (0, 0))],
        out_specs=(pl.BlockSpec((SEQ, hd), lambda b, h: (b, h)),
                   pl.BlockSpec((CTX_LEN, hd), lambda b, h: (b, h))),
        scratch_shapes=[dir_bf(), dir_bf(2 * hd), dir_bf(), dir_bf(), dir_bf(), dir_bf(), dir_bf(), dir_bf(),
                        dir_f32(), pltpu.VMEM((HG_BLKS, HG_BLK, hd), BF16), dir_f32(),
                        pltpu.VMEM((2, hd, hd), F32), pltpu.VMEM((2, HG_BLK, hd), F32),
                        pltpu.VMEM((2, HG_BLK, HG_BLK), jnp.int32)],
        compiler_params=_cparams(("arbitrary", "arbitrary")),
        name="hgrn2",
    )(p_lat, p_lat, p_lat, p_lat, p_lat, p_ctx, p_ctx, p_ctx, p_ctx, p_ctx,
      lb_fwd.reshape(1, -1), lb_bwd.reshape(1, -1), gn_g.reshape(1, hd))


DF_TQ = 512
DF_NK = SEQ + CTX_LEN
DF_TK = 768
DF_KTILES = DF_NK // DF_TK
DF_SCORE_LOOKAHEAD = 2


def _rope(x, cos2, sin2):
    return x * cos2 + pltpu.roll(x, DIFF_DK // 2, axis=1) * sin2


def _diff_body(lam_ref, q_ref, k_ref, kc_ref, v_ref, vc_ref, cos_ref, sin_ref, cost_ref, sint_ref,
               qgt_ref, kg_ref, sg_ref, o_ref, qt_s, k0_s, k1_s, vt_s, *, out_scale):
    dk = DIFF_DK
    kg = kg_ref[...]

    @pl.when(pl.program_id(2) == 0)
    def _():
        def prep(i, _):
            rows = pl.ds(pl.multiple_of(i * CTX_LEN, CTX_LEN), CTX_LEN)
            c2 = cos_ref[rows, :]
            s2 = sin_ref[rows, :]
            kk = k_ref[rows, :]
            k0_s[rows, :] = _rope(_rms_rows(kk[:, :dk], kg), c2, s2).astype(BF16)
            k1_s[rows, :] = _rope(_rms_rows(kk[:, dk:], kg), c2, s2).astype(BF16)
            vt_s[i] = v_ref[rows, :].T.astype(BF16)
            qq_t = q_ref[rows, :].T
            ct = cost_ref[i]
            st = sint_ref[i]
            for j in range(2):
                xt = qq_t[j * dk:(j + 1) * dk, :]
                inv = lax.rsqrt(jnp.mean(xt * xt, axis=0, keepdims=True) + EPS)
                xn = xt * inv * qgt_ref[...]
                swapped = jnp.concatenate([xn[dk // 2:, :], xn[:dk // 2, :]], axis=0)
                qt_s[j, i] = (xn * ct + swapped * st).astype(BF16)
            return 0

        lax.fori_loop(0, SEQ // CTX_LEN, prep, 0)
        kc = kc_ref[...]
        k0_s[SEQ:, :] = _rms_rows(kc[:, :dk], kg).astype(BF16)
        k1_s[SEQ:, :] = _rms_rows(kc[:, dk:], kg).astype(BF16)
        vt_s[SEQ // CTX_LEN] = vc_ref[...].T.astype(BF16)

    lam = lam_ref[0, 0]
    qpt = DF_TQ // CTX_LEN
    q0 = pl.program_id(2) * qpt
    qt = [jnp.concatenate([qt_s[i, q0 + j] for j in range(qpt)], axis=1) for i in range(2)]
    k_s = (k0_s, k1_s)

    m = [None, None]
    l = [None, None]
    acc = [None, None]
    cpt = DF_TK // CTX_LEN
    jobs = [(t, i) for t in range(DF_KTILES) for i in range(2)]

    def scores(job):
        t, i = job
        return _dot(k_s[i][t * DF_TK:(t + 1) * DF_TK, :], qt[i])

    pending = [scores(job) for job in jobs[:DF_SCORE_LOOKAHEAD]]
    for n, (t, i) in enumerate(jobs):
        s = pending.pop(0)
        if n + DF_SCORE_LOOKAHEAD < len(jobs):
            pending.append(scores(jobs[n + DF_SCORE_LOOKAHEAD]))
        vt = jnp.concatenate([vt_s[t * cpt + j] for j in range(cpt)], axis=1)
        tile_max = jnp.max(s, axis=0, keepdims=True)
        if t == 0:
            m[i] = tile_max
            e = jnp.exp(s - m[i])
            l[i] = jnp.sum(e, axis=0, keepdims=True)
            acc[i] = _dot(vt, e.astype(BF16))
        else:
            m_new = jnp.maximum(m[i], tile_max)
            alpha = jnp.exp(m[i] - m_new)
            e = jnp.exp(s - m_new)
            l[i] = alpha * l[i] + jnp.sum(e, axis=0, keepdims=True)
            acc[i] = alpha * acc[i] + _dot(vt, e.astype(BF16))
            m[i] = m_new
    o = (acc[0] * (1.0 / l[0]) - acc[1] * (lam / l[1])).T
    o_ref[...] = (_rms_rows(o, sg_ref[...]) * out_scale).astype(o_ref.dtype)


def diff_attention(p_lat, p_ctx_kv, lam, cos2, sin2, q_g, k_g, subln_g, out_scale):
    nq = SEQ // DF_TQ
    w = 2 * DIFF_DK
    kcol0 = D_MODEL // w
    vcol0 = 2 * D_MODEL // w
    vec = lambda n: pl.BlockSpec((1, n), lambda b, h, i: (0, 0))
    n_chunks = SEQ // CTX_LEN
    chunked_t = lambda tab: tab.reshape(n_chunks, CTX_LEN, DIFF_DK).transpose(0, 2, 1)
    q_gain_t = jnp.broadcast_to((q_g.astype(F32) * DIFF_DK ** -0.5)[:, None], (DIFF_DK, CTX_LEN))
    return pl.pallas_call(
        functools.partial(_diff_body, out_scale=out_scale),
        out_shape=jax.ShapeDtypeStruct((N_LAT, D_MODEL), BF16),
        grid=(BATCH, DIFF_HEADS, nq),
        in_specs=[
            pl.BlockSpec(memory_space=pltpu.SMEM),
            pl.BlockSpec((SEQ, w), lambda b, h, i: (b, h)),
            pl.BlockSpec((SEQ, w), lambda b, h, i: (b, kcol0 + h)),
            pl.BlockSpec((CTX_LEN, w), lambda b, h, i: (b, h)),
            pl.BlockSpec((SEQ, w), lambda b, h, i: (b, vcol0 + h)),
            pl.BlockSpec((CTX_LEN, w), lambda b, h, i: (b, kcol0 + h)),
            pl.BlockSpec((SEQ, DIFF_DK), lambda b, h, i: (0, 0)),
            pl.BlockSpec((SEQ, DIFF_DK), lambda b, h, i: (0, 0)),
            pl.BlockSpec((n_chunks, DIFF_DK, CTX_LEN), lambda b, h, i: (0, 0, 0)),
            pl.BlockSpec((n_chunks, DIFF_DK, CTX_LEN), lambda b, h, i: (0, 0, 0)),
            pl.BlockSpec((DIFF_DK, CTX_LEN), lambda b, h, i: (0, 0)),
            vec(DIFF_DK), vec(DIFF_DV),
        ],
        out_specs=pl.BlockSpec((DF_TQ, w), lambda b, h, i: (b * nq + i, h)),
        scratch_shapes=[pltpu.VMEM((2, SEQ // CTX_LEN, DIFF_DK, CTX_LEN), BF16),
                        pltpu.VMEM((DF_NK, DIFF_DK), BF16), pltpu.VMEM((DF_NK, DIFF_DK), BF16),
                        pltpu.VMEM((DF_NK // CTX_LEN, DIFF_DV, CTX_LEN), BF16)],
        compiler_params=_cparams(("arbitrary", "arbitrary", "arbitrary")),
        name="diff_attention",
    )(lam, p_lat, p_lat, p_ctx_kv, p_lat, p_ctx_kv, cos2, sin2, chunked_t(cos2), chunked_t(sin2), q_gain_t,
      k_g.reshape(1, DIFF_DK), subln_g.reshape(1, DIFF_DV))


def _rope_tables():
    t = jnp.arange(SEQ)
    row = (t // GRID_W).astype(F32)
    col = (t % GRID_W).astype(F32)
    n_freq = DIFF_DK // 4
    inv = ROPE_THETA ** (-jnp.arange(n_freq, dtype=F32) / n_freq)
    ang = jnp.concatenate([row[:, None] * inv, col[:, None] * inv], axis=-1)
    cos, sin = jnp.cos(ang), jnp.sin(ang)
    return jnp.concatenate([cos, cos], axis=-1), jnp.concatenate([-sin, sin], axis=-1)


def _mlp(x, mod, g, w1, w2, layer, is_ctx):
    a = norm_matmul(x, g, mod, w1, layer, shift_idx=3, is_ctx=is_ctx, out_dtype=BF16, relu2=True, name="mlp_up")
    return res_matmul([a], w2, layer, x, mod, gate_idx=5, is_ctx=is_ctx, name="mlp_down")


def kernel(x, c, ctx, c_ctx, ada_w, ada_b, norm_mix_g, norm_mlp_g, mlp_w1, mlp_w2, ev_w_in, ev_w_out, na_q_g, na_k_g, na_rpb, hg_lb_logits, hg_gnorm_g, od_w_in, od_w_out, df_q_g, df_k_g, df_lambda, df_subln_g):
    xl = x.reshape(N_LAT, D_MODEL)
    xc = ctx.reshape(N_CTX, D_MODEL)
    cond = jnp.concatenate([c, c_ctx[None], jnp.zeros((MOD_ROWS - BATCH - 1, D_MODEL), F32)], axis=0)
    mods = ada_table(cond, ada_w, ada_b)
    lb_all = jnp.cumsum(jax.nn.softmax(hg_lb_logits.astype(F32), axis=1), axis=1)
    cos2, sin2 = _rope_tables()
    w1, w2 = mlp_w1.astype(BF16), mlp_w2.astype(BF16)
    ev_in, ev_out = ev_w_in.astype(BF16), ev_w_out.astype(BF16)
    od_in, od_out = od_w_in.astype(BF16), od_w_out.astype(BF16)

    for l in range(DEPTH):
        need_ctx = l < DEPTH - 1
        mod = mods[l]
        g_mix = norm_mix_g[l]
        if l % 2 == 0:
            e = l // 2
            p_lat = norm_matmul(xl, g_mix, mod, ev_in, e, shift_idx=0, is_ctx=False, out_dtype=F32, name="even_in")
            p_ctx = norm_matmul(xc, g_mix, mod, ev_in, e, shift_idx=0, is_ctx=True, out_dtype=F32,
                                name="even_in_ctx")
            na_lat, na_ctx = na_attention(p_lat, p_ctx, na_q_g[e], na_k_g[e], na_bias_table(na_rpb[e]))
            hg_lat, hg_ctx = hgrn2(p_lat, p_ctx, lb_all[0, l], lb_all[1, l], hg_gnorm_g[e])
            xl = res_matmul([na_lat, hg_lat], ev_out, e, xl, mod, gate_idx=2, is_ctx=False, name="mix_out")
            if need_ctx:
                xc = res_matmul([na_ctx, hg_ctx], ev_out, e, xc, mod, gate_idx=2, is_ctx=True, name="mix_out_ctx")
        else:
            assert not need_ctx, "an odd layer followed by another layer needs context outputs"
            o = l // 2
            p_lat = norm_matmul(xl, g_mix, mod, od_in, o, shift_idx=0, is_ctx=False, out_dtype=F32, name="odd_in")
            p_ctx = norm_matmul(xc, g_mix, mod, od_in, o, shift_idx=0, is_ctx=True, out_dtype=F32,
                                col_block0=D_MODEL // MM_TILE, n_cols=2 * D_MODEL, name="odd_in_ctx")
            lam_init = 0.8 - 0.6 * math.exp(-0.3 * l)
            lp = df_lambda[o].astype(F32)
            lam = jnp.exp(jnp.sum(lp[0] * lp[1])) - jnp.exp(jnp.sum(lp[2] * lp[3])) + lam_init
            om = diff_attention(p_lat, p_ctx, lam.reshape(1, 1), cos2, sin2, df_q_g[o], df_k_g[o], df_subln_g[o],
                                1.0 - lam_init)
            xl = res_matmul([om], od_out, o, xl, mod, gate_idx=2, is_ctx=False, name="mix_out")
        xl = _mlp(xl, mod, norm_mlp_g[l], w1, w2, l, False)
        if need_ctx:
            xc = _mlp(xc, mod, norm_mlp_g[l], w1, w2, l, True)
    return xl.reshape(BATCH, SEQ, D_MODEL)
```

```python
import functools
import math

import jax
import jax.numpy as jnp
import numpy as np
from jax import lax
from jax.experimental import pallas as pl
from jax.experimental.pallas import tpu as pltpu

F32 = jnp.float32
BF16 = jnp.bfloat16

D_MODEL = 2048
BATCH = 4
SEQ = 2048
DEPTH = 2
GRID_W = 64
GRID_ROWS = SEQ // GRID_W
CTX_LEN = 256
HEAD_DIM = 128
NA_HEADS = 8
NA_WIN_R = 8
NA_WIN_C = 16
NA_WIDTH = NA_HEADS * HEAD_DIM
HG_HEADS = 8
HG_CHUNK = 16
HG_WIDTH = HG_HEADS * HEAD_DIM
EVEN_IN = 8 * NA_WIDTH
DIFF_HEADS = 8
DIFF_DK = HEAD_DIM
DIFF_DV = 2 * HEAD_DIM
ODD_IN = 3 * D_MODEL
D_FF = 4 * D_MODEL
N_MOD = 6
ROPE_THETA = 10000.0
EPS = 1e-6

N_LAT = BATCH * SEQ
N_CTX = BATCH * CTX_LEN
MOD_ROWS = 8
CTX_MOD_ROW = BATCH
NEG_BIG = -1e30
LOG2E = math.log2(math.e)

VMEM_LIMIT_V7X = 56 * 1024 * 1024
MM_TILE = 1024
MM_TK = 2048


def _cparams(sem, vmem=VMEM_LIMIT_V7X):
    return pltpu.CompilerParams(dimension_semantics=sem, vmem_limit_bytes=vmem)


def _dot(a, b):
    return jnp.dot(a, b, preferred_element_type=F32)


def _dot_nt(a, b):
    return lax.dot_general(a, b, (((1,), (1,)), ((), ())), preferred_element_type=F32)


def _rms_rows(x, g):
    return x * lax.rsqrt(jnp.mean(x * x, axis=-1, keepdims=True) + EPS) * g


def _mod_row_of_tile(is_ctx, tile_rows):
    if is_ctx:
        return lambda i: CTX_MOD_ROW
    return lambda i: (i * tile_rows) // SEQ


def _ada_body(c_ref, w_ref, b_ref, o_ref):
    c = c_ref[...]
    s = (c * jax.nn.sigmoid(c)).astype(BF16)
    o_ref[0] = _dot(s, w_ref[0].astype(BF16)) + b_ref[0]


def ada_table(cond, ada_w, ada_b):
    tn = 1024
    n = N_MOD * D_MODEL
    out = pl.pallas_call(
        _ada_body,
        out_shape=jax.ShapeDtypeStruct((DEPTH, MOD_ROWS, n), F32),
        grid=(DEPTH, n // tn),
        in_specs=[
            pl.BlockSpec((MOD_ROWS, D_MODEL), lambda l, j: (0, 0)),
            pl.BlockSpec((1, D_MODEL, tn), lambda l, j: (l, 0, j)),
            pl.BlockSpec((1, 1, tn), lambda l, j: (l, 0, j)),
        ],
        out_specs=pl.BlockSpec((1, MOD_ROWS, tn), lambda l, j: (l, 0, j)),
        compiler_params=_cparams(("arbitrary", "arbitrary")),
        name="ada_table",
    )(cond, ada_w, ada_b.reshape(DEPTH, 1, n))
    return out.reshape(DEPTH, MOD_ROWS, N_MOD, D_MODEL)


def _norm_mm_body(x_ref, g_ref, mod_ref, w_ref, o_ref, h_ref, *, shift_idx, relu2):
    @pl.when(pl.program_id(1) == 0)
    def _():
        shift = mod_ref[0, shift_idx:shift_idx + 1, :]
        scale = mod_ref[0, shift_idx + 1:shift_idx + 2, :]
        h_ref[...] = (_rms_rows(x_ref[...], g_ref[...]) * (1.0 + scale) + shift).astype(BF16)

    acc = _dot(h_ref[...], w_ref[...])
    if relu2:
        acc = jnp.maximum(acc, 0.0)
        acc = acc * acc
    o_ref[...] = acc.astype(o_ref.dtype)


def norm_matmul(x, g, mod, w, w_idx, *, shift_idx, is_ctx, out_dtype, relu2=False, col_block0=0, n_cols=None, name):
    n_rows = x.shape[0]
    tm = MM_TILE
    tn = MM_TILE * (jnp.dtype(F32).itemsize // jnp.dtype(out_dtype).itemsize)
    col_block0 = col_block0 * MM_TILE // tn
    n_cols = w.shape[2] if n_cols is None else n_cols
    mod_row = _mod_row_of_tile(is_ctx, tm)
    return pl.pallas_call(
        functools.partial(_norm_mm_body, shift_idx=shift_idx, relu2=relu2),
        out_shape=jax.ShapeDtypeStruct((n_rows, n_cols), out_dtype),
        grid=(n_rows // tm, n_cols // tn),
        in_specs=[
            pl.BlockSpec((tm, D_MODEL), lambda i, j: (i, 0)),
            pl.BlockSpec((1, D_MODEL), lambda i, j: (0, 0)),
            pl.BlockSpec((1, N_MOD, D_MODEL), lambda i, j: (mod_row(i), 0, 0)),
            pl.BlockSpec((None, D_MODEL, tn), lambda i, j: (w_idx, 0, col_block0 + j)),
        ],
        out_specs=pl.BlockSpec((tm, tn), lambda i, j: (i, j)),
        scratch_shapes=[pltpu.VMEM((tm, D_MODEL), BF16)],
        compiler_params=_cparams(("arbitrary", "arbitrary")),
        name=name,
    )(x, g.reshape(1, D_MODEL), mod, w)


def _res_mm_body(*refs, n_x, nk, gate_idx):
    x_refs = refs[:n_x]
    w_ref, res_ref, mod_ref, o_ref = refs[n_x:n_x + 4]
    scratch = refs[n_x + 4:]

    def partial_product():
        acc = None
        k0 = 0
        for x_ref in x_refs:
            kw = x_ref.shape[1]
            d = _dot(x_ref[...], w_ref[k0:k0 + kw, :])
            acc = d if acc is None else acc + d
            k0 += kw
        return acc

    def finish(acc):
        gate = mod_ref[0, gate_idx:gate_idx + 1, :]
        o_ref[...] = res_ref[...] + gate * acc

    if nk == 1:
        finish(partial_product())
    else:
        acc_ref = scratch[0]
        k = pl.program_id(2)

        @pl.when(k == 0)
        def _():
            acc_ref[...] = jnp.zeros_like(acc_ref)

        acc_ref[...] += partial_product()

        @pl.when(k == nk - 1)
        def _():
            finish(acc_ref[...])


def res_matmul(xs, w, w_idx, res, mod, *, gate_idx, is_ctx, name):
    n_rows = res.shape[0]
    _, k_dim, n_dim = w.shape
    tm = tn = MM_TILE
    if len(xs) == 1:
        tk = MM_TK
        x_specs = [pl.BlockSpec((tm, tk), lambda i, j, k: (i, k))]
    else:
        tk = k_dim
        x_specs = [pl.BlockSpec((tm, x.shape[1]), lambda i, j, k: (i, 0)) for x in xs]
    nk = k_dim // tk
    mod_row = _mod_row_of_tile(is_ctx, tm)
    return pl.pallas_call(
        functools.partial(_res_mm_body, n_x=len(xs), nk=nk, gate_idx=gate_idx),
        out_shape=jax.ShapeDtypeStruct((n_rows, n_dim), F32),
        grid=(n_rows // tm, n_dim // tn, nk),
        in_specs=x_specs + [
            pl.BlockSpec((None, tk, tn), lambda i, j, k: (w_idx, k, j)),
            pl.BlockSpec((tm, tn), lambda i, j, k: (i, j)),
            pl.BlockSpec((1, N_MOD, tn), lambda i, j, k: (mod_row(i), 0, j)),
        ],
        out_specs=pl.BlockSpec((tm, tn), lambda i, j, k: (i, j)),
        scratch_shapes=[pltpu.VMEM((tm, tn), F32)] if nk > 1 else [],
        compiler_params=_cparams(("arbitrary", "arbitrary", "arbitrary")),
        name=name,
    )(*xs, w, res, mod)


NA_GROUP_ROWS = 4
NA_GROUP_Q = NA_GROUP_ROWS * GRID_W
NA_GROUP_KROWS = NA_GROUP_ROWS + NA_WIN_R
NA_GROUP_K = NA_GROUP_KROWS * GRID_W
NA_GROUPS = GRID_ROWS // NA_GROUP_ROWS
NA_PREP_ROWS = 256
NA_SCORE_LOOKAHEAD = 1


def _na_window_row0(rq):
    return min(max(rq - NA_WIN_R // 2, 0), GRID_ROWS - NA_WIN_R)


def _na_group_key_row0(r):
    return min(max(r - NA_WIN_R // 2, 0), GRID_ROWS - NA_GROUP_KROWS)


def _na_group_signature(r):
    k0 = _na_group_key_row0(r)
    return (k0 - r,) + tuple(_na_window_row0(r + j) - k0 for j in range(NA_GROUP_ROWS))


NA_GROUP_KINDS = tuple(sorted({_na_group_signature(r): r for r in reversed(range(0, GRID_ROWS, NA_GROUP_ROWS))}.values()))


def _na_group_kind(r):
    return [_na_group_signature(k) for k in NA_GROUP_KINDS].index(_na_group_signature(r))


def na_bias_table(rpb):
    w = GRID_W
    col = np.arange(w)
    c0 = np.clip(col - NA_WIN_C // 2, 0, w - NA_WIN_C)
    in_win = (col[None, :] >= c0[:, None]) & (col[None, :] < c0[:, None] + NA_WIN_C)
    padded = jnp.pad(rpb.astype(F32), ((0, 0), (0, 0), (w, w)))
    toep = jnp.stack([padded[:, :, NA_WIN_C - 1 - qc + w:NA_WIN_C - 1 - qc + 2 * w] for qc in range(w)], axis=2)
    toep = jnp.where(in_win[None, None], toep * LOG2E, NEG_BIG)
    masked = jnp.full((NA_HEADS, w, w), NEG_BIG, F32)
    kinds = []
    for r in NA_GROUP_KINDS:
        q_rows = []
        for rq in range(r, r + NA_GROUP_ROWS):
            r0 = _na_window_row0(rq)
            blocks = []
            for kr in range(_na_group_key_row0(r), _na_group_key_row0(r) + NA_GROUP_KROWS):
                blocks.append(toep[:, kr - rq + NA_WIN_R - 1] if r0 <= kr < r0 + NA_WIN_R else masked)
            q_rows.append(jnp.concatenate(blocks, axis=2))
        kinds.append(jnp.concatenate(q_rows, axis=1))
    return jnp.stack(kinds, axis=1)


def _na_body(q_ref, k_ref, v_ref, qc_ref, kc_ref, vc_ref, qg_ref, kg_ref, bias_ref,
             o_ref, oc_ref, qn_ref, knt_ref, vn_ref, kcnt_ref, vcn_ref):
    scale = HEAD_DIM ** -0.5 * LOG2E
    qg = qg_ref[...]
    kg = kg_ref[...]

    def with_ones(v):
        return jnp.concatenate([v.astype(BF16), jnp.ones(v.shape, BF16)], axis=1)

    def normalised(o_ext):
        return (o_ext[:, :HEAD_DIM] / o_ext[:, HEAD_DIM:]).astype(BF16)

    for i in range(SEQ // NA_PREP_ROWS):
        rows = slice(i * NA_PREP_ROWS, (i + 1) * NA_PREP_ROWS)
        qn_ref[rows, :] = (_rms_rows(q_ref[rows, :], qg) * scale).astype(BF16)
        knt_ref[:, rows] = _rms_rows(k_ref[rows, :], kg).T.astype(BF16)
        vn_ref[rows, :] = with_ones(v_ref[rows, :])
    kcnt = _rms_rows(kc_ref[...], kg).T.astype(BF16)
    vcn = with_ones(vc_ref[...])
    kcnt_ref[...] = kcnt
    vcn_ref[...] = vcn

    def group_rows(gi):
        k0 = _na_group_key_row0(gi * NA_GROUP_ROWS) * GRID_W
        return slice(gi * NA_GROUP_Q, (gi + 1) * NA_GROUP_Q), slice(k0, k0 + NA_GROUP_K)

    def scores(gi):
        qrows, krows = group_rows(gi)
        qb = qn_ref[qrows, :]
        return _dot(qb, knt_ref[:, krows]), _dot(qb, kcnt_ref[...])

    pending = [scores(gi) for gi in range(NA_SCORE_LOOKAHEAD)]
    for gi in range(NA_GROUPS):
        s1, s2 = pending.pop(0)
        if gi + NA_SCORE_LOOKAHEAD < NA_GROUPS:
            pending.append(scores(gi + NA_SCORE_LOOKAHEAD))
        qrows, krows = group_rows(gi)
        s1 = s1 + bias_ref[0, _na_group_kind(gi * NA_GROUP_ROWS)]
        m = jnp.maximum(jnp.max(s1, axis=-1, keepdims=True), jnp.max(s2, axis=-1, keepdims=True))
        p1 = jnp.exp2(s1 - m)
        p2 = jnp.exp2(s2 - m)
        o_ref[qrows, :] = normalised(_dot(p1.astype(BF16), vn_ref[krows, :]) + _dot(p2.astype(BF16), vcn_ref[...]))

    qcn = (_rms_rows(qc_ref[...], qg) * scale).astype(BF16)
    s = _dot(qcn, kcnt)
    m = jnp.max(s, axis=-1, keepdims=True)
    p = jnp.exp2(s - m)
    oc_ref[...] = normalised(_dot(p.astype(BF16), vcn))


def na_attention(p_lat, p_ctx, q_g, k_g, bias):
    hd = HEAD_DIM
    lat = lambda seg: pl.BlockSpec((SEQ, hd), lambda h, b: (b, seg * NA_HEADS + h))
    ctx = lambda seg: pl.BlockSpec((CTX_LEN, hd), lambda h, b: (b, seg * NA_HEADS + h))
    return pl.pallas_call(
        _na_body,
        out_shape=(jax.ShapeDtypeStruct((N_LAT, NA_WIDTH), BF16),
                   jax.ShapeDtypeStruct((N_CTX, NA_WIDTH), BF16)),
        grid=(NA_HEADS, BATCH),
        in_specs=[lat(0), lat(1), lat(2), ctx(0), ctx(1), ctx(2),
                  pl.BlockSpec((1, hd), lambda h, b: (0, 0)),
                  pl.BlockSpec((1, hd), lambda h, b: (0, 0)),
                  pl.BlockSpec((1, len(NA_GROUP_KINDS), NA_GROUP_Q, NA_GROUP_K), lambda h, b: (h, 0, 0, 0))],
        out_specs=(pl.BlockSpec((SEQ, hd), lambda h, b: (b, h)),
                   pl.BlockSpec((CTX_LEN, hd), lambda h, b: (b, h))),
        scratch_shapes=[pltpu.VMEM((SEQ, hd), BF16), pltpu.VMEM((hd, SEQ), BF16), pltpu.VMEM((SEQ, 2 * hd), BF16),
                        pltpu.VMEM((hd, CTX_LEN), BF16), pltpu.VMEM((CTX_LEN, 2 * hd), BF16)],
        compiler_params=_cparams(("arbitrary", "arbitrary")),
        name="na_attention",
    )(p_lat, p_lat, p_lat, p_ctx, p_ctx, p_ctx, q_g.reshape(1, hd), k_g.reshape(1, hd), bias)


HG_BLK = 128
HG_CPB = HG_BLK // HG_CHUNK
HG_CTX_BLKS = CTX_LEN // HG_BLK
HG_LAT_BLKS = SEQ // HG_BLK
HG_BLKS = HG_CTX_BLKS + HG_LAT_BLKS
HG_SCAN_STEPS = 3


def _hg_ref_rows(reverse):
    b, c = HG_BLK, HG_CHUNK
    if not reverse:
        return {c: [None] + [c * i - 1 for i in range(1, b // c)],
                32: [32 * j + 15 for j in range(b // 32)], 64: [64 * j + 31 for j in range(b // 64)],
                128: [63], "end": [b - 1]}
    return {c: [c * i + c for i in range(b // c - 1)] + [None],
            32: [32 * j + 16 for j in range(b // 32)], 64: [64 * j + 32 for j in range(b // 64)],
            128: [64], "end": [0]}


def _hg_level_codes(reverse):
    t = lax.broadcasted_iota(jnp.int32, (HG_BLK, HG_BLK), 0)
    s = lax.broadcasted_iota(jnp.int32, (HG_BLK, HG_BLK), 1)
    if reverse:
        t, s = s, t
    code = jnp.where((t >> 6) > (s >> 6), 4, 0)
    code = jnp.where(((t >> 6) == (s >> 6)) & ((t >> 5) > (s >> 5)), 3, code)
    code = jnp.where(((t >> 5) == (s >> 5)) & ((t >> 4) > (s >> 4)), 2, code)
    return jnp.where(((t >> 4) == (s >> 4)) & (s <= t), 1, code)


def _hg_body(q_ref, zf_ref, zb_ref, i_ref, g_ref, qc_ref, zfc_ref, zbc_ref, ic_ref, gc_ref,
             lbf_ref, lbb_ref, gn_ref, o_ref, oc_ref,
             qd_s, ka_s, q64_s, k64_s, q128_s, k128_s, qb_s, ke_s, gb_s, v_s, o_s, st_s, x_s, code_s):
    lbs = (lbf_ref[...], lbb_ref[...])
    ri = lax.broadcasted_iota(jnp.int32, (HG_BLK, HG_BLK), 0)
    ci = lax.broadcasted_iota(jnp.int32, (HG_BLK, HG_BLK), 1)
    tri = jnp.where(ci <= ri, 1.0, 0.0).astype(BF16)
    for d in range(2):
        code_s[d] = _hg_level_codes(d == 1)

    def prefix_rows(x):
        hi = x.astype(BF16)
        r1 = x - hi.astype(F32)
        mid = r1.astype(BF16)
        lo = (r1 - mid.astype(F32)).astype(BF16)
        y = _dot(tri, jnp.concatenate([hi, mid, lo], axis=1))
        return y[:, :HEAD_DIM] + y[:, HEAD_DIM:2 * HEAD_DIM] + y[:, 2 * HEAD_DIM:]

    def ref_rows(d, rows):
        group = HG_BLK // len(rows)
        parts = [jnp.zeros((group, HEAD_DIM), F32) if r is None else
                 jnp.broadcast_to(x_s[d, r:r + 1, :], (group, HEAD_DIM)) for r in rows]
        return parts[0] if len(parts) == 1 else jnp.concatenate(parts, axis=0)

    def prep_block(blk, q, zf, zb, v):
        qs = q * jax.nn.sigmoid(q)
        v_s[blk] = v.astype(BF16)
        for d, z in enumerate((zf, zb)):
            f = lbs[d] + (1.0 - lbs[d]) * jax.nn.sigmoid(z)
            logf = jnp.log2(f)
            k = 1.0 - f
            x = prefix_rows(logf)
            x_s[d] = x
            if d == 1:
                x = ref_rows(d, [HG_BLK - 1]) - x + logf
                x_s[d] = x
            rows = _hg_ref_rows(d == 1)
            r16, r32, r64, r128 = (ref_rows(d, rows[g]) for g in (HG_CHUNK, 32, 64, 128))
            x_end = ref_rows(d, rows["end"])
            e32 = jnp.exp2(-jnp.abs(x - r32))
            e64 = jnp.exp2(-jnp.abs(x - r64))
            e128 = jnp.exp2(-jnp.abs(x - r128))
            qd_s[d, blk] = (qs * jnp.exp2(x - r16)).astype(BF16)
            ka_s[d, blk] = jnp.concatenate([(k * jnp.exp2(r16 - x)).T, (k * e32).T], axis=1).astype(BF16)
            q64_s[d, blk] = (qs * e64).astype(BF16)
            k64_s[d, blk] = (k * e64).T.astype(BF16)
            q128_s[d, blk] = (qs * e128).astype(BF16)
            k128_s[d, blk] = (k * e128).T.astype(BF16)
            qb_s[d, blk] = (qs * jnp.exp2(x)).astype(BF16)
            ke_s[d, blk] = (k * jnp.exp2(x_end - x)).T.astype(BF16)
            gb_s[d, blk] = jnp.exp2(x_end).T

    for cb in range(HG_CTX_BLKS):
        rows = slice(cb * HG_BLK, (cb + 1) * HG_BLK)
        prep_block(cb, qc_ref[rows, :], zfc_ref[rows, :], zbc_ref[rows, :], ic_ref[rows, :])

    def prep_lat(n, _):
        rows = pl.ds(pl.multiple_of(n * HG_BLK, HG_BLK), HG_BLK)
        prep_block(HG_CTX_BLKS + n, q_ref[rows, :], zf_ref[rows, :], zb_ref[rows, :], i_ref[rows, :])
        return 0

    lax.fori_loop(0, HG_LAT_BLKS, prep_lat, 0, unroll=2)

    st_s[...] = jnp.zeros_like(st_s)

    def scan_steps(i, _):
        chains = []
        for n in [i * HG_SCAN_STEPS + j for j in range(HG_SCAN_STEPS)]:
            chains.append((0, n))
            chains.append((1, jnp.where(n < HG_CTX_BLKS, HG_CTX_BLKS - 1 - n, HG_BLKS + HG_CTX_BLKS - 1 - n)))
        ready = []
        for d, blk in chains:
            a1 = _dot(qd_s[d, blk], ka_s[d, blk])
            a64 = _dot(q64_s[d, blk], k64_s[d, blk])
            a128 = _dot(q128_s[d, blk], k128_s[d, blk])
            v = v_s[blk]
            inc = _dot(ke_s[d, blk], v)
            code = code_s[d]
            att = jnp.where(code == 1, a1[:, :HG_BLK],
                            jnp.where(code == 2, a1[:, HG_BLK:],
                                      jnp.where(code == 3, a64, jnp.where(code == 4, a128, 0.0))))
            ready.append((jnp.concatenate([att.astype(BF16), qb_s[d, blk]], axis=1), v, inc))
        for (d, blk), (lhs, v, inc) in zip(chains, ready):
            state = st_s[d]
            o_s[d, blk] = _dot(lhs, jnp.concatenate([v, state.astype(BF16)], axis=0))
            st_s[d] = gb_s[d, blk] * state + inc
        return 0

    lax.fori_loop(0, HG_BLKS // HG_SCAN_STEPS, scan_steps, 0)

    gn = gn_ref[...]

    def finish(blk, gate):
        o = o_s[0, blk] + o_s[1, blk]
        return (_rms_rows(o, gn) * (gate * jax.nn.sigmoid(gate))).astype(BF16)

    for cb in range(HG_CTX_BLKS):
        rows = slice(cb * HG_BLK, (cb + 1) * HG_BLK)
        oc_ref[rows, :] = finish(cb, gc_ref[rows, :])

    def fin_lat(n, _):
        rows = pl.ds(pl.multiple_of(n * HG_BLK, HG_BLK), HG_BLK)
        o_ref[rows, :] = finish(HG_CTX_BLKS + n, g_ref[rows, :])
        return 0

    lax.fori_loop(0, HG_LAT_BLKS, fin_lat, 0)


def hgrn2(p_lat, p_ctx, lb_fwd, lb_bwd, gn_g):
    hd = HEAD_DIM
    lat = lambda seg: pl.BlockSpec((SEQ, hd), lambda b, h: (b, seg * HG_HEADS + h))
    ctx = lambda seg: pl.BlockSpec((CTX_LEN, hd), lambda b, h: (b, seg * HG_HEADS + h))
    head_vec = pl.BlockSpec((1, hd), lambda b, h: (0, h))
    dir_bf = lambda width=hd: pltpu.VMEM((2, HG_BLKS, HG_BLK, width), BF16)
    dir_f32 = lambda: pltpu.VMEM((2, HG_BLKS, HG_BLK, hd), F32)
    return pl.pallas_call(
        _hg_body,
        out_shape=(jax.ShapeDtypeStruct((N_LAT, HG_WIDTH), BF16),
                   jax.ShapeDtypeStruct((N_CTX, HG_WIDTH), BF16)),
        grid=(BATCH, HG_HEADS),
        in_specs=[lat(3), lat(4), lat(5), lat(6), lat(7), ctx(3), ctx(4), ctx(5), ctx(6), ctx(7),
                  head_vec, head_vec, pl.BlockSpec((1, hd), lambda b, h: (0, 0))],
        out_specs=(pl.BlockSpec((SEQ, hd), lambda b, h: (b, h)),
                   pl.BlockSpec((CTX_LEN, hd), lambda b, h: (b, h))),
        scratch_shapes=[dir_bf(), dir_bf(2 * hd), dir_bf(), dir_bf(), dir_bf(), dir_bf(), dir_bf(), dir_bf(),
                        dir_f32(), pltpu.VMEM((HG_BLKS, HG_BLK, hd), BF16), dir_f32(),
                        pltpu.VMEM((2, hd, hd), F32), pltpu.VMEM((2, HG_BLK, hd), F32),
                        pltpu.VMEM((2, HG_BLK, HG_BLK), jnp.int32)],
        compiler_params=_cparams(("arbitrary", "arbitrary")),
        name="hgrn2",
    )(p_lat, p_lat, p_lat, p_lat, p_lat, p_ctx, p_ctx, p_ctx, p_ctx, p_ctx,
      lb_fwd.reshape(1, -1), lb_bwd.reshape(1, -1), gn_g.reshape(1, hd))


DF_TQ = 512
DF_NK = SEQ + CTX_LEN
DF_TK = 768
DF_KTILES = DF_NK // DF_TK
DF_SCORE_LOOKAHEAD = 2


def _rope(x, cos2, sin2):
    return x * cos2 + pltpu.roll(x, DIFF_DK // 2, axis=1) * sin2


def _diff_body(lam_ref, q_ref, k_ref, kc_ref, v_ref, vc_ref, cos_ref, sin_ref, cost_ref, sint_ref,
               qgt_ref, kg_ref, sg_ref, o_ref, qt_s, k0_s, k1_s, vt_s, *, out_scale):
    dk = DIFF_DK
    kg = kg_ref[...]

    @pl.when(pl.program_id(2) == 0)
    def _():
        def prep(i, _):
            rows = pl.ds(pl.multiple_of(i * CTX_LEN, CTX_LEN), CTX_LEN)
            c2 = cos_ref[rows, :]
            s2 = sin_ref[rows, :]
            kk = k_ref[rows, :]
            k0_s[rows, :] = _rope(_rms_rows(kk[:, :dk], kg), c2, s2).astype(BF16)
            k1_s[rows, :] = _rope(_rms_rows(kk[:, dk:], kg), c2, s2).astype(BF16)
            vt_s[i] = v_ref[rows, :].T.astype(BF16)
            qq_t = q_ref[rows, :].T
            ct = cost_ref[i]
            st = sint_ref[i]
            for j in range(2):
                xt = qq_t[j * dk:(j + 1) * dk, :]
                inv = lax.rsqrt(jnp.mean(xt * xt, axis=0, keepdims=True) + EPS)
                xn = xt * inv * qgt_ref[...]
                swapped = jnp.concatenate([xn[dk // 2:, :], xn[:dk // 2, :]], axis=0)
                qt_s[j, i] = (xn * ct + swapped * st).astype(BF16)
            return 0

        lax.fori_loop(0, SEQ // CTX_LEN, prep, 0)
        kc = kc_ref[...]
        k0_s[SEQ:, :] = _rms_rows(kc[:, :dk], kg).astype(BF16)
        k1_s[SEQ:, :] = _rms_rows(kc[:, dk:], kg).astype(BF16)
        vt_s[SEQ // CTX_LEN] = vc_ref[...].T.astype(BF16)

    lam = lam_ref[0, 0]
    qpt = DF_TQ // CTX_LEN
    q0 = pl.program_id(2) * qpt
    qt = [jnp.concatenate([qt_s[i, q0 + j] for j in range(qpt)], axis=1) for i in range(2)]
    k_s = (k0_s, k1_s)

    m = [None, None]
    l = [None, None]
    acc = [None, None]
    cpt = DF_TK // CTX_LEN
    jobs = [(t, i) for t in range(DF_KTILES) for i in range(2)]

    def scores(job):
        t, i = job
        return _dot(k_s[i][t * DF_TK:(t + 1) * DF_TK, :], qt[i])

    pending = [scores(job) for job in jobs[:DF_SCORE_LOOKAHEAD]]
    for n, (t, i) in enumerate(jobs):
        s = pending.pop(0)
        if n + DF_SCORE_LOOKAHEAD < len(jobs):
            pending.append(scores(jobs[n + DF_SCORE_LOOKAHEAD]))
        vt = jnp.concatenate([vt_s[t * cpt + j] for j in range(cpt)], axis=1)
        tile_max = jnp.max(s, axis=0, keepdims=True)
        if t == 0:
            m[i] = tile_max
            e = jnp.exp2(s - m[i])
            l[i] = jnp.sum(e, axis=0, keepdims=True)
            acc[i] = _dot(vt, e.astype(BF16))
        else:
            m_new = jnp.maximum(m[i], tile_max)
            alpha = jnp.exp2(m[i] - m_new)
            e = jnp.exp2(s - m_new)
            l[i] = alpha * l[i] + jnp.sum(e, axis=0, keepdims=True)
            acc[i] = alpha * acc[i] + _dot(vt, e.astype(BF16))
            m[i] = m_new
    o = (acc[0] * (1.0 / l[0]) - acc[1] * (lam / l[1])).T
    o_ref[...] = (_rms_rows(o, sg_ref[...]) * out_scale).astype(o_ref.dtype)


def diff_attention(p_lat, p_ctx_kv, lam, cos2, sin2, q_g, k_g, subln_g, out_scale):
    nq = SEQ // DF_TQ
    w = 2 * DIFF_DK
    kcol0 = D_MODEL // w
    vcol0 = 2 * D_MODEL // w
    vec = lambda n: pl.BlockSpec((1, n), lambda b, h, i: (0, 0))
    n_chunks = SEQ // CTX_LEN
    chunked_t = lambda tab: tab.reshape(n_chunks, CTX_LEN, DIFF_DK).transpose(0, 2, 1)
    q_gain_t = jnp.broadcast_to((q_g.astype(F32) * (DIFF_DK ** -0.5 * LOG2E))[:, None], (DIFF_DK, CTX_LEN))
    return pl.pallas_call(
        functools.partial(_diff_body, out_scale=out_scale),
        out_shape=jax.ShapeDtypeStruct((N_LAT, D_MODEL), BF16),
        grid=(BATCH, DIFF_HEADS, nq),
        in_specs=[
            pl.BlockSpec(memory_space=pltpu.SMEM),
            pl.BlockSpec((SEQ, w), lambda b, h, i: (b, h)),
            pl.BlockSpec((SEQ, w), lambda b, h, i: (b, kcol0 + h)),
            pl.BlockSpec((CTX_LEN, w), lambda b, h, i: (b, h)),
            pl.BlockSpec((SEQ, w), lambda b, h, i: (b, vcol0 + h)),
            pl.BlockSpec((CTX_LEN, w), lambda b, h, i: (b, kcol0 + h)),
            pl.BlockSpec((SEQ, DIFF_DK), lambda b, h, i: (0, 0)),
            pl.BlockSpec((SEQ, DIFF_DK), lambda b, h, i: (0, 0)),
            pl.BlockSpec((n_chunks, DIFF_DK, CTX_LEN), lambda b, h, i: (0, 0, 0)),
            pl.BlockSpec((n_chunks, DIFF_DK, CTX_LEN), lambda b, h, i: (0, 0, 0)),
            pl.BlockSpec((DIFF_DK, CTX_LEN), lambda b, h, i: (0, 0)),
            vec(DIFF_DK), vec(DIFF_DV),
        ],
        out_specs=pl.BlockSpec((DF_TQ, w), lambda b, h, i: (b * nq + i, h)),
        scratch_shapes=[pltpu.VMEM((2, SEQ // CTX_LEN, DIFF_DK, CTX_LEN), BF16),
                        pltpu.VMEM((DF_NK, DIFF_DK), BF16), pltpu.VMEM((DF_NK, DIFF_DK), BF16),
                        pltpu.VMEM((DF_NK // CTX_LEN, DIFF_DV, CTX_LEN), BF16)],
        compiler_params=_cparams(("arbitrary", "arbitrary", "arbitrary")),
        name="diff_attention",
    )(lam, p_lat, p_lat, p_ctx_kv, p_lat, p_ctx_kv, cos2, sin2, chunked_t(cos2), chunked_t(sin2), q_gain_t,
      k_g.reshape(1, DIFF_DK), subln_g.reshape(1, DIFF_DV))


def _rope_tables():
    t = jnp.arange(SEQ)
    row = (t // GRID_W).astype(F32)
    col = (t % GRID_W).astype(F32)
    n_freq = DIFF_DK // 4
    inv = ROPE_THETA ** (-jnp.arange(n_freq, dtype=F32) / n_freq)
    ang = jnp.concatenate([row[:, None] * inv, col[:, None] * inv], axis=-1)
    cos, sin = jnp.cos(ang), jnp.sin(ang)
    return jnp.concatenate([cos, cos], axis=-1), jnp.concatenate([-sin, sin], axis=-1)


def _mlp(x, mod, g, w1, w2, layer, is_ctx):
    a = norm_matmul(x, g, mod, w1, layer, shift_idx=3, is_ctx=is_ctx, out_dtype=BF16, relu2=True, name="mlp_up")
    return res_matmul([a], w2, layer, x, mod, gate_idx=5, is_ctx=is_ctx, name="mlp_down")


def kernel(x, c, ctx, c_ctx, ada_w, ada_b, norm_mix_g, norm_mlp_g, mlp_w1, mlp_w2, ev_w_in, ev_w_out, na_q_g, na_k_g, na_rpb, hg_lb_logits, hg_gnorm_g, od_w_in, od_w_out, df_q_g, df_k_g, df_lambda, df_subln_g):
    xl = x.reshape(N_LAT, D_MODEL)
    xc = ctx.reshape(N_CTX, D_MODEL)
    cond = jnp.concatenate([c, c_ctx[None], jnp.zeros((MOD_ROWS - BATCH - 1, D_MODEL), F32)], axis=0)
    mods = ada_table(cond, ada_w, ada_b)
    lb_all = jnp.cumsum(jax.nn.softmax(hg_lb_logits.astype(F32), axis=1), axis=1)
    cos2, sin2 = _rope_tables()
    w1, w2 = mlp_w1.astype(BF16), mlp_w2.astype(BF16)
    ev_in, ev_out = ev_w_in.astype(BF16), ev_w_out.astype(BF16)
    od_in, od_out = od_w_in.astype(BF16), od_w_out.astype(BF16)

    for l in range(DEPTH):
        need_ctx = l < DEPTH - 1
        mod = mods[l]
        g_mix = norm_mix_g[l]
        if l % 2 == 0:
            e = l // 2
            p_lat = norm_matmul(xl, g_mix, mod, ev_in, e, shift_idx=0, is_ctx=False, out_dtype=F32, name="even_in")
            p_ctx = norm_matmul(xc, g_mix, mod, ev_in, e, shift_idx=0, is_ctx=True, out_dtype=F32,
                                name="even_in_ctx")
            na_lat, na_ctx = na_attention(p_lat, p_ctx, na_q_g[e], na_k_g[e], na_bias_table(na_rpb[e]))
            hg_lat, hg_ctx = hgrn2(p_lat, p_ctx, lb_all[0, l], lb_all[1, l], hg_gnorm_g[e])
            xl = res_matmul([na_lat, hg_lat], ev_out, e, xl, mod, gate_idx=2, is_ctx=False, name="mix_out")
            if need_ctx:
                xc = res_matmul([na_ctx, hg_ctx], ev_out, e, xc, mod, gate_idx=2, is_ctx=True, name="mix_out_ctx")
        else:
            assert not need_ctx, "an odd layer followed by another layer needs context outputs"
            o = l // 2
            p_lat = norm_matmul(xl, g_mix, mod, od_in, o, shift_idx=0, is_ctx=False, out_dtype=F32, name="odd_in")
            p_ctx = norm_matmul(xc, g_mix, mod, od_in, o, shift_idx=0, is_ctx=True, out_dtype=F32,
                                col_block0=D_MODEL // MM_TILE, n_cols=2 * D_MODEL, name="odd_in_ctx")
            lam_init = 0.8 - 0.6 * math.exp(-0.3 * l)
            lp = df_lambda[o].astype(F32)
            lam = jnp.exp(jnp.sum(lp[0] * lp[1])) - jnp.exp(jnp.sum(lp[2] * lp[3])) + lam_init
            om = diff_attention(p_lat, p_ctx, lam.reshape(1, 1), cos2, sin2, df_q_g[o], df_k_g[o], df_subln_g[o],
                                1.0 - lam_init)
            xl = res_matmul([om], od_out, o, xl, mod, gate_idx=2, is_ctx=False, name="mix_out")
        xl = _mlp(xl, mod, norm_mlp_g[l], w1, w2, l, False)
        if need_ctx:
            xc = _mlp(xc, mod, norm_mlp_g[l], w1, w2, l, True)
    return xl.reshape(BATCH, SEQ, D_MODEL)
```

```python
import functools
import math

import jax
import jax.numpy as jnp
import numpy as np
from jax import lax
from jax.experimental import pallas as pl
from jax.experimental.pallas import tpu as pltpu

F32 = jnp.float32
BF16 = jnp.bfloat16

D_MODEL = 2048
BATCH = 4
SEQ = 2048
DEPTH = 2
GRID_W = 64
GRID_ROWS = SEQ // GRID_W
CTX_LEN = 256
HEAD_DIM = 128
NA_HEADS = 8
NA_WIN_R = 8
NA_WIN_C = 16
NA_WIDTH = NA_HEADS * HEAD_DIM
HG_HEADS = 8
HG_CHUNK = 16
HG_WIDTH = HG_HEADS * HEAD_DIM
EVEN_IN = 8 * NA_WIDTH
DIFF_HEADS = 8
DIFF_DK = HEAD_DIM
DIFF_DV = 2 * HEAD_DIM
ODD_IN = 3 * D_MODEL
D_FF = 4 * D_MODEL
N_MOD = 6
ROPE_THETA = 10000.0
EPS = 1e-6

N_LAT = BATCH * SEQ
N_CTX = BATCH * CTX_LEN
MOD_ROWS = 8
CTX_MOD_ROW = BATCH
NEG_BIG = -1e30
LOG2E = math.log2(math.e)
SOFTMAX_NOSHIFT_LOG2 = 100.0

VMEM_LIMIT_V7X = 56 * 1024 * 1024
MM_TILE = 1024
MM_TK = 2048


def _cparams(sem, vmem=VMEM_LIMIT_V7X):
    return pltpu.CompilerParams(dimension_semantics=sem, vmem_limit_bytes=vmem)


def _dot(a, b):
    return jnp.dot(a, b, preferred_element_type=F32)


def _dot_nt(a, b):
    return lax.dot_general(a, b, (((1,), (1,)), ((), ())), preferred_element_type=F32)


def _rms_rows(x, g):
    return x * lax.rsqrt(jnp.mean(x * x, axis=-1, keepdims=True) + EPS) * g


def _mod_row_of_tile(is_ctx, tile_rows):
    if is_ctx:
        return lambda i: CTX_MOD_ROW
    return lambda i: (i * tile_rows) // SEQ


def _ada_body(c_ref, w_ref, b_ref, o_ref):
    c = c_ref[...]
    s = (c * jax.nn.sigmoid(c)).astype(BF16)
    o_ref[0] = _dot(s, w_ref[0].astype(BF16)) + b_ref[0]


def ada_table(cond, ada_w, ada_b):
    tn = 1024
    n = N_MOD * D_MODEL
    out = pl.pallas_call(
        _ada_body,
        out_shape=jax.ShapeDtypeStruct((DEPTH, MOD_ROWS, n), F32),
        grid=(DEPTH, n // tn),
        in_specs=[
            pl.BlockSpec((MOD_ROWS, D_MODEL), lambda l, j: (0, 0)),
            pl.BlockSpec((1, D_MODEL, tn), lambda l, j: (l, 0, j)),
            pl.BlockSpec((1, 1, tn), lambda l, j: (l, 0, j)),
        ],
        out_specs=pl.BlockSpec((1, MOD_ROWS, tn), lambda l, j: (l, 0, j)),
        compiler_params=_cparams(("arbitrary", "arbitrary")),
        name="ada_table",
    )(cond, ada_w, ada_b.reshape(DEPTH, 1, n))
    return out.reshape(DEPTH, MOD_ROWS, N_MOD, D_MODEL)


def _norm_mm_body(x_ref, g_ref, mod_ref, w_ref, o_ref, h_ref, *, shift_idx, relu2):
    @pl.when(pl.program_id(1) == 0)
    def _():
        shift = mod_ref[0, shift_idx:shift_idx + 1, :]
        scale = mod_ref[0, shift_idx + 1:shift_idx + 2, :]
        h_ref[...] = (_rms_rows(x_ref[...], g_ref[...]) * (1.0 + scale) + shift).astype(BF16)

    acc = _dot(h_ref[...], w_ref[...])
    if relu2:
        acc = jnp.maximum(acc, 0.0)
        acc = acc * acc
    o_ref[...] = acc.astype(o_ref.dtype)


def norm_matmul(x, g, mod, w, w_idx, *, shift_idx, is_ctx, out_dtype, relu2=False, col_block0=0, n_cols=None, name):
    n_rows = x.shape[0]
    tm = MM_TILE
    tn = MM_TILE * (jnp.dtype(F32).itemsize // jnp.dtype(out_dtype).itemsize)
    col_block0 = col_block0 * MM_TILE // tn
    n_cols = w.shape[2] if n_cols is None else n_cols
    mod_row = _mod_row_of_tile(is_ctx, tm)
    return pl.pallas_call(
        functools.partial(_norm_mm_body, shift_idx=shift_idx, relu2=relu2),
        out_shape=jax.ShapeDtypeStruct((n_rows, n_cols), out_dtype),
        grid=(n_rows // tm, n_cols // tn),
        in_specs=[
            pl.BlockSpec((tm, D_MODEL), lambda i, j: (i, 0)),
            pl.BlockSpec((1, D_MODEL), lambda i, j: (0, 0)),
            pl.BlockSpec((1, N_MOD, D_MODEL), lambda i, j: (mod_row(i), 0, 0)),
            pl.BlockSpec((None, D_MODEL, tn), lambda i, j: (w_idx, 0, col_block0 + j)),
        ],
        out_specs=pl.BlockSpec((tm, tn), lambda i, j: (i, j)),
        scratch_shapes=[pltpu.VMEM((tm, D_MODEL), BF16)],
        compiler_params=_cparams(("arbitrary", "arbitrary")),
        name=name,
    )(x, g.reshape(1, D_MODEL), mod, w)


def _res_mm_body(*refs, n_x, nk, gate_idx):
    x_refs = refs[:n_x]
    w_ref, res_ref, mod_ref, o_ref = refs[n_x:n_x + 4]
    scratch = refs[n_x + 4:]

    def partial_product():
        acc = None
        k0 = 0
        for x_ref in x_refs:
            kw = x_ref.shape[1]
            d = _dot(x_ref[...], w_ref[k0:k0 + kw, :])
            acc = d if acc is None else acc + d
            k0 += kw
        return acc

    def finish(acc):
        gate = mod_ref[0, gate_idx:gate_idx + 1, :]
        o_ref[...] = res_ref[...] + gate * acc

    if nk == 1:
        finish(partial_product())
    else:
        acc_ref = scratch[0]
        k = pl.program_id(2)

        @pl.when(k == 0)
        def _():
            acc_ref[...] = jnp.zeros_like(acc_ref)

        acc_ref[...] += partial_product()

        @pl.when(k == nk - 1)
        def _():
            finish(acc_ref[...])


def res_matmul(xs, w, w_idx, res, mod, *, gate_idx, is_ctx, name):
    n_rows = res.shape[0]
    _, k_dim, n_dim = w.shape
    tk = MM_TK if len(xs) == 1 else k_dim
    nk = k_dim // tk
    if nk == 1:
        tm, tn = MM_TILE // 2, n_dim
    else:
        tm = tn = MM_TILE
    if len(xs) == 1:
        x_specs = [pl.BlockSpec((tm, tk), lambda i, j, k: (i, k))]
    else:
        x_specs = [pl.BlockSpec((tm, x.shape[1]), lambda i, j, k: (i, 0)) for x in xs]
    mod_row = _mod_row_of_tile(is_ctx, tm)
    return pl.pallas_call(
        functools.partial(_res_mm_body, n_x=len(xs), nk=nk, gate_idx=gate_idx),
        out_shape=jax.ShapeDtypeStruct((n_rows, n_dim), F32),
        grid=(n_rows // tm, n_dim // tn, nk),
        in_specs=x_specs + [
            pl.BlockSpec((None, tk, tn), lambda i, j, k: (w_idx, k, j)),
            pl.BlockSpec((tm, tn), lambda i, j, k: (i, j)),
            pl.BlockSpec((1, N_MOD, tn), lambda i, j, k: (mod_row(i), 0, j)),
        ],
        out_specs=pl.BlockSpec((tm, tn), lambda i, j, k: (i, j)),
        scratch_shapes=[pltpu.VMEM((tm, tn), F32)] if nk > 1 else [],
        compiler_params=_cparams(("arbitrary", "arbitrary", "arbitrary")),
        name=name,
    )(*xs, w, res, mod)


NA_GROUP_ROWS = 4
NA_GROUP_Q = NA_GROUP_ROWS * GRID_W
NA_GROUP_KROWS = NA_GROUP_ROWS + NA_WIN_R
NA_GROUP_K = NA_GROUP_KROWS * GRID_W
NA_GROUPS = GRID_ROWS // NA_GROUP_ROWS
NA_PREP_ROWS = 256
NA_SCORE_LOOKAHEAD = 1


def _na_window_row0(rq):
    return min(max(rq - NA_WIN_R // 2, 0), GRID_ROWS - NA_WIN_R)


def _na_group_key_row0(r):
    return min(max(r - NA_WIN_R // 2, 0), GRID_ROWS - NA_GROUP_KROWS)


def _na_group_signature(r):
    k0 = _na_group_key_row0(r)
    return (k0 - r,) + tuple(_na_window_row0(r + j) - k0 for j in range(NA_GROUP_ROWS))


NA_GROUP_KINDS = tuple(sorted({_na_group_signature(r): r for r in reversed(range(0, GRID_ROWS, NA_GROUP_ROWS))}.values()))


def _na_group_kind(r):
    return [_na_group_signature(k) for k in NA_GROUP_KINDS].index(_na_group_signature(r))


def na_bias_table(rpb):
    w = GRID_W
    col = np.arange(w)
    c0 = np.clip(col - NA_WIN_C // 2, 0, w - NA_WIN_C)
    in_win = (col[None, :] >= c0[:, None]) & (col[None, :] < c0[:, None] + NA_WIN_C)
    padded = jnp.pad(rpb.astype(F32), ((0, 0), (0, 0), (w, w)))
    toep = jnp.stack([padded[:, :, NA_WIN_C - 1 - qc + w:NA_WIN_C - 1 - qc + 2 * w] for qc in range(w)], axis=2)
    toep = jnp.where(in_win[None, None], toep * LOG2E, NEG_BIG)
    masked = jnp.full((NA_HEADS, w, w), NEG_BIG, F32)
    kinds = []
    for r in NA_GROUP_KINDS:
        q_rows = []
        for rq in range(r, r + NA_GROUP_ROWS):
            r0 = _na_window_row0(rq)
            blocks = []
            for kr in range(_na_group_key_row0(r), _na_group_key_row0(r) + NA_GROUP_KROWS):
                blocks.append(toep[:, kr - rq + NA_WIN_R - 1] if r0 <= kr < r0 + NA_WIN_R else masked)
            q_rows.append(jnp.concatenate(blocks, axis=2))
        kinds.append(jnp.concatenate(q_rows, axis=1))
    return jnp.stack(kinds, axis=1)


def _na_body(q_ref, k_ref, v_ref, qc_ref, kc_ref, vc_ref, qg_ref, kg_ref, bias_ref,
             o_ref, oc_ref, qn_ref, knt_ref, vn_ref, kcnt_ref, vcn_ref):
    scale = HEAD_DIM ** -0.5 * LOG2E
    qg = qg_ref[...]
    kg = kg_ref[...]

    def with_ones(v):
        return jnp.concatenate([v.astype(BF16), jnp.ones(v.shape, BF16)], axis=1)

    def normalised(o_ext):
        return (o_ext[:, :HEAD_DIM] / o_ext[:, HEAD_DIM:]).astype(BF16)

    for i in range(SEQ // NA_PREP_ROWS):
        rows = slice(i * NA_PREP_ROWS, (i + 1) * NA_PREP_ROWS)
        qn_ref[rows, :] = (_rms_rows(q_ref[rows, :], qg) * scale).astype(BF16)
        knt_ref[:, rows] = _rms_rows(k_ref[rows, :], kg).T.astype(BF16)
        vn_ref[rows, :] = with_ones(v_ref[rows, :])
    kcnt = _rms_rows(kc_ref[...], kg).T.astype(BF16)
    vcn = with_ones(vc_ref[...])
    kcnt_ref[...] = kcnt
    vcn_ref[...] = vcn

    def group_rows(gi):
        k0 = _na_group_key_row0(gi * NA_GROUP_ROWS) * GRID_W
        return slice(gi * NA_GROUP_Q, (gi + 1) * NA_GROUP_Q), slice(k0, k0 + NA_GROUP_K)

    def scores(gi):
        qrows, krows = group_rows(gi)
        qb = qn_ref[qrows, :]
        return _dot(qb, knt_ref[:, krows]), _dot(qb, kcnt_ref[...])

    pending = [scores(gi) for gi in range(NA_SCORE_LOOKAHEAD)]
    for gi in range(NA_GROUPS):
        s1, s2 = pending.pop(0)
        if gi + NA_SCORE_LOOKAHEAD < NA_GROUPS:
            pending.append(scores(gi + NA_SCORE_LOOKAHEAD))
        qrows, krows = group_rows(gi)
        s1 = s1 + bias_ref[0, _na_group_kind(gi * NA_GROUP_ROWS)]
        m = jnp.maximum(jnp.max(s1, axis=-1, keepdims=True), jnp.max(s2, axis=-1, keepdims=True))
        p1 = jnp.exp2(s1 - m)
        p2 = jnp.exp2(s2 - m)
        o_ref[qrows, :] = normalised(_dot(p1.astype(BF16), vn_ref[krows, :]) + _dot(p2.astype(BF16), vcn_ref[...]))

    qcn = (_rms_rows(qc_ref[...], qg) * scale).astype(BF16)
    s = _dot(qcn, kcnt)
    m = jnp.max(s, axis=-1, keepdims=True)
    p = jnp.exp2(s - m)
    oc_ref[...] = normalised(_dot(p.astype(BF16), vcn))


def na_attention(p_lat, p_ctx, q_g, k_g, bias):
    hd = HEAD_DIM
    lat = lambda seg: pl.BlockSpec((SEQ, hd), lambda h, b: (b, seg * NA_HEADS + h))
    ctx = lambda seg: pl.BlockSpec((CTX_LEN, hd), lambda h, b: (b, seg * NA_HEADS + h))
    return pl.pallas_call(
        _na_body,
        out_shape=(jax.ShapeDtypeStruct((N_LAT, NA_WIDTH), BF16),
                   jax.ShapeDtypeStruct((N_CTX, NA_WIDTH), BF16)),
        grid=(NA_HEADS, BATCH),
        in_specs=[lat(0), lat(1), lat(2), ctx(0), ctx(1), ctx(2),
                  pl.BlockSpec((1, hd), lambda h, b: (0, 0)),
                  pl.BlockSpec((1, hd), lambda h, b: (0, 0)),
                  pl.BlockSpec((1, len(NA_GROUP_KINDS), NA_GROUP_Q, NA_GROUP_K), lambda h, b: (h, 0, 0, 0))],
        out_specs=(pl.BlockSpec((SEQ, hd), lambda h, b: (b, h)),
                   pl.BlockSpec((CTX_LEN, hd), lambda h, b: (b, h))),
        scratch_shapes=[pltpu.VMEM((SEQ, hd), BF16), pltpu.VMEM((hd, SEQ), BF16), pltpu.VMEM((SEQ, 2 * hd), BF16),
                        pltpu.VMEM((hd, CTX_LEN), BF16), pltpu.VMEM((CTX_LEN, 2 * hd), BF16)],
        compiler_params=_cparams(("arbitrary", "arbitrary")),
        name="na_attention",
    )(p_lat, p_lat, p_lat, p_ctx, p_ctx, p_ctx, q_g.reshape(1, hd), k_g.reshape(1, hd), bias)


HG_BLK = 128
HG_CPB = HG_BLK // HG_CHUNK
HG_CTX_BLKS = CTX_LEN // HG_BLK
HG_LAT_BLKS = SEQ // HG_BLK
HG_BLKS = HG_CTX_BLKS + HG_LAT_BLKS
HG_SCAN_STEPS = 3


def _hg_ref_rows(reverse):
    b, c = HG_BLK, HG_CHUNK
    if not reverse:
        return {c: [None] + [c * i - 1 for i in range(1, b // c)],
                32: [32 * j + 15 for j in range(b // 32)], 64: [64 * j + 31 for j in range(b // 64)],
                128: [63], "end": [b - 1]}
    return {c: [c * i + c for i in range(b // c - 1)] + [None],
            32: [32 * j + 16 for j in range(b // 32)], 64: [64 * j + 32 for j in range(b // 64)],
            128: [64], "end": [0]}


def _hg_level_codes(reverse):
    t = lax.broadcasted_iota(jnp.int32, (HG_BLK, HG_BLK), 0)
    s = lax.broadcasted_iota(jnp.int32, (HG_BLK, HG_BLK), 1)
    if reverse:
        t, s = s, t
    code = jnp.where((t >> 6) > (s >> 6), 4, 0)
    code = jnp.where(((t >> 6) == (s >> 6)) & ((t >> 5) > (s >> 5)), 3, code)
    code = jnp.where(((t >> 5) == (s >> 5)) & ((t >> 4) > (s >> 4)), 2, code)
    return jnp.where(((t >> 4) == (s >> 4)) & (s <= t), 1, code)


def _hg_body(q_ref, zf_ref, zb_ref, i_ref, g_ref, qc_ref, zfc_ref, zbc_ref, ic_ref, gc_ref,
             lbf_ref, lbb_ref, gn_ref, o_ref, oc_ref,
             qd_s, ka_s, q64_s, k64_s, q128_s, k128_s, qb_s, ke_s, gb_s, v_s, o_s, st_s, x_s, code_s):
    lbs = (lbf_ref[...], lbb_ref[...])
    ri = lax.broadcasted_iota(jnp.int32, (HG_BLK, HG_BLK), 0)
    ci = lax.broadcasted_iota(jnp.int32, (HG_BLK, HG_BLK), 1)
    tri = jnp.where(ci <= ri, 1.0, 0.0).astype(BF16)
    for d in range(2):
        code_s[d] = _hg_level_codes(d == 1)

    def prefix_rows(x):
        hi = x.astype(BF16)
        r1 = x - hi.astype(F32)
        mid = r1.astype(BF16)
        lo = (r1 - mid.astype(F32)).astype(BF16)
        y = _dot(tri, jnp.concatenate([hi, mid, lo], axis=1))
        return y[:, :HEAD_DIM] + y[:, HEAD_DIM:2 * HEAD_DIM] + y[:, 2 * HEAD_DIM:]

    def ref_rows(d, rows):
        group = HG_BLK // len(rows)
        parts = [jnp.zeros((group, HEAD_DIM), F32) if r is None else
                 jnp.broadcast_to(x_s[d, r:r + 1, :], (group, HEAD_DIM)) for r in rows]
        return parts[0] if len(parts) == 1 else jnp.concatenate(parts, axis=0)

    def prep_block(blk, q, zf, zb, v):
        qs = q * jax.nn.sigmoid(q)
        v_s[blk] = v.astype(BF16)
        for d, z in enumerate((zf, zb)):
            f = lbs[d] + (1.0 - lbs[d]) * jax.nn.sigmoid(z)
            logf = jnp.log2(f)
            k = 1.0 - f
            x = prefix_rows(logf)
            x_s[d] = x
            if d == 1:
                x = ref_rows(d, [HG_BLK - 1]) - x + logf
                x_s[d] = x
            rows = _hg_ref_rows(d == 1)
            r16, r32, r64, r128 = (ref_rows(d, rows[g]) for g in (HG_CHUNK, 32, 64, 128))
            x_end = ref_rows(d, rows["end"])
            e32 = jnp.exp2(-jnp.abs(x - r32))
            e64 = jnp.exp2(-jnp.abs(x - r64))
            e128 = jnp.exp2(-jnp.abs(x - r128))
            qd_s[d, blk] = (qs * jnp.exp2(x - r16)).astype(BF16)
            ka_s[d, blk] = jnp.concatenate([(k * jnp.exp2(r16 - x)).T, (k * e32).T], axis=1).astype(BF16)
            q64_s[d, blk] = (qs * e64).astype(BF16)
            k64_s[d, blk] = (k * e64).T.astype(BF16)
            q128_s[d, blk] = (qs * e128).astype(BF16)
            k128_s[d, blk] = (k * e128).T.astype(BF16)
            qb_s[d, blk] = (qs * jnp.exp2(x)).astype(BF16)
            ke_s[d, blk] = (k * jnp.exp2(x_end - x)).T.astype(BF16)
            gb_s[d, blk] = jnp.exp2(x_end).T

    for cb in range(HG_CTX_BLKS):
        rows = slice(cb * HG_BLK, (cb + 1) * HG_BLK)
        prep_block(cb, qc_ref[rows, :], zfc_ref[rows, :], zbc_ref[rows, :], ic_ref[rows, :])

    def prep_lat(n, _):
        rows = pl.ds(pl.multiple_of(n * HG_BLK, HG_BLK), HG_BLK)
        prep_block(HG_CTX_BLKS + n, q_ref[rows, :], zf_ref[rows, :], zb_ref[rows, :], i_ref[rows, :])
        return 0

    lax.fori_loop(0, HG_LAT_BLKS, prep_lat, 0, unroll=2)

    st_s[...] = jnp.zeros_like(st_s)

    def scan_steps(i, _):
        chains = []
        for n in [i * HG_SCAN_STEPS + j for j in range(HG_SCAN_STEPS)]:
            chains.append((0, n))
            chains.append((1, jnp.where(n < HG_CTX_BLKS, HG_CTX_BLKS - 1 - n, HG_BLKS + HG_CTX_BLKS - 1 - n)))
        ready = []
        for d, blk in chains:
            a1 = _dot(qd_s[d, blk], ka_s[d, blk])
            a64 = _dot(q64_s[d, blk], k64_s[d, blk])
            a128 = _dot(q128_s[d, blk], k128_s[d, blk])
            v = v_s[blk]
            inc = _dot(ke_s[d, blk], v)
            code = code_s[d]
            att = jnp.where(code == 1, a1[:, :HG_BLK],
                            jnp.where(code == 2, a1[:, HG_BLK:],
                                      jnp.where(code == 3, a64, jnp.where(code == 4, a128, 0.0))))
            ready.append((jnp.concatenate([att.astype(BF16), qb_s[d, blk]], axis=1), v, inc))
        for (d, blk), (lhs, v, inc) in zip(chains, ready):
            state = st_s[d]
            o_s[d, blk] = _dot(lhs, jnp.concatenate([v, state.astype(BF16)], axis=0))
            st_s[d] = gb_s[d, blk] * state + inc
        return 0

    lax.fori_loop(0, HG_BLKS // HG_SCAN_STEPS, scan_steps, 0)

    gn = gn_ref[...]

    def finish(blk, gate):
        o = o_s[0, blk] + o_s[1, blk]
        return (_rms_rows(o, gn) * (gate * jax.nn.sigmoid(gate))).astype(BF16)

    for cb in range(HG_CTX_BLKS):
        rows = slice(cb * HG_BLK, (cb + 1) * HG_BLK)
        oc_ref[rows, :] = finish(cb, gc_ref[rows, :])

    def fin_lat(n, _):
        rows = pl.ds(pl.multiple_of(n * HG_BLK, HG_BLK), HG_BLK)
        o_ref[rows, :] = finish(HG_CTX_BLKS + n, g_ref[rows, :])
        return 0

    lax.fori_loop(0, HG_LAT_BLKS, fin_lat, 0)


def hgrn2(p_lat, p_ctx, lb_fwd, lb_bwd, gn_g):
    hd = HEAD_DIM
    lat = lambda seg: pl.BlockSpec((SEQ, hd), lambda b, h: (b, seg * HG_HEADS + h))
    ctx = lambda seg: pl.BlockSpec((CTX_LEN, hd), lambda b, h: (b, seg * HG_HEADS + h))
    head_vec = pl.BlockSpec((1, hd), lambda b, h: (0, h))
    dir_bf = lambda width=hd: pltpu.VMEM((2, HG_BLKS, HG_BLK, width), BF16)
    dir_f32 = lambda: pltpu.VMEM((2, HG_BLKS, HG_BLK, hd), F32)
    return pl.pallas_call(
        _hg_body,
        out_shape=(jax.ShapeDtypeStruct((N_LAT, HG_WIDTH), BF16),
                   jax.ShapeDtypeStruct((N_CTX, HG_WIDTH), BF16)),
        grid=(BATCH, HG_HEADS),
        in_specs=[lat(3), lat(4), lat(5), lat(6), lat(7), ctx(3), ctx(4), ctx(5), ctx(6), ctx(7),
                  head_vec, head_vec, pl.BlockSpec((1, hd), lambda b, h: (0, 0))],
        out_specs=(pl.BlockSpec((SEQ, hd), lambda b, h: (b, h)),
                   pl.BlockSpec((CTX_LEN, hd), lambda b, h: (b, h))),
        scratch_shapes=[dir_bf(), dir_bf(2 * hd), dir_bf(), dir_bf(), dir_bf(), dir_bf(), dir_bf(), dir_bf(),
                        dir_f32(), pltpu.VMEM((HG_BLKS, HG_BLK, hd), BF16), dir_f32(),
                        pltpu.VMEM((2, hd, hd), F32), pltpu.VMEM((2, HG_BLK, hd), F32),
                        pltpu.VMEM((2, HG_BLK, HG_BLK), jnp.int32)],
        compiler_params=_cparams(("arbitrary", "arbitrary")),
        name="hgrn2",
    )(p_lat, p_lat, p_lat, p_lat, p_lat, p_ctx, p_ctx, p_ctx, p_ctx, p_ctx,
      lb_fwd.reshape(1, -1), lb_bwd.reshape(1, -1), gn_g.reshape(1, hd))


DF_TQ = 512
DF_NK = SEQ + CTX_LEN
DF_TK = 768
DF_KTILES = DF_NK // DF_TK
DF_SCORE_LOOKAHEAD = 2


def _rope(x, cos2, sin2):
    return x * cos2 + pltpu.roll(x, DIFF_DK // 2, axis=1) * sin2


def _diff_body(lam_ref, q_ref, k_ref, kc_ref, v_ref, vc_ref, cos_ref, sin_ref, cost_ref, sint_ref,
               qgt_ref, kg_ref, sg_ref, o_ref, qt_s, k0_s, k1_s, vt_s, e_s, *, out_scale):
    dk = DIFF_DK
    kg = kg_ref[...]

    @pl.when(pl.program_id(2) == 0)
    def _():
        def prep(i, _):
            rows = pl.ds(pl.multiple_of(i * CTX_LEN, CTX_LEN), CTX_LEN)
            c2 = cos_ref[rows, :]
            s2 = sin_ref[rows, :]
            kk = k_ref[rows, :]
            k0_s[rows, :] = _rope(_rms_rows(kk[:, :dk], kg), c2, s2).astype(BF16)
            k1_s[rows, :] = _rope(_rms_rows(kk[:, dk:], kg), c2, s2).astype(BF16)
            vt_s[i] = v_ref[rows, :].T.astype(BF16)
            qq_t = q_ref[rows, :].T
            ct = cost_ref[i]
            st = sint_ref[i]
            for j in range(2):
                xt = qq_t[j * dk:(j + 1) * dk, :]
                inv = lax.rsqrt(jnp.mean(xt * xt, axis=0, keepdims=True) + EPS)
                xn = xt * inv * qgt_ref[...]
                swapped = jnp.concatenate([xn[dk // 2:, :], xn[:dk // 2, :]], axis=0)
                qt_s[j, i] = (xn * ct + swapped * st).astype(BF16)
            return 0

        lax.fori_loop(0, SEQ // CTX_LEN, prep, 0)
        kc = kc_ref[...]
        k0_s[SEQ:, :] = _rms_rows(kc[:, :dk], kg).astype(BF16)
        k1_s[SEQ:, :] = _rms_rows(kc[:, dk:], kg).astype(BF16)
        vt_s[SEQ // CTX_LEN] = vc_ref[...].T.astype(BF16)

    lam = lam_ref[0, 0]
    qpt = DF_TQ // CTX_LEN
    q0 = pl.program_id(2) * qpt
    qt = [jnp.concatenate([qt_s[i, q0 + j] for j in range(qpt)], axis=1) for i in range(2)]
    k_s = (k0_s, k1_s)

    cpt = DF_TK // CTX_LEN
    jobs = [(t, i) for t in range(DF_KTILES) for i in range(2)]

    def scores(job):
        t, i = job
        return _dot(k_s[i][t * DF_TK:(t + 1) * DF_TK, :], qt[i])

    def attend(shifted):
        m = [None, None]
        l = [None, None]
        m_tile = [[None] * DF_KTILES for _ in range(2)]
        pending = [scores(job) for job in jobs[:DF_SCORE_LOOKAHEAD]]
        for n, (t, i) in enumerate(jobs):
            s = pending.pop(0)
            if n + DF_SCORE_LOOKAHEAD < len(jobs):
                pending.append(scores(jobs[n + DF_SCORE_LOOKAHEAD]))
            if shifted:
                tile_max = jnp.max(s, axis=0, keepdims=True)
                m_new = tile_max if t == 0 else jnp.maximum(m[i], tile_max)
                e = jnp.exp2(s - m_new)
            else:
                e = jnp.exp2(s)
            e_s[i, t] = e.astype(BF16)
            tile_sum = jnp.sum(e, axis=0, keepdims=True)
            if t == 0:
                l[i] = tile_sum
            elif shifted:
                l[i] = jnp.exp2(m[i] - m_new) * l[i] + tile_sum
            else:
                l[i] = l[i] + tile_sum
            if shifted:
                m[i] = m_new
                m_tile[i][t] = m_new

        weight = (1.0 / l[0], lam / l[1])
        acc = None
        for t in range(DF_KTILES):
            if shifted:
                f0, f1 = ((jnp.exp2(m_tile[i][t] - m[i]) * weight[i]).astype(BF16) for i in range(2))
            else:
                f0, f1 = (w.astype(BF16) for w in weight)
            a = e_s[0, t] * f0 - e_s[1, t] * f1
            vt = jnp.concatenate([vt_s[t * cpt + j] for j in range(cpt)], axis=1)
            d = _dot(vt, a)
            acc = d if acc is None else acc + d
        o_ref[...] = (_rms_rows(acc.T, sg_ref[...]) * out_scale).astype(o_ref.dtype)

    scores_bounded = lam_ref[0, 1] > 0.5
    pl.when(scores_bounded)(lambda: attend(False))
    pl.when(jnp.logical_not(scores_bounded))(lambda: attend(True))


def diff_attention(p_lat, p_ctx_kv, lam, cos2, sin2, q_g, k_g, subln_g, out_scale):
    nq = SEQ // DF_TQ
    w = 2 * DIFF_DK
    kcol0 = D_MODEL // w
    vcol0 = 2 * D_MODEL // w
    vec = lambda n: pl.BlockSpec((1, n), lambda b, h, i: (0, 0))
    n_chunks = SEQ // CTX_LEN
    chunked_t = lambda tab: tab.reshape(n_chunks, CTX_LEN, DIFF_DK).transpose(0, 2, 1)
    q_gain_t = jnp.broadcast_to((q_g.astype(F32) * (DIFF_DK ** -0.5 * LOG2E))[:, None], (DIFF_DK, CTX_LEN))
    score_bound = LOG2E * DIFF_DK ** 0.5 * jnp.max(jnp.abs(q_g)) * jnp.max(jnp.abs(k_g))
    scalars = jnp.stack([lam.reshape(()), (score_bound < SOFTMAX_NOSHIFT_LOG2).astype(F32)]).reshape(1, 2)
    return pl.pallas_call(
        functools.partial(_diff_body, out_scale=out_scale),
        out_shape=jax.ShapeDtypeStruct((N_LAT, D_MODEL), BF16),
        grid=(BATCH, DIFF_HEADS, nq),
        in_specs=[
            pl.BlockSpec(memory_space=pltpu.SMEM),
            pl.BlockSpec((SEQ, w), lambda b, h, i: (b, h)),
            pl.BlockSpec((SEQ, w), lambda b, h, i: (b, kcol0 + h)),
            pl.BlockSpec((CTX_LEN, w), lambda b, h, i: (b, h)),
            pl.BlockSpec((SEQ, w), lambda b, h, i: (b, vcol0 + h)),
            pl.BlockSpec((CTX_LEN, w), lambda b, h, i: (b, kcol0 + h)),
            pl.BlockSpec((SEQ, DIFF_DK), lambda b, h, i: (0, 0)),
            pl.BlockSpec((SEQ, DIFF_DK), lambda b, h, i: (0, 0)),
            pl.BlockSpec((n_chunks, DIFF_DK, CTX_LEN), lambda b, h, i: (0, 0, 0)),
            pl.BlockSpec((n_chunks, DIFF_DK, CTX_LEN), lambda b, h, i: (0, 0, 0)),
            pl.BlockSpec((DIFF_DK, CTX_LEN), lambda b, h, i: (0, 0)),
            vec(DIFF_DK), vec(DIFF_DV),
        ],
        out_specs=pl.BlockSpec((DF_TQ, w), lambda b, h, i: (b * nq + i, h)),
        scratch_shapes=[pltpu.VMEM((2, SEQ // CTX_LEN, DIFF_DK, CTX_LEN), BF16),
                        pltpu.VMEM((DF_NK, DIFF_DK), BF16), pltpu.VMEM((DF_NK, DIFF_DK), BF16),
                        pltpu.VMEM((DF_NK // CTX_LEN, DIFF_DV, CTX_LEN), BF16),
                        pltpu.VMEM((2, DF_KTILES, DF_TK, DF_TQ), BF16)],
        compiler_params=_cparams(("arbitrary", "arbitrary", "arbitrary")),
        name="diff_attention",
    )(scalars, p_lat, p_lat, p_ctx_kv, p_lat, p_ctx_kv, cos2, sin2, chunked_t(cos2), chunked_t(sin2), q_gain_t,
      k_g.reshape(1, DIFF_DK), subln_g.reshape(1, DIFF_DV))


def _rope_tables():
    t = jnp.arange(SEQ)
    row = (t // GRID_W).astype(F32)
    col = (t % GRID_W).astype(F32)
    n_freq = DIFF_DK // 4
    inv = ROPE_THETA ** (-jnp.arange(n_freq, dtype=F32) / n_freq)
    ang = jnp.concatenate([row[:, None] * inv, col[:, None] * inv], axis=-1)
    cos, sin = jnp.cos(ang), jnp.sin(ang)
    return jnp.concatenate([cos, cos], axis=-1), jnp.concatenate([-sin, sin], axis=-1)


def _mlp(x, mod, g, w1, w2, layer, is_ctx):
    a = norm_matmul(x, g, mod, w1, layer, shift_idx=3, is_ctx=is_ctx, out_dtype=BF16, relu2=True, name="mlp_up")
    return res_matmul([a], w2, layer, x, mod, gate_idx=5, is_ctx=is_ctx, name="mlp_down")


def kernel(x, c, ctx, c_ctx, ada_w, ada_b, norm_mix_g, norm_mlp_g, mlp_w1, mlp_w2, ev_w_in, ev_w_out, na_q_g, na_k_g, na_rpb, hg_lb_logits, hg_gnorm_g, od_w_in, od_w_out, df_q_g, df_k_g, df_lambda, df_subln_g):
    xl = x.reshape(N_LAT, D_MODEL)
    xc = ctx.reshape(N_CTX, D_MODEL)
    cond = jnp.concatenate([c, c_ctx[None], jnp.zeros((MOD_ROWS - BATCH - 1, D_MODEL), F32)], axis=0)
    mods = ada_table(cond, ada_w, ada_b)
    lb_all = jnp.cumsum(jax.nn.softmax(hg_lb_logits.astype(F32), axis=1), axis=1)
    cos2, sin2 = _rope_tables()
    w1, w2 = mlp_w1.astype(BF16), mlp_w2.astype(BF16)
    ev_in, ev_out = ev_w_in.astype(BF16), ev_w_out.astype(BF16)
    od_in, od_out = od_w_in.astype(BF16), od_w_out.astype(BF16)

    for l in range(DEPTH):
        need_ctx = l < DEPTH - 1
        mod = mods[l]
        g_mix = norm_mix_g[l]
        if l % 2 == 0:
            e = l // 2
            p_lat = norm_matmul(xl, g_mix, mod, ev_in, e, shift_idx=0, is_ctx=False, out_dtype=F32, name="even_in")
            p_ctx = norm_matmul(xc, g_mix, mod, ev_in, e, shift_idx=0, is_ctx=True, out_dtype=F32,
                                name="even_in_ctx")
            na_lat, na_ctx = na_attention(p_lat, p_ctx, na_q_g[e], na_k_g[e], na_bias_table(na_rpb[e]))
            hg_lat, hg_ctx = hgrn2(p_lat, p_ctx, lb_all[0, l], lb_all[1, l], hg_gnorm_g[e])
            xl = res_matmul([na_lat, hg_lat], ev_out, e, xl, mod, gate_idx=2, is_ctx=False, name="mix_out")
            if need_ctx:
                xc = res_matmul([na_ctx, hg_ctx], ev_out, e, xc, mod, gate_idx=2, is_ctx=True, name="mix_out_ctx")
        else:
            assert not need_ctx, "an odd layer followed by another layer needs context outputs"
            o = l // 2
            p_lat = norm_matmul(xl, g_mix, mod, od_in, o, shift_idx=0, is_ctx=False, out_dtype=F32, name="odd_in")
            p_ctx = norm_matmul(xc, g_mix, mod, od_in, o, shift_idx=0, is_ctx=True, out_dtype=F32,
                                col_block0=D_MODEL // MM_TILE, n_cols=2 * D_MODEL, name="odd_in_ctx")
            lam_init = 0.8 - 0.6 * math.exp(-0.3 * l)
            lp = df_lambda[o].astype(F32)
            lam = jnp.exp(jnp.sum(lp[0] * lp[1])) - jnp.exp(jnp.sum(lp[2] * lp[3])) + lam_init
            om = diff_attention(p_lat, p_ctx, lam, cos2, sin2, df_q_g[o], df_k_g[o], df_subln_g[o],
                                1.0 - lam_init)
            xl = res_matmul([om], od_out, o, xl, mod, gate_idx=2, is_ctx=False, name="mix_out")
        xl = _mlp(xl, mod, norm_mlp_g[l], w1, w2, l, False)
        if need_ctx:
            xc = _mlp(xc, mod, norm_mlp_g[l], w1, w2, l, True)
    return xl.reshape(BATCH, SEQ, D_MODEL)
```

```python
import functools
import math

import jax
import jax.numpy as jnp
import numpy as np
from jax import lax
from jax.experimental import pallas as pl
from jax.experimental.pallas import tpu as pltpu

F32 = jnp.float32
BF16 = jnp.bfloat16

D_MODEL = 2048
BATCH = 4
SEQ = 2048
DEPTH = 2
GRID_W = 64
GRID_ROWS = SEQ // GRID_W
CTX_LEN = 256
HEAD_DIM = 128
NA_HEADS = 8
NA_WIN_R = 8
NA_WIN_C = 16
NA_WIDTH = NA_HEADS * HEAD_DIM
HG_HEADS = 8
HG_CHUNK = 16
HG_WIDTH = HG_HEADS * HEAD_DIM
EVEN_IN = 8 * NA_WIDTH
DIFF_HEADS = 8
DIFF_DK = HEAD_DIM
DIFF_DV = 2 * HEAD_DIM
ODD_IN = 3 * D_MODEL
D_FF = 4 * D_MODEL
N_MOD = 6
ROPE_THETA = 10000.0
EPS = 1e-6

N_LAT = BATCH * SEQ
N_CTX = BATCH * CTX_LEN
MOD_ROWS = 8
CTX_MOD_ROW = BATCH
NEG_BIG = -1e30
LOG2E = math.log2(math.e)
SOFTMAX_NOSHIFT_LOG2 = 100.0

VMEM_LIMIT_V7X = 56 * 1024 * 1024
MM_TILE = 1024
MM_TK = 2048


def _cparams(sem, vmem=VMEM_LIMIT_V7X):
    return pltpu.CompilerParams(dimension_semantics=sem, vmem_limit_bytes=vmem)


def _dot(a, b):
    return jnp.dot(a, b, preferred_element_type=F32)


def _dot_nt(a, b):
    return lax.dot_general(a, b, (((1,), (1,)), ((), ())), preferred_element_type=F32)


def _rms_rows(x, g):
    return x * lax.rsqrt(jnp.mean(x * x, axis=-1, keepdims=True) + EPS) * g


def _mod_row_of_tile(is_ctx, tile_rows):
    if is_ctx:
        return lambda i: CTX_MOD_ROW
    return lambda i: (i * tile_rows) // SEQ


def _ada_body(c_ref, w_ref, b_ref, o_ref):
    c = c_ref[...]
    s = (c * jax.nn.sigmoid(c)).astype(BF16)
    o_ref[0] = _dot(s, w_ref[0].astype(BF16)) + b_ref[0]


def ada_table(cond, ada_w, ada_b):
    tn = 1024
    n = N_MOD * D_MODEL
    out = pl.pallas_call(
        _ada_body,
        out_shape=jax.ShapeDtypeStruct((DEPTH, MOD_ROWS, n), F32),
        grid=(DEPTH, n // tn),
        in_specs=[
            pl.BlockSpec((MOD_ROWS, D_MODEL), lambda l, j: (0, 0)),
            pl.BlockSpec((1, D_MODEL, tn), lambda l, j: (l, 0, j)),
            pl.BlockSpec((1, 1, tn), lambda l, j: (l, 0, j)),
        ],
        out_specs=pl.BlockSpec((1, MOD_ROWS, tn), lambda l, j: (l, 0, j)),
        compiler_params=_cparams(("arbitrary", "arbitrary")),
        name="ada_table",
    )(cond, ada_w, ada_b.reshape(DEPTH, 1, n))
    return out.reshape(DEPTH, MOD_ROWS, N_MOD, D_MODEL)


def _norm_mm_body(x_ref, g_ref, mod_ref, w_ref, *out_and_scratch, shift_idx, relu2, cast_w):
    o_ref, h_ref = out_and_scratch[0], out_and_scratch[-1]

    @pl.when(pl.program_id(1) == 0)
    def _():
        shift = mod_ref[0, shift_idx:shift_idx + 1, :]
        scale = mod_ref[0, shift_idx + 1:shift_idx + 2, :]
        h_ref[...] = (_rms_rows(x_ref[...], g_ref[...]) * (1.0 + scale) + shift).astype(BF16)

    w = w_ref[...]
    if cast_w:
        w = w.astype(BF16)
        out_and_scratch[1][...] = w
    acc = _dot(h_ref[...], w)
    if relu2:
        acc = jnp.maximum(acc, 0.0)
        acc = acc * acc
    o_ref[...] = acc.astype(o_ref.dtype)


def norm_matmul(x, g, mod, w, w_idx, *, shift_idx, is_ctx, out_dtype, relu2=False, col_block0=0, n_cols=None,
                cast_weights=False, name):
    n_rows = x.shape[0]
    tm = MM_TILE
    if cast_weights:
        assert n_rows == tm and col_block0 == 0 and n_cols is None
        tn = MM_TILE // 2
    else:
        tn = MM_TILE * (jnp.dtype(F32).itemsize // jnp.dtype(out_dtype).itemsize)
    col_block0 = col_block0 * MM_TILE // tn
    n_cols = w.shape[2] if n_cols is None else n_cols
    mod_row = _mod_row_of_tile(is_ctx, tm)
    out_shape = jax.ShapeDtypeStruct((n_rows, n_cols), out_dtype)
    out_specs = pl.BlockSpec((tm, tn), lambda i, j: (i, j))
    if cast_weights:
        out_shape = (out_shape, jax.ShapeDtypeStruct((1, D_MODEL, n_cols), BF16))
        out_specs = (out_specs, pl.BlockSpec((None, D_MODEL, tn), lambda i, j: (0, 0, j)))
    return pl.pallas_call(
        functools.partial(_norm_mm_body, shift_idx=shift_idx, relu2=relu2, cast_w=cast_weights),
        out_shape=out_shape,
        grid=(n_rows // tm, n_cols // tn),
        in_specs=[
            pl.BlockSpec((tm, D_MODEL), lambda i, j: (i, 0)),
            pl.BlockSpec((1, D_MODEL), lambda i, j: (0, 0)),
            pl.BlockSpec((1, N_MOD, D_MODEL), lambda i, j: (mod_row(i), 0, 0)),
            pl.BlockSpec((None, D_MODEL, tn), lambda i, j: (w_idx, 0, col_block0 + j)),
        ],
        out_specs=out_specs,
        scratch_shapes=[pltpu.VMEM((tm, D_MODEL), BF16)],
        compiler_params=_cparams(("arbitrary", "arbitrary")),
        name=name,
    )(x, g.reshape(1, D_MODEL), mod, w)


def _res_mm_body(*refs, n_x, nk, gate_idx, cast_w):
    x_refs = refs[:n_x]
    w_ref, res_ref, mod_ref, o_ref = refs[n_x:n_x + 4]
    scratch = refs[n_x + 4:]
    if cast_w:
        wb_ref, scratch = scratch[0], scratch[1:]

    def partial_product():
        acc = None
        k0 = 0
        for x_ref in x_refs:
            kw = x_ref.shape[1]
            w = w_ref[k0:k0 + kw, :]
            if cast_w:
                w = w.astype(BF16)
                wb_ref[k0:k0 + kw, :] = w
            d = _dot(x_ref[...], w)
            acc = d if acc is None else acc + d
            k0 += kw
        return acc

    def finish(acc):
        gate = mod_ref[0, gate_idx:gate_idx + 1, :]
        o_ref[...] = res_ref[...] + gate * acc

    if nk == 1:
        finish(partial_product())
    else:
        acc_ref = scratch[0]
        k = pl.program_id(2)

        @pl.when(k == 0)
        def _():
            acc_ref[...] = jnp.zeros_like(acc_ref)

        acc_ref[...] += partial_product()

        @pl.when(k == nk - 1)
        def _():
            finish(acc_ref[...])


def res_matmul(xs, w, w_idx, res, mod, *, gate_idx, is_ctx, cast_weights=False, name):
    n_rows = res.shape[0]
    _, k_dim, n_dim = w.shape
    tk = MM_TK if len(xs) == 1 else k_dim
    nk = k_dim // tk
    if cast_weights:
        assert n_rows == MM_TILE
        tm, tn = MM_TILE, MM_TILE // 2
    elif nk == 1:
        tm, tn = MM_TILE // 2, n_dim
    else:
        tm = tn = MM_TILE
    if len(xs) == 1:
        x_specs = [pl.BlockSpec((tm, tk), lambda i, j, k: (i, k))]
    else:
        x_specs = [pl.BlockSpec((tm, x.shape[1]), lambda i, j, k: (i, 0)) for x in xs]
    mod_row = _mod_row_of_tile(is_ctx, tm)
    out_shape = jax.ShapeDtypeStruct((n_rows, n_dim), F32)
    out_specs = pl.BlockSpec((tm, tn), lambda i, j, k: (i, j))
    if cast_weights:
        out_shape = (out_shape, jax.ShapeDtypeStruct((1, k_dim, n_dim), BF16))
        out_specs = (out_specs, pl.BlockSpec((None, tk, tn), lambda i, j, k: (0, k, j)))
    return pl.pallas_call(
        functools.partial(_res_mm_body, n_x=len(xs), nk=nk, gate_idx=gate_idx, cast_w=cast_weights),
        out_shape=out_shape,
        grid=(n_rows // tm, n_dim // tn, nk),
        in_specs=x_specs + [
            pl.BlockSpec((None, tk, tn), lambda i, j, k: (w_idx, k, j)),
            pl.BlockSpec((tm, tn), lambda i, j, k: (i, j)),
            pl.BlockSpec((1, N_MOD, tn), lambda i, j, k: (mod_row(i), 0, j)),
        ],
        out_specs=out_specs,
        scratch_shapes=[pltpu.VMEM((tm, tn), F32)] if nk > 1 else [],
        compiler_params=_cparams(("arbitrary", "arbitrary", "arbitrary")),
        name=name,
    )(*xs, w, res, mod)


NA_GROUP_ROWS = 4
NA_GROUP_Q = NA_GROUP_ROWS * GRID_W
NA_GROUP_KROWS = NA_GROUP_ROWS + NA_WIN_R
NA_GROUP_K = NA_GROUP_KROWS * GRID_W
NA_GROUPS = GRID_ROWS // NA_GROUP_ROWS
NA_PREP_ROWS = 256
NA_SCORE_LOOKAHEAD = 1


def _na_window_row0(rq):
    return min(max(rq - NA_WIN_R // 2, 0), GRID_ROWS - NA_WIN_R)


def _na_group_key_row0(r):
    return min(max(r - NA_WIN_R // 2, 0), GRID_ROWS - NA_GROUP_KROWS)


def _na_group_signature(r):
    k0 = _na_group_key_row0(r)
    return (k0 - r,) + tuple(_na_window_row0(r + j) - k0 for j in range(NA_GROUP_ROWS))


NA_GROUP_KINDS = tuple(sorted({_na_group_signature(r): r for r in reversed(range(0, GRID_ROWS, NA_GROUP_ROWS))}.values()))


def _na_group_kind(r):
    return [_na_group_signature(k) for k in NA_GROUP_KINDS].index(_na_group_signature(r))


def na_bias_table(rpb):
    w = GRID_W
    col = np.arange(w)
    c0 = np.clip(col - NA_WIN_C // 2, 0, w - NA_WIN_C)
    in_win = (col[None, :] >= c0[:, None]) & (col[None, :] < c0[:, None] + NA_WIN_C)
    padded = jnp.pad(rpb.astype(F32), ((0, 0), (0, 0), (w, w)))
    toep = jnp.stack([padded[:, :, NA_WIN_C - 1 - qc + w:NA_WIN_C - 1 - qc + 2 * w] for qc in range(w)], axis=2)
    toep = jnp.where(in_win[None, None], toep * LOG2E, NEG_BIG)
    masked = jnp.full((NA_HEADS, w, w), NEG_BIG, F32)
    kinds = []
    for r in NA_GROUP_KINDS:
        q_rows = []
        for rq in range(r, r + NA_GROUP_ROWS):
            r0 = _na_window_row0(rq)
            blocks = []
            for kr in range(_na_group_key_row0(r), _na_group_key_row0(r) + NA_GROUP_KROWS):
                blocks.append(toep[:, kr - rq + NA_WIN_R - 1] if r0 <= kr < r0 + NA_WIN_R else masked)
            q_rows.append(jnp.concatenate(blocks, axis=2))
        kinds.append(jnp.concatenate(q_rows, axis=1))
    return jnp.stack(kinds, axis=1)


def _na_body(q_ref, k_ref, v_ref, qc_ref, kc_ref, vc_ref, qg_ref, kg_ref, bias_ref,
             o_ref, oc_ref, qn_ref, knt_ref, vn_ref, kcnt_ref, vcn_ref):
    scale = HEAD_DIM ** -0.5 * LOG2E
    qg = qg_ref[...]
    kg = kg_ref[...]

    def with_ones(v):
        return jnp.concatenate([v.astype(BF16), jnp.ones(v.shape, BF16)], axis=1)

    def normalised(o_ext):
        return (o_ext[:, :HEAD_DIM] / o_ext[:, HEAD_DIM:]).astype(BF16)

    for i in range(SEQ // NA_PREP_ROWS):
        rows = slice(i * NA_PREP_ROWS, (i + 1) * NA_PREP_ROWS)
        qn_ref[rows, :] = (_rms_rows(q_ref[rows, :], qg) * scale).astype(BF16)
        knt_ref[:, rows] = _rms_rows(k_ref[rows, :], kg).T.astype(BF16)
        vn_ref[rows, :] = with_ones(v_ref[rows, :])
    kcnt = _rms_rows(kc_ref[...], kg).T.astype(BF16)
    vcn = with_ones(vc_ref[...])
    kcnt_ref[...] = kcnt
    vcn_ref[...] = vcn

    def group_rows(gi):
        k0 = _na_group_key_row0(gi * NA_GROUP_ROWS) * GRID_W
        return slice(gi * NA_GROUP_Q, (gi + 1) * NA_GROUP_Q), slice(k0, k0 + NA_GROUP_K)

    def scores(gi):
        qrows, krows = group_rows(gi)
        qb = qn_ref[qrows, :]
        return _dot(qb, knt_ref[:, krows]), _dot(qb, kcnt_ref[...])

    pending = [scores(gi) for gi in range(NA_SCORE_LOOKAHEAD)]
    for gi in range(NA_GROUPS):
        s1, s2 = pending.pop(0)
        if gi + NA_SCORE_LOOKAHEAD < NA_GROUPS:
            pending.append(scores(gi + NA_SCORE_LOOKAHEAD))
        qrows, krows = group_rows(gi)
        s1 = s1 + bias_ref[0, _na_group_kind(gi * NA_GROUP_ROWS)]
        m = jnp.maximum(jnp.max(s1, axis=-1, keepdims=True), jnp.max(s2, axis=-1, keepdims=True))
        p1 = jnp.exp2(s1 - m)
        p2 = jnp.exp2(s2 - m)
        o_ref[qrows, :] = normalised(_dot(p1.astype(BF16), vn_ref[krows, :]) + _dot(p2.astype(BF16), vcn_ref[...]))

    qcn = (_rms_rows(qc_ref[...], qg) * scale).astype(BF16)
    s = _dot(qcn, kcnt)
    m = jnp.max(s, axis=-1, keepdims=True)
    p = jnp.exp2(s - m)
    oc_ref[...] = normalised(_dot(p.astype(BF16), vcn))


def na_attention(p_lat, p_ctx, q_g, k_g, bias):
    hd = HEAD_DIM
    lat = lambda seg: pl.BlockSpec((SEQ, hd), lambda h, b: (b, seg * NA_HEADS + h))
    ctx = lambda seg: pl.BlockSpec((CTX_LEN, hd), lambda h, b: (b, seg * NA_HEADS + h))
    return pl.pallas_call(
        _na_body,
        out_shape=(jax.ShapeDtypeStruct((N_LAT, NA_WIDTH), BF16),
                   jax.ShapeDtypeStruct((N_CTX, NA_WIDTH), BF16)),
        grid=(NA_HEADS, BATCH),
        in_specs=[lat(0), lat(1), lat(2), ctx(0), ctx(1), ctx(2),
                  pl.BlockSpec((1, hd), lambda h, b: (0, 0)),
                  pl.BlockSpec((1, hd), lambda h, b: (0, 0)),
                  pl.BlockSpec((1, len(NA_GROUP_KINDS), NA_GROUP_Q, NA_GROUP_K), lambda h, b: (h, 0, 0, 0))],
        out_specs=(pl.BlockSpec((SEQ, hd), lambda h, b: (b, h)),
                   pl.BlockSpec((CTX_LEN, hd), lambda h, b: (b, h))),
        scratch_shapes=[pltpu.VMEM((SEQ, hd), BF16), pltpu.VMEM((hd, SEQ), BF16), pltpu.VMEM((SEQ, 2 * hd), BF16),
                        pltpu.VMEM((hd, CTX_LEN), BF16), pltpu.VMEM((CTX_LEN, 2 * hd), BF16)],
        compiler_params=_cparams(("arbitrary", "arbitrary")),
        name="na_attention",
    )(p_lat, p_lat, p_lat, p_ctx, p_ctx, p_ctx, q_g.reshape(1, hd), k_g.reshape(1, hd), bias)


HG_BLK = 128
HG_CPB = HG_BLK // HG_CHUNK
HG_CTX_BLKS = CTX_LEN // HG_BLK
HG_LAT_BLKS = SEQ // HG_BLK
HG_BLKS = HG_CTX_BLKS + HG_LAT_BLKS
HG_SCAN_STEPS = 3


def _hg_ref_rows(reverse):
    b, c = HG_BLK, HG_CHUNK
    if not reverse:
        return {c: [None] + [c * i - 1 for i in range(1, b // c)],
                32: [32 * j + 15 for j in range(b // 32)], 64: [64 * j + 31 for j in range(b // 64)],
                128: [63], "end": [b - 1]}
    return {c: [c * i + c for i in range(b // c - 1)] + [None],
            32: [32 * j + 16 for j in range(b // 32)], 64: [64 * j + 32 for j in range(b // 64)],
            128: [64], "end": [0]}


def _hg_level_codes(reverse):
    t = lax.broadcasted_iota(jnp.int32, (HG_BLK, HG_BLK), 0)
    s = lax.broadcasted_iota(jnp.int32, (HG_BLK, HG_BLK), 1)
    if reverse:
        t, s = s, t
    code = jnp.where((t >> 6) > (s >> 6), 4, 0)
    code = jnp.where(((t >> 6) == (s >> 6)) & ((t >> 5) > (s >> 5)), 3, code)
    code = jnp.where(((t >> 5) == (s >> 5)) & ((t >> 4) > (s >> 4)), 2, code)
    return jnp.where(((t >> 4) == (s >> 4)) & (s <= t), 1, code)


def _hg_body(q_ref, zf_ref, zb_ref, i_ref, g_ref, qc_ref, zfc_ref, zbc_ref, ic_ref, gc_ref,
             lbf_ref, lbb_ref, gn_ref, o_ref, oc_ref,
             qd_s, ka_s, q64_s, k64_s, q128_s, k128_s, qb_s, ke_s, gb_s, v_s, o_s, st_s, x_s, code_s):
    lbs = (lbf_ref[...], lbb_ref[...])
    ri = lax.broadcasted_iota(jnp.int32, (HG_BLK, HG_BLK), 0)
    ci = lax.broadcasted_iota(jnp.int32, (HG_BLK, HG_BLK), 1)
    tri = jnp.where(ci <= ri, 1.0, 0.0).astype(BF16)
    for d in range(2):
        code_s[d] = _hg_level_codes(d == 1)

    def prefix_rows(x):
        hi = x.astype(BF16)
        r1 = x - hi.astype(F32)
        mid = r1.astype(BF16)
        lo = (r1 - mid.astype(F32)).astype(BF16)
        y = _dot(tri, jnp.concatenate([hi, mid, lo], axis=1))
        return y[:, :HEAD_DIM] + y[:, HEAD_DIM:2 * HEAD_DIM] + y[:, 2 * HEAD_DIM:]

    def ref_rows(d, rows):
        group = HG_BLK // len(rows)
        parts = [jnp.zeros((group, HEAD_DIM), F32) if r is None else
                 jnp.broadcast_to(x_s[d, r:r + 1, :], (group, HEAD_DIM)) for r in rows]
        return parts[0] if len(parts) == 1 else jnp.concatenate(parts, axis=0)

    def prep_block(blk, q, zf, zb, v):
        qs = q * jax.nn.sigmoid(q)
        v_s[blk] = v.astype(BF16)
        for d, z in enumerate((zf, zb)):
            f = lbs[d] + (1.0 - lbs[d]) * jax.nn.sigmoid(z)
            logf = jnp.log2(f)
            k = 1.0 - f
            x = prefix_rows(logf)
            x_s[d] = x
            if d == 1:
                x = ref_rows(d, [HG_BLK - 1]) - x + logf
                x_s[d] = x
            rows = _hg_ref_rows(d == 1)
            r16, r32, r64, r128 = (ref_rows(d, rows[g]) for g in (HG_CHUNK, 32, 64, 128))
            x_end = ref_rows(d, rows["end"])
            e32 = jnp.exp2(-jnp.abs(x - r32))
            e64 = jnp.exp2(-jnp.abs(x - r64))
            e128 = jnp.exp2(-jnp.abs(x - r128))
            qd_s[d, blk] = (qs * jnp.exp2(x - r16)).astype(BF16)
            ka_s[d, blk] = jnp.concatenate([(k * jnp.exp2(r16 - x)).T, (k * e32).T], axis=1).astype(BF16)
            q64_s[d, blk] = (qs * e64).astype(BF16)
            k64_s[d, blk] = (k * e64).T.astype(BF16)
            q128_s[d, blk] = (qs * e128).astype(BF16)
            k128_s[d, blk] = (k * e128).T.astype(BF16)
            qb_s[d, blk] = (qs * jnp.exp2(x)).astype(BF16)
            ke_s[d, blk] = (k * jnp.exp2(x_end - x)).T.astype(BF16)
            gb_s[d, blk] = jnp.exp2(x_end).T

    for cb in range(HG_CTX_BLKS):
        rows = slice(cb * HG_BLK, (cb + 1) * HG_BLK)
        prep_block(cb, qc_ref[rows, :], zfc_ref[rows, :], zbc_ref[rows, :], ic_ref[rows, :])

    def prep_lat(n, _):
        rows = pl.ds(pl.multiple_of(n * HG_BLK, HG_BLK), HG_BLK)
        prep_block(HG_CTX_BLKS + n, q_ref[rows, :], zf_ref[rows, :], zb_ref[rows, :], i_ref[rows, :])
        return 0

    lax.fori_loop(0, HG_LAT_BLKS, prep_lat, 0, unroll=2)

    st_s[...] = jnp.zeros_like(st_s)

    def scan_steps(i, _):
        chains = []
        for n in [i * HG_SCAN_STEPS + j for j in range(HG_SCAN_STEPS)]:
            chains.append((0, n))
            chains.append((1, jnp.where(n < HG_CTX_BLKS, HG_CTX_BLKS - 1 - n, HG_BLKS + HG_CTX_BLKS - 1 - n)))
        ready = []
        for d, blk in chains:
            a1 = _dot(qd_s[d, blk], ka_s[d, blk])
            a64 = _dot(q64_s[d, blk], k64_s[d, blk])
            a128 = _dot(q128_s[d, blk], k128_s[d, blk])
            v = v_s[blk]
            inc = _dot(ke_s[d, blk], v)
            code = code_s[d]
            att = jnp.where(code == 1, a1[:, :HG_BLK],
                            jnp.where(code == 2, a1[:, HG_BLK:],
                                      jnp.where(code == 3, a64, jnp.where(code == 4, a128, 0.0))))
            ready.append((jnp.concatenate([att.astype(BF16), qb_s[d, blk]], axis=1), v, inc))
        for (d, blk), (lhs, v, inc) in zip(chains, ready):
            state = st_s[d]
            o_s[d, blk] = _dot(lhs, jnp.concatenate([v, state.astype(BF16)], axis=0))
            st_s[d] = gb_s[d, blk] * state + inc
        return 0

    lax.fori_loop(0, HG_BLKS // HG_SCAN_STEPS, scan_steps, 0)

    gn = gn_ref[...]

    def finish(blk, gate):
        o = o_s[0, blk] + o_s[1, blk]
        return (_rms_rows(o, gn) * (gate * jax.nn.sigmoid(gate))).astype(BF16)

    for cb in range(HG_CTX_BLKS):
        rows = slice(cb * HG_BLK, (cb + 1) * HG_BLK)
        oc_ref[rows, :] = finish(cb, gc_ref[rows, :])

    def fin_lat(n, _):
        rows = pl.ds(pl.multiple_of(n * HG_BLK, HG_BLK), HG_BLK)
        o_ref[rows, :] = finish(HG_CTX_BLKS + n, g_ref[rows, :])
        return 0

    lax.fori_loop(0, HG_LAT_BLKS, fin_lat, 0)


def hgrn2(p_lat, p_ctx, lb_fwd, lb_bwd, gn_g):
    hd = HEAD_DIM
    lat = lambda seg: pl.BlockSpec((SEQ, hd), lambda b, h: (b, seg * HG_HEADS + h))
    ctx = lambda seg: pl.BlockSpec((CTX_LEN, hd), lambda b, h: (b, seg * HG_HEADS + h))
    head_vec = pl.BlockSpec((1, hd), lambda b, h: (0, h))
    dir_bf = lambda width=hd: pltpu.VMEM((2, HG_BLKS, HG_BLK, width), BF16)
    dir_f32 = lambda: pltpu.VMEM((2, HG_BLKS, HG_BLK, hd), F32)
    return pl.pallas_call(
        _hg_body,
        out_shape=(jax.ShapeDtypeStruct((N_LAT, HG_WIDTH), BF16),
                   jax.ShapeDtypeStruct((N_CTX, HG_WIDTH), BF16)),
        grid=(BATCH, HG_HEADS),
        in_specs=[lat(3), lat(4), lat(5), lat(6), lat(7), ctx(3), ctx(4), ctx(5), ctx(6), ctx(7),
                  head_vec, head_vec, pl.BlockSpec((1, hd), lambda b, h: (0, 0))],
        out_specs=(pl.BlockSpec((SEQ, hd), lambda b, h: (b, h)),
                   pl.BlockSpec((CTX_LEN, hd), lambda b, h: (b, h))),
        scratch_shapes=[dir_bf(), dir_bf(2 * hd), dir_bf(), dir_bf(), dir_bf(), dir_bf(), dir_bf(), dir_bf(),
                        dir_f32(), pltpu.VMEM((HG_BLKS, HG_BLK, hd), BF16), dir_f32(),
                        pltpu.VMEM((2, hd, hd), F32), pltpu.VMEM((2, HG_BLK, hd), F32),
                        pltpu.VMEM((2, HG_BLK, HG_BLK), jnp.int32)],
        compiler_params=_cparams(("arbitrary", "arbitrary")),
        name="hgrn2",
    )(p_lat, p_lat, p_lat, p_lat, p_lat, p_ctx, p_ctx, p_ctx, p_ctx, p_ctx,
      lb_fwd.reshape(1, -1), lb_bwd.reshape(1, -1), gn_g.reshape(1, hd))


DF_TQ = 512
DF_NK = SEQ + CTX_LEN
DF_TK = 768
DF_KTILES = DF_NK // DF_TK
DF_SCORE_LOOKAHEAD = 2


def _rope(x, cos2, sin2):
    return x * cos2 + pltpu.roll(x, DIFF_DK // 2, axis=1) * sin2


def _diff_body(lam_ref, q_ref, k_ref, kc_ref, v_ref, vc_ref, cos_ref, sin_ref, cost_ref, sint_ref,
               qgt_ref, kg_ref, sg_ref, o_ref, qt_s, k0_s, k1_s, vt_s, e_s, *, out_scale):
    dk = DIFF_DK
    kg = kg_ref[...]

    @pl.when(pl.program_id(2) == 0)
    def _():
        def prep(i, _):
            rows = pl.ds(pl.multiple_of(i * CTX_LEN, CTX_LEN), CTX_LEN)
            c2 = cos_ref[rows, :]
            s2 = sin_ref[rows, :]
            kk = k_ref[rows, :]
            k0_s[rows, :] = _rope(_rms_rows(kk[:, :dk], kg), c2, s2).astype(BF16)
            k1_s[rows, :] = _rope(_rms_rows(kk[:, dk:], kg), c2, s2).astype(BF16)
            vt_s[i] = v_ref[rows, :].T.astype(BF16)
            qq_t = q_ref[rows, :].T
            ct = cost_ref[i]
            st = sint_ref[i]
            for j in range(2):
                xt = qq_t[j * dk:(j + 1) * dk, :]
                inv = lax.rsqrt(jnp.mean(xt * xt, axis=0, keepdims=True) + EPS)
                xn = xt * inv * qgt_ref[...]
                swapped = jnp.concatenate([xn[dk // 2:, :], xn[:dk // 2, :]], axis=0)
                qt_s[j, i] = (xn * ct + swapped * st).astype(BF16)
            return 0

        lax.fori_loop(0, SEQ // CTX_LEN, prep, 0)
        kc = kc_ref[...]
        k0_s[SEQ:, :] = _rms_rows(kc[:, :dk], kg).astype(BF16)
        k1_s[SEQ:, :] = _rms_rows(kc[:, dk:], kg).astype(BF16)
        vt_s[SEQ // CTX_LEN] = vc_ref[...].T.astype(BF16)

    lam = lam_ref[0, 0]
    qpt = DF_TQ // CTX_LEN
    q0 = pl.program_id(2) * qpt
    qt = [jnp.concatenate([qt_s[i, q0 + j] for j in range(qpt)], axis=1) for i in range(2)]
    k_s = (k0_s, k1_s)

    cpt = DF_TK // CTX_LEN
    jobs = [(t, i) for t in range(DF_KTILES) for i in range(2)]

    def scores(job):
        t, i = job
        return _dot(k_s[i][t * DF_TK:(t + 1) * DF_TK, :], qt[i])

    def attend(shifted):
        m = [None, None]
        l = [None, None]
        m_tile = [[None] * DF_KTILES for _ in range(2)]
        pending = [scores(job) for job in jobs[:DF_SCORE_LOOKAHEAD]]
        for n, (t, i) in enumerate(jobs):
            s = pending.pop(0)
            if n + DF_SCORE_LOOKAHEAD < len(jobs):
                pending.append(scores(jobs[n + DF_SCORE_LOOKAHEAD]))
            if shifted:
                tile_max = jnp.max(s, axis=0, keepdims=True)
                m_new = tile_max if t == 0 else jnp.maximum(m[i], tile_max)
                e = jnp.exp2(s - m_new)
            else:
                e = jnp.exp2(s)
            e_s[i, t] = e.astype(BF16)
            tile_sum = jnp.sum(e, axis=0, keepdims=True)
            if t == 0:
                l[i] = tile_sum
            elif shifted:
                l[i] = jnp.exp2(m[i] - m_new) * l[i] + tile_sum
            else:
                l[i] = l[i] + tile_sum
            if shifted:
                m[i] = m_new
                m_tile[i][t] = m_new

        weight = (1.0 / l[0], lam / l[1])
        acc = None
        for t in range(DF_KTILES):
            if shifted:
                f0, f1 = ((jnp.exp2(m_tile[i][t] - m[i]) * weight[i]).astype(BF16) for i in range(2))
            else:
                f0, f1 = (w.astype(BF16) for w in weight)
            a = e_s[0, t] * f0 - e_s[1, t] * f1
            vt = jnp.concatenate([vt_s[t * cpt + j] for j in range(cpt)], axis=1)
            d = _dot(vt, a)
            acc = d if acc is None else acc + d
        o_ref[...] = (_rms_rows(acc.T, sg_ref[...]) * out_scale).astype(o_ref.dtype)

    scores_bounded = lam_ref[0, 1] > 0.5
    pl.when(scores_bounded)(lambda: attend(False))
    pl.when(jnp.logical_not(scores_bounded))(lambda: attend(True))


def diff_attention(p_lat, p_ctx, lam, cos2, sin2, q_g, k_g, subln_g, out_scale):
    nq = SEQ // DF_TQ
    w = 2 * DIFF_DK
    kcol0 = D_MODEL // w
    vcol0 = 2 * D_MODEL // w
    vec = lambda n: pl.BlockSpec((1, n), lambda b, h, i: (0, 0))
    n_chunks = SEQ // CTX_LEN
    chunked_t = lambda tab: tab.reshape(n_chunks, CTX_LEN, DIFF_DK).transpose(0, 2, 1)
    q_gain_t = jnp.broadcast_to((q_g.astype(F32) * (DIFF_DK ** -0.5 * LOG2E))[:, None], (DIFF_DK, CTX_LEN))
    score_bound = LOG2E * DIFF_DK ** 0.5 * jnp.max(jnp.abs(q_g)) * jnp.max(jnp.abs(k_g))
    scalars = jnp.stack([lam.reshape(()), (score_bound < SOFTMAX_NOSHIFT_LOG2).astype(F32)]).reshape(1, 2)
    return pl.pallas_call(
        functools.partial(_diff_body, out_scale=out_scale),
        out_shape=jax.ShapeDtypeStruct((N_LAT, D_MODEL), BF16),
        grid=(BATCH, DIFF_HEADS, nq),
        in_specs=[
            pl.BlockSpec(memory_space=pltpu.SMEM),
            pl.BlockSpec((SEQ, w), lambda b, h, i: (b, h)),
            pl.BlockSpec((SEQ, w), lambda b, h, i: (b, kcol0 + h)),
            pl.BlockSpec((CTX_LEN, w), lambda b, h, i: (b, kcol0 + h)),
            pl.BlockSpec((SEQ, w), lambda b, h, i: (b, vcol0 + h)),
            pl.BlockSpec((CTX_LEN, w), lambda b, h, i: (b, vcol0 + h)),
            pl.BlockSpec((SEQ, DIFF_DK), lambda b, h, i: (0, 0)),
            pl.BlockSpec((SEQ, DIFF_DK), lambda b, h, i: (0, 0)),
            pl.BlockSpec((n_chunks, DIFF_DK, CTX_LEN), lambda b, h, i: (0, 0, 0)),
            pl.BlockSpec((n_chunks, DIFF_DK, CTX_LEN), lambda b, h, i: (0, 0, 0)),
            pl.BlockSpec((DIFF_DK, CTX_LEN), lambda b, h, i: (0, 0)),
            vec(DIFF_DK), vec(DIFF_DV),
        ],
        out_specs=pl.BlockSpec((DF_TQ, w), lambda b, h, i: (b * nq + i, h)),
        scratch_shapes=[pltpu.VMEM((2, SEQ // CTX_LEN, DIFF_DK, CTX_LEN), BF16),
                        pltpu.VMEM((DF_NK, DIFF_DK), BF16), pltpu.VMEM((DF_NK, DIFF_DK), BF16),
                        pltpu.VMEM((DF_NK // CTX_LEN, DIFF_DV, CTX_LEN), BF16),
                        pltpu.VMEM((2, DF_KTILES, DF_TK, DF_TQ), BF16)],
        compiler_params=_cparams(("arbitrary", "arbitrary", "arbitrary")),
        name="diff_attention",
    )(scalars, p_lat, p_lat, p_ctx, p_lat, p_ctx, cos2, sin2, chunked_t(cos2), chunked_t(sin2), q_gain_t,
      k_g.reshape(1, DIFF_DK), subln_g.reshape(1, DIFF_DV))


def _rope_tables():
    t = jnp.arange(SEQ)
    row = (t // GRID_W).astype(F32)
    col = (t % GRID_W).astype(F32)
    n_freq = DIFF_DK // 4
    inv = ROPE_THETA ** (-jnp.arange(n_freq, dtype=F32) / n_freq)
    ang = jnp.concatenate([row[:, None] * inv, col[:, None] * inv], axis=-1)
    cos, sin = jnp.cos(ang), jnp.sin(ang)
    return jnp.concatenate([cos, cos], axis=-1), jnp.concatenate([-sin, sin], axis=-1)


def _mlp_ctx(x, mod, g, w1_f32, w2_f32, layer):
    a, w1 = norm_matmul(x, g, mod, w1_f32, layer, shift_idx=3, is_ctx=True, out_dtype=BF16, relu2=True,
                        cast_weights=True, name="mlp_up_ctx")
    x, w2 = res_matmul([a], w2_f32, layer, x, mod, gate_idx=5, is_ctx=True, cast_weights=True, name="mlp_down_ctx")
    return x, w1, w2


def _mlp(x, mod, g, w1, w2):
    a = norm_matmul(x, g, mod, w1, 0, shift_idx=3, is_ctx=False, out_dtype=BF16, relu2=True, name="mlp_up")
    return res_matmul([a], w2, 0, x, mod, gate_idx=5, is_ctx=False, name="mlp_down")


def _bf16_layer(w, idx):
    return w[idx:idx + 1].astype(BF16)


def kernel(x, c, ctx, c_ctx, ada_w, ada_b, norm_mix_g, norm_mlp_g, mlp_w1, mlp_w2, ev_w_in, ev_w_out, na_q_g, na_k_g, na_rpb, hg_lb_logits, hg_gnorm_g, od_w_in, od_w_out, df_q_g, df_k_g, df_lambda, df_subln_g):
    xl = x.reshape(N_LAT, D_MODEL)
    xc = ctx.reshape(N_CTX, D_MODEL)
    cond = jnp.concatenate([c, c_ctx[None], jnp.zeros((MOD_ROWS - BATCH - 1, D_MODEL), F32)], axis=0)
    mods = ada_table(cond, ada_w, ada_b)
    lb_all = jnp.cumsum(jax.nn.softmax(hg_lb_logits.astype(F32), axis=1), axis=1)
    cos2, sin2 = _rope_tables()

    for l in range(DEPTH):
        need_ctx = l < DEPTH - 1
        mod = mods[l]
        g_mix = norm_mix_g[l]
        if l % 2 == 0:
            e = l // 2
            p_ctx, w_in = norm_matmul(xc, g_mix, mod, ev_w_in, e, shift_idx=0, is_ctx=True, out_dtype=F32,
                                      cast_weights=True, name="even_in_ctx")
            p_lat = norm_matmul(xl, g_mix, mod, w_in, 0, shift_idx=0, is_ctx=False, out_dtype=F32, name="even_in")
            na_lat, na_ctx = na_attention(p_lat, p_ctx, na_q_g[e], na_k_g[e], na_bias_table(na_rpb[e]))
            hg_lat, hg_ctx = hgrn2(p_lat, p_ctx, lb_all[0, l], lb_all[1, l], hg_gnorm_g[e])
            if need_ctx:
                xc, w_out = res_matmul([na_ctx, hg_ctx], ev_w_out, e, xc, mod, gate_idx=2, is_ctx=True,
                                       cast_weights=True, name="mix_out_ctx")
            else:
                w_out = _bf16_layer(ev_w_out, e)
            xl = res_matmul([na_lat, hg_lat], w_out, 0, xl, mod, gate_idx=2, is_ctx=False, name="mix_out")
        else:
            assert not need_ctx, "an odd layer followed by another layer needs context outputs"
            o = l // 2
            p_ctx, w_in = norm_matmul(xc, g_mix, mod, od_w_in, o, shift_idx=0, is_ctx=True, out_dtype=F32,
                                      cast_weights=True, name="odd_in_ctx")
            p_lat = norm_matmul(xl, g_mix, mod, w_in, 0, shift_idx=0, is_ctx=False, out_dtype=F32, name="odd_in")
            lam_init = 0.8 - 0.6 * math.exp(-0.3 * l)
            lp = df_lambda[o].astype(F32)
            lam = jnp.exp(jnp.sum(lp[0] * lp[1])) - jnp.exp(jnp.sum(lp[2] * lp[3])) + lam_init
            om = diff_attention(p_lat, p_ctx, lam, cos2, sin2, df_q_g[o], df_k_g[o], df_subln_g[o],
                                1.0 - lam_init)
            xl = res_matmul([om], _bf16_layer(od_w_out, o), 0, xl, mod, gate_idx=2, is_ctx=False, name="mix_out")
        if need_ctx:
            xc, w1, w2 = _mlp_ctx(xc, mod, norm_mlp_g[l], mlp_w1, mlp_w2, l)
        else:
            w1, w2 = _bf16_layer(mlp_w1, l), _bf16_layer(mlp_w2, l)
        xl = _mlp(xl, mod, norm_mlp_g[l], w1, w2)
    return xl.reshape(BATCH, SEQ, D_MODEL)
```

```python
import functools
import math

import jax
import jax.numpy as jnp
import numpy as np
from jax import lax
from jax.experimental import pallas as pl
from jax.experimental.pallas import tpu as pltpu

F32 = jnp.float32
BF16 = jnp.bfloat16

D_MODEL = 2048
BATCH = 4
SEQ = 2048
DEPTH = 2
GRID_W = 64
GRID_ROWS = SEQ // GRID_W
CTX_LEN = 256
HEAD_DIM = 128
NA_HEADS = 8
NA_WIN_R = 8
NA_WIN_C = 16
NA_WIDTH = NA_HEADS * HEAD_DIM
HG_HEADS = 8
HG_CHUNK = 16
HG_WIDTH = HG_HEADS * HEAD_DIM
EVEN_IN = 8 * NA_WIDTH
DIFF_HEADS = 8
DIFF_DK = HEAD_DIM
DIFF_DV = 2 * HEAD_DIM
ODD_IN = 3 * D_MODEL
D_FF = 4 * D_MODEL
N_MOD = 6
ROPE_THETA = 10000.0
EPS = 1e-6

N_LAT = BATCH * SEQ
N_CTX = BATCH * CTX_LEN
MOD_ROWS = 8
CTX_MOD_ROW = BATCH
NEG_BIG = -1e30
LOG2E = math.log2(math.e)
SOFTMAX_NOSHIFT_LOG2 = 100.0

VMEM_LIMIT_V7X = 56 * 1024 * 1024
MM_TILE = 1024
MM_TK = 2048


def _cparams(sem, vmem=VMEM_LIMIT_V7X):
    return pltpu.CompilerParams(dimension_semantics=sem, vmem_limit_bytes=vmem)


def _dot(a, b):
    return jnp.dot(a, b, preferred_element_type=F32)


def _dot_nt(a, b):
    return lax.dot_general(a, b, (((1,), (1,)), ((), ())), preferred_element_type=F32)


def _rms_rows(x, g):
    return x * lax.rsqrt(jnp.mean(x * x, axis=-1, keepdims=True) + EPS) * g


def _mod_row_of_tile(is_ctx, tile_rows):
    if is_ctx:
        return lambda i: CTX_MOD_ROW
    return lambda i: (i * tile_rows) // SEQ


def _ada_body(c_ref, w_ref, b_ref, o_ref):
    c = c_ref[...]
    s = (c * jax.nn.sigmoid(c)).astype(BF16)
    o_ref[0] = _dot(s, w_ref[0].astype(BF16)) + b_ref[0]


def ada_table(cond, ada_w, ada_b):
    tn = 1024
    n = N_MOD * D_MODEL
    out = pl.pallas_call(
        _ada_body,
        out_shape=jax.ShapeDtypeStruct((DEPTH, MOD_ROWS, n), F32),
        grid=(DEPTH, n // tn),
        in_specs=[
            pl.BlockSpec((MOD_ROWS, D_MODEL), lambda l, j: (0, 0)),
            pl.BlockSpec((1, D_MODEL, tn), lambda l, j: (l, 0, j)),
            pl.BlockSpec((1, 1, tn), lambda l, j: (l, 0, j)),
        ],
        out_specs=pl.BlockSpec((1, MOD_ROWS, tn), lambda l, j: (l, 0, j)),
        compiler_params=_cparams(("arbitrary", "arbitrary")),
        name="ada_table",
    )(cond, ada_w, ada_b.reshape(DEPTH, 1, n))
    return out.reshape(DEPTH, MOD_ROWS, N_MOD, D_MODEL)


def _norm_mm_body(x_ref, g_ref, mod_ref, w_ref, *out_and_scratch, shift_idx, relu2, cast_w):
    o_ref, h_ref = out_and_scratch[0], out_and_scratch[-1]

    @pl.when(pl.program_id(1) == 0)
    def _():
        shift = mod_ref[0, shift_idx:shift_idx + 1, :]
        scale = mod_ref[0, shift_idx + 1:shift_idx + 2, :]
        h_ref[...] = (_rms_rows(x_ref[...], g_ref[...]) * (1.0 + scale) + shift).astype(BF16)

    w = w_ref[...]
    if cast_w:
        w = w.astype(BF16)
        out_and_scratch[1][...] = w
    acc = _dot(h_ref[...], w)
    if relu2:
        acc = jnp.maximum(acc, 0.0)
        acc = acc * acc
    o_ref[...] = acc.astype(o_ref.dtype)


def norm_matmul(x, g, mod, w, w_idx, *, shift_idx, is_ctx, out_dtype, relu2=False, col_block0=0, n_cols=None,
                cast_weights=False, name):
    n_rows = x.shape[0]
    tm = MM_TILE
    if cast_weights:
        assert n_rows == tm and col_block0 == 0 and n_cols is None
        tn = MM_TILE // 2
    else:
        tn = MM_TILE * (jnp.dtype(F32).itemsize // jnp.dtype(out_dtype).itemsize)
    col_block0 = col_block0 * MM_TILE // tn
    n_cols = w.shape[2] if n_cols is None else n_cols
    mod_row = _mod_row_of_tile(is_ctx, tm)
    out_shape = jax.ShapeDtypeStruct((n_rows, n_cols), out_dtype)
    out_specs = pl.BlockSpec((tm, tn), lambda i, j: (i, j))
    if cast_weights:
        out_shape = (out_shape, jax.ShapeDtypeStruct((1, D_MODEL, n_cols), BF16))
        out_specs = (out_specs, pl.BlockSpec((None, D_MODEL, tn), lambda i, j: (0, 0, j)))
    return pl.pallas_call(
        functools.partial(_norm_mm_body, shift_idx=shift_idx, relu2=relu2, cast_w=cast_weights),
        out_shape=out_shape,
        grid=(n_rows // tm, n_cols // tn),
        in_specs=[
            pl.BlockSpec((tm, D_MODEL), lambda i, j: (i, 0)),
            pl.BlockSpec((1, D_MODEL), lambda i, j: (0, 0)),
            pl.BlockSpec((1, N_MOD, D_MODEL), lambda i, j: (mod_row(i), 0, 0)),
            pl.BlockSpec((None, D_MODEL, tn), lambda i, j: (w_idx, 0, col_block0 + j)),
        ],
        out_specs=out_specs,
        scratch_shapes=[pltpu.VMEM((tm, D_MODEL), BF16)],
        compiler_params=_cparams(("arbitrary", "arbitrary")),
        name=name,
    )(x, g.reshape(1, D_MODEL), mod, w)


def _res_mm_body(*refs, n_x, nk, gate_idx, cast_w):
    x_refs = refs[:n_x]
    w_ref, res_ref, mod_ref, o_ref = refs[n_x:n_x + 4]
    scratch = refs[n_x + 4:]
    if cast_w:
        wb_ref, scratch = scratch[0], scratch[1:]

    def partial_product():
        acc = None
        k0 = 0
        for x_ref in x_refs:
            kw = x_ref.shape[1]
            w = w_ref[k0:k0 + kw, :]
            if cast_w:
                w = w.astype(BF16)
                wb_ref[k0:k0 + kw, :] = w
            d = _dot(x_ref[...], w)
            acc = d if acc is None else acc + d
            k0 += kw
        return acc

    def finish(acc):
        gate = mod_ref[0, gate_idx:gate_idx + 1, :]
        o_ref[...] = res_ref[...] + gate * acc

    if nk == 1:
        finish(partial_product())
    else:
        acc_ref = scratch[0]
        k = pl.program_id(2)

        @pl.when(k == 0)
        def _():
            acc_ref[...] = jnp.zeros_like(acc_ref)

        acc_ref[...] += partial_product()

        @pl.when(k == nk - 1)
        def _():
            finish(acc_ref[...])


def res_matmul(xs, w, w_idx, res, mod, *, gate_idx, is_ctx, cast_weights=False, name):
    n_rows = res.shape[0]
    _, k_dim, n_dim = w.shape
    tk = MM_TK if len(xs) == 1 else k_dim
    nk = k_dim // tk
    if cast_weights:
        assert n_rows == MM_TILE
        tm, tn = MM_TILE, MM_TILE // 2
    elif nk == 1:
        tm, tn = MM_TILE // 2, n_dim
    else:
        tm = tn = MM_TILE
    if len(xs) == 1:
        x_specs = [pl.BlockSpec((tm, tk), lambda i, j, k: (i, k))]
    else:
        x_specs = [pl.BlockSpec((tm, x.shape[1]), lambda i, j, k: (i, 0)) for x in xs]
    mod_row = _mod_row_of_tile(is_ctx, tm)
    out_shape = jax.ShapeDtypeStruct((n_rows, n_dim), F32)
    out_specs = pl.BlockSpec((tm, tn), lambda i, j, k: (i, j))
    if cast_weights:
        out_shape = (out_shape, jax.ShapeDtypeStruct((1, k_dim, n_dim), BF16))
        out_specs = (out_specs, pl.BlockSpec((None, tk, tn), lambda i, j, k: (0, k, j)))
    return pl.pallas_call(
        functools.partial(_res_mm_body, n_x=len(xs), nk=nk, gate_idx=gate_idx, cast_w=cast_weights),
        out_shape=out_shape,
        grid=(n_rows // tm, n_dim // tn, nk),
        in_specs=x_specs + [
            pl.BlockSpec((None, tk, tn), lambda i, j, k: (w_idx, k, j)),
            pl.BlockSpec((tm, tn), lambda i, j, k: (i, j)),
            pl.BlockSpec((1, N_MOD, tn), lambda i, j, k: (mod_row(i), 0, j)),
        ],
        out_specs=out_specs,
        scratch_shapes=[pltpu.VMEM((tm, tn), F32)] if nk > 1 else [],
        compiler_params=_cparams(("arbitrary", "arbitrary", "arbitrary")),
        name=name,
    )(*xs, w, res, mod)


NA_GROUP_ROWS = 4
NA_GROUP_Q = NA_GROUP_ROWS * GRID_W
NA_GROUP_KROWS = NA_GROUP_ROWS + NA_WIN_R
NA_GROUP_K = NA_GROUP_KROWS * GRID_W
NA_GROUPS = GRID_ROWS // NA_GROUP_ROWS
NA_PREP_ROWS = 256
NA_SCORE_LOOKAHEAD = 1


def _na_window_row0(rq):
    return min(max(rq - NA_WIN_R // 2, 0), GRID_ROWS - NA_WIN_R)


def _na_group_key_row0(r):
    return min(max(r - NA_WIN_R // 2, 0), GRID_ROWS - NA_GROUP_KROWS)


def _na_group_signature(r):
    k0 = _na_group_key_row0(r)
    return (k0 - r,) + tuple(_na_window_row0(r + j) - k0 for j in range(NA_GROUP_ROWS))


NA_GROUP_KINDS = tuple(sorted({_na_group_signature(r): r for r in reversed(range(0, GRID_ROWS, NA_GROUP_ROWS))}.values()))


def _na_group_kind(r):
    return [_na_group_signature(k) for k in NA_GROUP_KINDS].index(_na_group_signature(r))


def na_bias_table(rpb):
    w = GRID_W
    col = np.arange(w)
    c0 = np.clip(col - NA_WIN_C // 2, 0, w - NA_WIN_C)
    in_win = (col[None, :] >= c0[:, None]) & (col[None, :] < c0[:, None] + NA_WIN_C)
    padded = jnp.pad(rpb.astype(F32), ((0, 0), (0, 0), (w, w)))
    toep = jnp.stack([padded[:, :, NA_WIN_C - 1 - qc + w:NA_WIN_C - 1 - qc + 2 * w] for qc in range(w)], axis=2)
    toep = jnp.where(in_win[None, None], toep * LOG2E, NEG_BIG)
    masked = jnp.full((NA_HEADS, w, w), NEG_BIG, F32)
    kinds = []
    for r in NA_GROUP_KINDS:
        q_rows = []
        for rq in range(r, r + NA_GROUP_ROWS):
            r0 = _na_window_row0(rq)
            blocks = []
            for kr in range(_na_group_key_row0(r), _na_group_key_row0(r) + NA_GROUP_KROWS):
                blocks.append(toep[:, kr - rq + NA_WIN_R - 1] if r0 <= kr < r0 + NA_WIN_R else masked)
            q_rows.append(jnp.concatenate(blocks, axis=2))
        kinds.append(jnp.concatenate(q_rows, axis=1))
    return jnp.stack(kinds, axis=1)


def _na_body(q_ref, k_ref, v_ref, qc_ref, kc_ref, vc_ref, qg_ref, kg_ref, bias_ref,
             o_ref, oc_ref, qn_ref, knt_ref, vn_ref, kcnt_ref, vcn_ref):
    scale = HEAD_DIM ** -0.5 * LOG2E
    qg = qg_ref[...]
    kg = kg_ref[...]

    def with_ones(v):
        return jnp.concatenate([v.astype(BF16), jnp.ones(v.shape, BF16)], axis=1)

    def normalised(o_ext):
        return (o_ext[:, :HEAD_DIM] / o_ext[:, HEAD_DIM:]).astype(BF16)

    for i in range(SEQ // NA_PREP_ROWS):
        rows = slice(i * NA_PREP_ROWS, (i + 1) * NA_PREP_ROWS)
        qn_ref[rows, :] = (_rms_rows(q_ref[rows, :], qg) * scale).astype(BF16)
        knt_ref[:, rows] = _rms_rows(k_ref[rows, :], kg).T.astype(BF16)
        vn_ref[rows, :] = with_ones(v_ref[rows, :])
    kcnt = _rms_rows(kc_ref[...], kg).T.astype(BF16)
    vcn = with_ones(vc_ref[...])
    kcnt_ref[...] = kcnt
    vcn_ref[...] = vcn

    def group_rows(gi):
        k0 = _na_group_key_row0(gi * NA_GROUP_ROWS) * GRID_W
        return slice(gi * NA_GROUP_Q, (gi + 1) * NA_GROUP_Q), slice(k0, k0 + NA_GROUP_K)

    def scores(gi):
        qrows, krows = group_rows(gi)
        qb = qn_ref[qrows, :]
        return _dot(qb, knt_ref[:, krows]), _dot(qb, kcnt_ref[...])

    pending = [scores(gi) for gi in range(NA_SCORE_LOOKAHEAD)]
    for gi in range(NA_GROUPS):
        s1, s2 = pending.pop(0)
        if gi + NA_SCORE_LOOKAHEAD < NA_GROUPS:
            pending.append(scores(gi + NA_SCORE_LOOKAHEAD))
        qrows, krows = group_rows(gi)
        s1 = s1 + bias_ref[0, _na_group_kind(gi * NA_GROUP_ROWS)]
        m = jnp.maximum(jnp.max(s1, axis=-1, keepdims=True), jnp.max(s2, axis=-1, keepdims=True))
        p1 = jnp.exp2(s1 - m)
        p2 = jnp.exp2(s2 - m)
        o_ref[qrows, :] = normalised(_dot(p1.astype(BF16), vn_ref[krows, :]) + _dot(p2.astype(BF16), vcn_ref[...]))

    qcn = (_rms_rows(qc_ref[...], qg) * scale).astype(BF16)
    s = _dot(qcn, kcnt)
    m = jnp.max(s, axis=-1, keepdims=True)
    p = jnp.exp2(s - m)
    oc_ref[...] = normalised(_dot(p.astype(BF16), vcn))


def na_attention(p_lat, p_ctx, q_g, k_g, bias):
    hd = HEAD_DIM
    lat = lambda seg: pl.BlockSpec((SEQ, hd), lambda h, b: (b, seg * NA_HEADS + h))
    ctx = lambda seg: pl.BlockSpec((CTX_LEN, hd), lambda h, b: (b, seg * NA_HEADS + h))
    return pl.pallas_call(
        _na_body,
        out_shape=(jax.ShapeDtypeStruct((N_LAT, NA_WIDTH), BF16),
                   jax.ShapeDtypeStruct((N_CTX, NA_WIDTH), BF16)),
        grid=(NA_HEADS, BATCH),
        in_specs=[lat(0), lat(1), lat(2), ctx(0), ctx(1), ctx(2),
                  pl.BlockSpec((1, hd), lambda h, b: (0, 0)),
                  pl.BlockSpec((1, hd), lambda h, b: (0, 0)),
                  pl.BlockSpec((1, len(NA_GROUP_KINDS), NA_GROUP_Q, NA_GROUP_K), lambda h, b: (h, 0, 0, 0))],
        out_specs=(pl.BlockSpec((SEQ, hd), lambda h, b: (b, h)),
                   pl.BlockSpec((CTX_LEN, hd), lambda h, b: (b, h))),
        scratch_shapes=[pltpu.VMEM((SEQ, hd), BF16), pltpu.VMEM((hd, SEQ), BF16), pltpu.VMEM((SEQ, 2 * hd), BF16),
                        pltpu.VMEM((hd, CTX_LEN), BF16), pltpu.VMEM((CTX_LEN, 2 * hd), BF16)],
        compiler_params=_cparams(("arbitrary", "arbitrary")),
        name="na_attention",
    )(p_lat, p_lat, p_lat, p_ctx, p_ctx, p_ctx, q_g.reshape(1, hd), k_g.reshape(1, hd), bias)


HG_BLK = 128
HG_CPB = HG_BLK // HG_CHUNK
HG_CTX_BLKS = CTX_LEN // HG_BLK
HG_LAT_BLKS = SEQ // HG_BLK
HG_BLKS = HG_CTX_BLKS + HG_LAT_BLKS
HG_SCAN_STEPS = 3


def _hg_ref_rows(reverse):
    b, c = HG_BLK, HG_CHUNK
    if not reverse:
        return {c: [None] + [c * i - 1 for i in range(1, b // c)],
                32: [32 * j + 15 for j in range(b // 32)], 64: [64 * j + 31 for j in range(b // 64)],
                128: [63], "end": [b - 1]}
    return {c: [c * i + c for i in range(b // c - 1)] + [None],
            32: [32 * j + 16 for j in range(b // 32)], 64: [64 * j + 32 for j in range(b // 64)],
            128: [64], "end": [0]}


def _hg_level_codes(reverse):
    t = lax.broadcasted_iota(jnp.int32, (HG_BLK, HG_BLK), 0)
    s = lax.broadcasted_iota(jnp.int32, (HG_BLK, HG_BLK), 1)
    if reverse:
        t, s = s, t
    code = jnp.where((t >> 6) > (s >> 6), 4, 0)
    code = jnp.where(((t >> 6) == (s >> 6)) & ((t >> 5) > (s >> 5)), 3, code)
    code = jnp.where(((t >> 5) == (s >> 5)) & ((t >> 4) > (s >> 4)), 2, code)
    return jnp.where(((t >> 4) == (s >> 4)) & (s <= t), 1, code)


def _hg_body(q_ref, zf_ref, zb_ref, i_ref, g_ref, qc_ref, zfc_ref, zbc_ref, ic_ref, gc_ref,
             lbf_ref, lbb_ref, gn_ref, o_ref, oc_ref,
             qd_s, ka_s, q64_s, k64_s, q128_s, k128_s, qb_s, ke_s, gb_s, v_s, o_s, st_s, x_s, code_s):
    lbs = (lbf_ref[...], lbb_ref[...])
    ri = lax.broadcasted_iota(jnp.int32, (HG_BLK, HG_BLK), 0)
    ci = lax.broadcasted_iota(jnp.int32, (HG_BLK, HG_BLK), 1)
    tri = jnp.where(ci <= ri, 1.0, 0.0).astype(BF16)
    for d in range(2):
        code_s[d] = _hg_level_codes(d == 1)

    def prefix_rows(x):
        hi = x.astype(BF16)
        r1 = x - hi.astype(F32)
        mid = r1.astype(BF16)
        lo = (r1 - mid.astype(F32)).astype(BF16)
        y = _dot(tri, jnp.concatenate([hi, mid, lo], axis=1))
        return y[:, :HEAD_DIM] + y[:, HEAD_DIM:2 * HEAD_DIM] + y[:, 2 * HEAD_DIM:]

    def ref_rows(d, rows):
        group = HG_BLK // len(rows)
        parts = [jnp.zeros((group, HEAD_DIM), F32) if r is None else
                 jnp.broadcast_to(x_s[d, r:r + 1, :], (group, HEAD_DIM)) for r in rows]
        return parts[0] if len(parts) == 1 else jnp.concatenate(parts, axis=0)

    def prep_block(blk, q, zf, zb, v):
        qs = q * jax.nn.sigmoid(q)
        v_s[blk] = v.astype(BF16)
        for d, z in enumerate((zf, zb)):
            f = lbs[d] + (1.0 - lbs[d]) * jax.nn.sigmoid(z)
            logf = jnp.log2(f)
            k = 1.0 - f
            x = prefix_rows(logf)
            x_s[d] = x
            if d == 1:
                x = ref_rows(d, [HG_BLK - 1]) - x + logf
                x_s[d] = x
            rows = _hg_ref_rows(d == 1)
            r16, r32, r64, r128 = (ref_rows(d, rows[g]) for g in (HG_CHUNK, 32, 64, 128))
            x_end = ref_rows(d, rows["end"])
            e32 = jnp.exp2(-jnp.abs(x - r32))
            e64 = jnp.exp2(-jnp.abs(x - r64))
            e128 = jnp.exp2(-jnp.abs(x - r128))
            qd_s[d, blk] = (qs * jnp.exp2(x - r16)).astype(BF16)
            ka_s[d, blk] = jnp.concatenate([(k * jnp.exp2(r16 - x)).T, (k * e32).T], axis=1).astype(BF16)
            q64_s[d, blk] = (qs * e64).astype(BF16)
            k64_s[d, blk] = (k * e64).T.astype(BF16)
            q128_s[d, blk] = (qs * e128).astype(BF16)
            k128_s[d, blk] = (k * e128).T.astype(BF16)
            qb_s[d, blk] = (qs * jnp.exp2(x)).astype(BF16)
            ke_s[d, blk] = (k * jnp.exp2(x_end - x)).T.astype(BF16)
            gb_s[d, blk] = jnp.exp2(x_end).T

    for cb in range(HG_CTX_BLKS):
        rows = slice(cb * HG_BLK, (cb + 1) * HG_BLK)
        prep_block(cb, qc_ref[rows, :], zfc_ref[rows, :], zbc_ref[rows, :], ic_ref[rows, :])

    def prep_lat(n, _):
        rows = pl.ds(pl.multiple_of(n * HG_BLK, HG_BLK), HG_BLK)
        prep_block(HG_CTX_BLKS + n, q_ref[rows, :], zf_ref[rows, :], zb_ref[rows, :], i_ref[rows, :])
        return 0

    lax.fori_loop(0, HG_LAT_BLKS, prep_lat, 0, unroll=2)

    st_s[...] = jnp.zeros_like(st_s)

    def scan_steps(i, _):
        chains = []
        for n in [i * HG_SCAN_STEPS + j for j in range(HG_SCAN_STEPS)]:
            chains.append((0, n))
            chains.append((1, jnp.where(n < HG_CTX_BLKS, HG_CTX_BLKS - 1 - n, HG_BLKS + HG_CTX_BLKS - 1 - n)))
        ready = []
        for d, blk in chains:
            a1 = _dot(qd_s[d, blk], ka_s[d, blk])
            a64 = _dot(q64_s[d, blk], k64_s[d, blk])
            a128 = _dot(q128_s[d, blk], k128_s[d, blk])
            v = v_s[blk]
            inc = _dot(ke_s[d, blk], v)
            code = code_s[d]
            att = jnp.where(code == 1, a1[:, :HG_BLK],
                            jnp.where(code == 2, a1[:, HG_BLK:],
                                      jnp.where(code == 3, a64, jnp.where(code == 4, a128, 0.0))))
            ready.append((jnp.concatenate([att.astype(BF16), qb_s[d, blk]], axis=1), v, inc))
        for (d, blk), (lhs, v, inc) in zip(chains, ready):
            state = st_s[d]
            o_s[d, blk] = _dot(lhs, jnp.concatenate([v, state.astype(BF16)], axis=0))
            st_s[d] = gb_s[d, blk] * state + inc
        return 0

    lax.fori_loop(0, HG_BLKS // HG_SCAN_STEPS, scan_steps, 0)

    gn = gn_ref[...]

    def finish(blk, gate):
        o = o_s[0, blk] + o_s[1, blk]
        return (_rms_rows(o, gn) * (gate * jax.nn.sigmoid(gate))).astype(BF16)

    for cb in range(HG_CTX_BLKS):
        rows = slice(cb * HG_BLK, (cb + 1) * HG_BLK)
        oc_ref[rows, :] = finish(cb, gc_ref[rows, :])

    def fin_lat(n, _):
        rows = pl.ds(pl.multiple_of(n * HG_BLK, HG_BLK), HG_BLK)
        o_ref[rows, :] = finish(HG_CTX_BLKS + n, g_ref[rows, :])
        return 0

    lax.fori_loop(0, HG_LAT_BLKS, fin_lat, 0, unroll=4)


def hgrn2(p_lat, p_ctx, lb_fwd, lb_bwd, gn_g):
    hd = HEAD_DIM
    lat = lambda seg: pl.BlockSpec((SEQ, hd), lambda b, h: (b, seg * HG_HEADS + h))
    ctx = lambda seg: pl.BlockSpec((CTX_LEN, hd), lambda b, h: (b, seg * HG_HEADS + h))
    head_vec = pl.BlockSpec((1, hd), lambda b, h: (0, h))
    dir_bf = lambda width=hd: pltpu.VMEM((2, HG_BLKS, HG_BLK, width), BF16)
    dir_f32 = lambda: pltpu.VMEM((2, HG_BLKS, HG_BLK, hd), F32)
    return pl.pallas_call(
        _hg_body,
        out_shape=(jax.ShapeDtypeStruct((N_LAT, HG_WIDTH), BF16),
                   jax.ShapeDtypeStruct((N_CTX, HG_WIDTH), BF16)),
        grid=(BATCH, HG_HEADS),
        in_specs=[lat(3), lat(4), lat(5), lat(6), lat(7), ctx(3), ctx(4), ctx(5), ctx(6), ctx(7),
                  head_vec, head_vec, pl.BlockSpec((1, hd), lambda b, h: (0, 0))],
        out_specs=(pl.BlockSpec((SEQ, hd), lambda b, h: (b, h)),
                   pl.BlockSpec((CTX_LEN, hd), lambda b, h: (b, h))),
        scratch_shapes=[dir_bf(), dir_bf(2 * hd), dir_bf(), dir_bf(), dir_bf(), dir_bf(), dir_bf(), dir_bf(),
                        dir_f32(), pltpu.VMEM((HG_BLKS, HG_BLK, hd), BF16), dir_f32(),
                        pltpu.VMEM((2, hd, hd), F32), pltpu.VMEM((2, HG_BLK, hd), F32),
                        pltpu.VMEM((2, HG_BLK, HG_BLK), jnp.int32)],
        compiler_params=_cparams(("arbitrary", "arbitrary")),
        name="hgrn2",
    )(p_lat, p_lat, p_lat, p_lat, p_lat, p_ctx, p_ctx, p_ctx, p_ctx, p_ctx,
      lb_fwd.reshape(1, -1), lb_bwd.reshape(1, -1), gn_g.reshape(1, hd))


DF_TQ = 512
DF_NK = SEQ + CTX_LEN
DF_TK = 768
DF_KTILES = DF_NK // DF_TK
DF_SCORE_LOOKAHEAD = 2


def _diff_body(lam_ref, q_ref, k_ref, kc_ref, v_ref, vc_ref, cos_ref, sin_ref, cost_ref, sint_ref,
               qgt_ref, kg_ref, sg_ref, o_ref, qt_s, k0_s, k1_s, vt_s, e_s, *, out_scale):
    dk = DIFF_DK
    kg = kg_ref[...]

    @pl.when(pl.program_id(2) == 0)
    def _():
        ri = lax.broadcasted_iota(jnp.int32, (dk, dk), 0)
        ci = lax.broadcasted_iota(jnp.int32, (dk, dk), 1)
        ones = jnp.ones((dk, dk), BF16)
        swap_halves = jnp.where(ri == ((ci + dk // 2) & (dk - 1)), 1.0, 0.0).astype(BF16)

        def times(x, m):
            hi = x.astype(BF16)
            lo = (x - hi.astype(F32)).astype(BF16)
            return _dot(hi, m) + _dot(lo, m)

        def key_rows(x, c2, s2):
            xn = x * lax.rsqrt(times(x * x, ones) * (1.0 / dk) + EPS) * kg
            return (xn * c2 + times(xn, swap_halves) * s2).astype(BF16)

        def prep(i, _):
            rows = pl.ds(pl.multiple_of(i * CTX_LEN, CTX_LEN), CTX_LEN)
            c2 = cos_ref[rows, :]
            s2 = sin_ref[rows, :]
            kk = k_ref[rows, :]
            k0_s[rows, :] = key_rows(kk[:, :dk], c2, s2)
            k1_s[rows, :] = key_rows(kk[:, dk:], c2, s2)
            vt_s[i] = v_ref[rows, :].T.astype(BF16)
            qq_t = q_ref[rows, :].T
            ct = cost_ref[i]
            st = sint_ref[i]
            for j in range(2):
                xt = qq_t[j * dk:(j + 1) * dk, :]
                inv = lax.rsqrt(jnp.mean(xt * xt, axis=0, keepdims=True) + EPS)
                xn = xt * inv * qgt_ref[...]
                swapped = jnp.concatenate([xn[dk // 2:, :], xn[:dk // 2, :]], axis=0)
                qt_s[j, i] = (xn * ct + swapped * st).astype(BF16)
            return 0

        lax.fori_loop(0, SEQ // CTX_LEN, prep, 0, unroll=4)
        kc = kc_ref[...]
        k0_s[SEQ:, :] = _rms_rows(kc[:, :dk], kg).astype(BF16)
        k1_s[SEQ:, :] = _rms_rows(kc[:, dk:], kg).astype(BF16)
        vt_s[SEQ // CTX_LEN] = vc_ref[...].T.astype(BF16)

    lam = lam_ref[0, 0]
    qpt = DF_TQ // CTX_LEN
    q0 = pl.program_id(2) * qpt
    qt = [jnp.concatenate([qt_s[i, q0 + j] for j in range(qpt)], axis=1) for i in range(2)]
    k_s = (k0_s, k1_s)

    cpt = DF_TK // CTX_LEN
    jobs = [(t, i) for t in range(DF_KTILES) for i in range(2)]

    def scores(job):
        t, i = job
        return _dot(k_s[i][t * DF_TK:(t + 1) * DF_TK, :], qt[i])

    def attend(shifted):
        m = [None, None]
        l = [None, None]
        m_tile = [[None] * DF_KTILES for _ in range(2)]
        pending = [scores(job) for job in jobs[:DF_SCORE_LOOKAHEAD]]
        for n, (t, i) in enumerate(jobs):
            s = pending.pop(0)
            if n + DF_SCORE_LOOKAHEAD < len(jobs):
                pending.append(scores(jobs[n + DF_SCORE_LOOKAHEAD]))
            if shifted:
                tile_max = jnp.max(s, axis=0, keepdims=True)
                m_new = tile_max if t == 0 else jnp.maximum(m[i], tile_max)
                e = jnp.exp2(s - m_new)
            else:
                e = jnp.exp2(s)
            e_s[i, t] = e.astype(BF16)
            tile_sum = jnp.sum(e, axis=0, keepdims=True)
            if t == 0:
                l[i] = tile_sum
            elif shifted:
                l[i] = jnp.exp2(m[i] - m_new) * l[i] + tile_sum
            else:
                l[i] = l[i] + tile_sum
            if shifted:
                m[i] = m_new
                m_tile[i][t] = m_new

        weight = (1.0 / l[0], lam / l[1])
        acc = None
        for t in range(DF_KTILES):
            if shifted:
                f0, f1 = ((jnp.exp2(m_tile[i][t] - m[i]) * weight[i]).astype(BF16) for i in range(2))
            else:
                f0, f1 = (w.astype(BF16) for w in weight)
            a = e_s[0, t] * f0 - e_s[1, t] * f1
            vt = jnp.concatenate([vt_s[t * cpt + j] for j in range(cpt)], axis=1)
            d = _dot(vt, a)
            acc = d if acc is None else acc + d
        o_ref[...] = (_rms_rows(acc.T, sg_ref[...]) * out_scale).astype(o_ref.dtype)

    scores_bounded = lam_ref[0, 1] > 0.5
    pl.when(scores_bounded)(lambda: attend(False))
    pl.when(jnp.logical_not(scores_bounded))(lambda: attend(True))


def diff_attention(p_lat, p_ctx, lam, cos2, sin2, q_g, k_g, subln_g, out_scale):
    nq = SEQ // DF_TQ
    w = 2 * DIFF_DK
    kcol0 = D_MODEL // w
    vcol0 = 2 * D_MODEL // w
    vec = lambda n: pl.BlockSpec((1, n), lambda b, h, i: (0, 0))
    n_chunks = SEQ // CTX_LEN
    chunked_t = lambda tab: tab.reshape(n_chunks, CTX_LEN, DIFF_DK).transpose(0, 2, 1)
    q_gain_t = jnp.broadcast_to((q_g.astype(F32) * (DIFF_DK ** -0.5 * LOG2E))[:, None], (DIFF_DK, CTX_LEN))
    score_bound = LOG2E * DIFF_DK ** 0.5 * jnp.max(jnp.abs(q_g)) * jnp.max(jnp.abs(k_g))
    scalars = jnp.stack([lam.reshape(()), (score_bound < SOFTMAX_NOSHIFT_LOG2).astype(F32)]).reshape(1, 2)
    return pl.pallas_call(
        functools.partial(_diff_body, out_scale=out_scale),
        out_shape=jax.ShapeDtypeStruct((N_LAT, D_MODEL), BF16),
        grid=(BATCH, DIFF_HEADS, nq),
        in_specs=[
            pl.BlockSpec(memory_space=pltpu.SMEM),
            pl.BlockSpec((SEQ, w), lambda b, h, i: (b, h)),
            pl.BlockSpec((SEQ, w), lambda b, h, i: (b, kcol0 + h)),
            pl.BlockSpec((CTX_LEN, w), lambda b, h, i: (b, kcol0 + h)),
            pl.BlockSpec((SEQ, w), lambda b, h, i: (b, vcol0 + h)),
            pl.BlockSpec((CTX_LEN, w), lambda b, h, i: (b, vcol0 + h)),
            pl.BlockSpec((SEQ, DIFF_DK), lambda b, h, i: (0, 0)),
            pl.BlockSpec((SEQ, DIFF_DK), lambda b, h, i: (0, 0)),
            pl.BlockSpec((n_chunks, DIFF_DK, CTX_LEN), lambda b, h, i: (0, 0, 0)),
            pl.BlockSpec((n_chunks, DIFF_DK, CTX_LEN), lambda b, h, i: (0, 0, 0)),
            pl.BlockSpec((DIFF_DK, CTX_LEN), lambda b, h, i: (0, 0)),
            vec(DIFF_DK), vec(DIFF_DV),
        ],
        out_specs=pl.BlockSpec((DF_TQ, w), lambda b, h, i: (b * nq + i, h)),
        scratch_shapes=[pltpu.VMEM((2, SEQ // CTX_LEN, DIFF_DK, CTX_LEN), BF16),
                        pltpu.VMEM((DF_NK, DIFF_DK), BF16), pltpu.VMEM((DF_NK, DIFF_DK), BF16),
                        pltpu.VMEM((DF_NK // CTX_LEN, DIFF_DV, CTX_LEN), BF16),
                        pltpu.VMEM((2, DF_KTILES, DF_TK, DF_TQ), BF16)],
        compiler_params=_cparams(("arbitrary", "arbitrary", "arbitrary")),
        name="diff_attention",
    )(scalars, p_lat, p_lat, p_ctx, p_lat, p_ctx, cos2, sin2, chunked_t(cos2), chunked_t(sin2), q_gain_t,
      k_g.reshape(1, DIFF_DK), subln_g.reshape(1, DIFF_DV))


def _rope_tables():
    t = jnp.arange(SEQ)
    row = (t // GRID_W).astype(F32)
    col = (t % GRID_W).astype(F32)
    n_freq = DIFF_DK // 4
    inv = ROPE_THETA ** (-jnp.arange(n_freq, dtype=F32) / n_freq)
    ang = jnp.concatenate([row[:, None] * inv, col[:, None] * inv], axis=-1)
    cos, sin = jnp.cos(ang), jnp.sin(ang)
    return jnp.concatenate([cos, cos], axis=-1), jnp.concatenate([-sin, sin], axis=-1)


def _mlp_ctx(x, mod, g, w1_f32, w2_f32, layer):
    a, w1 = norm_matmul(x, g, mod, w1_f32, layer, shift_idx=3, is_ctx=True, out_dtype=BF16, relu2=True,
                        cast_weights=True, name="mlp_up_ctx")
    x, w2 = res_matmul([a], w2_f32, layer, x, mod, gate_idx=5, is_ctx=True, cast_weights=True, name="mlp_down_ctx")
    return x, w1, w2


def _mlp(x, mod, g, w1, w2):
    a = norm_matmul(x, g, mod, w1, 0, shift_idx=3, is_ctx=False, out_dtype=BF16, relu2=True, name="mlp_up")
    return res_matmul([a], w2, 0, x, mod, gate_idx=5, is_ctx=False, name="mlp_down")


def _bf16_layer(w, idx):
    return w[idx:idx + 1].astype(BF16)


def kernel(x, c, ctx, c_ctx, ada_w, ada_b, norm_mix_g, norm_mlp_g, mlp_w1, mlp_w2, ev_w_in, ev_w_out, na_q_g, na_k_g, na_rpb, hg_lb_logits, hg_gnorm_g, od_w_in, od_w_out, df_q_g, df_k_g, df_lambda, df_subln_g):
    xl = x.reshape(N_LAT, D_MODEL)
    xc = ctx.reshape(N_CTX, D_MODEL)
    cond = jnp.concatenate([c, c_ctx[None], jnp.zeros((MOD_ROWS - BATCH - 1, D_MODEL), F32)], axis=0)
    mods = ada_table(cond, ada_w, ada_b)
    lb_all = jnp.cumsum(jax.nn.softmax(hg_lb_logits.astype(F32), axis=1), axis=1)
    cos2, sin2 = _rope_tables()

    for l in range(DEPTH):
        need_ctx = l < DEPTH - 1
        mod = mods[l]
        g_mix = norm_mix_g[l]
        if l % 2 == 0:
            e = l // 2
            p_ctx, w_in = norm_matmul(xc, g_mix, mod, ev_w_in, e, shift_idx=0, is_ctx=True, out_dtype=F32,
                                      cast_weights=True, name="even_in_ctx")
            p_lat = norm_matmul(xl, g_mix, mod, w_in, 0, shift_idx=0, is_ctx=False, out_dtype=F32, name="even_in")
            na_lat, na_ctx = na_attention(p_lat, p_ctx, na_q_g[e], na_k_g[e], na_bias_table(na_rpb[e]))
            hg_lat, hg_ctx = hgrn2(p_lat, p_ctx, lb_all[0, l], lb_all[1, l], hg_gnorm_g[e])
            if need_ctx:
                xc, w_out = res_matmul([na_ctx, hg_ctx], ev_w_out, e, xc, mod, gate_idx=2, is_ctx=True,
                                       cast_weights=True, name="mix_out_ctx")
            else:
                w_out = _bf16_layer(ev_w_out, e)
            xl = res_matmul([na_lat, hg_lat], w_out, 0, xl, mod, gate_idx=2, is_ctx=False, name="mix_out")
        else:
            assert not need_ctx, "an odd layer followed by another layer needs context outputs"
            o = l // 2
            p_ctx, w_in = norm_matmul(xc, g_mix, mod, od_w_in, o, shift_idx=0, is_ctx=True, out_dtype=F32,
                                      cast_weights=True, name="odd_in_ctx")
            p_lat = norm_matmul(xl, g_mix, mod, w_in, 0, shift_idx=0, is_ctx=False, out_dtype=F32, name="odd_in")
            lam_init = 0.8 - 0.6 * math.exp(-0.3 * l)
            lp = df_lambda[o].astype(F32)
            lam = jnp.exp(jnp.sum(lp[0] * lp[1])) - jnp.exp(jnp.sum(lp[2] * lp[3])) + lam_init
            om = diff_attention(p_lat, p_ctx, lam, cos2, sin2, df_q_g[o], df_k_g[o], df_subln_g[o],
                                1.0 - lam_init)
            xl = res_matmul([om], _bf16_layer(od_w_out, o), 0, xl, mod, gate_idx=2, is_ctx=False, name="mix_out")
        if need_ctx:
            xc, w1, w2 = _mlp_ctx(xc, mod, norm_mlp_g[l], mlp_w1, mlp_w2, l)
        else:
            w1, w2 = _bf16_layer(mlp_w1, l), _bf16_layer(mlp_w2, l)
        xl = _mlp(xl, mod, norm_mlp_g[l], w1, w2)
    return xl.reshape(BATCH, SEQ, D_MODEL)
```

```python
import functools
import math

import jax
import jax.numpy as jnp
import numpy as np
from jax import lax
from jax.experimental import pallas as pl
from jax.experimental.pallas import tpu as pltpu

F32 = jnp.float32
BF16 = jnp.bfloat16

D_MODEL = 2048
BATCH = 4
SEQ = 2048
DEPTH = 2
GRID_W = 64
GRID_ROWS = SEQ // GRID_W
CTX_LEN = 256
HEAD_DIM = 128
NA_HEADS = 8
NA_WIN_R = 8
NA_WIN_C = 16
NA_WIDTH = NA_HEADS * HEAD_DIM
HG_HEADS = 8
HG_CHUNK = 16
HG_WIDTH = HG_HEADS * HEAD_DIM
EVEN_IN = 8 * NA_WIDTH
DIFF_HEADS = 8
DIFF_DK = HEAD_DIM
DIFF_DV = 2 * HEAD_DIM
ODD_IN = 3 * D_MODEL
D_FF = 4 * D_MODEL
N_MOD = 6
ROPE_THETA = 10000.0
EPS = 1e-6

N_LAT = BATCH * SEQ
N_CTX = BATCH * CTX_LEN
MOD_ROWS = 8
CTX_MOD_ROW = BATCH
NEG_BIG = -1e30
LOG2E = math.log2(math.e)
SOFTMAX_NOSHIFT_LOG2 = 100.0

VMEM_LIMIT_V7X = 56 * 1024 * 1024
MM_TILE = 1024
MM_TK = 2048


def _cparams(sem, vmem=VMEM_LIMIT_V7X):
    return pltpu.CompilerParams(dimension_semantics=sem, vmem_limit_bytes=vmem)


def _dot(a, b):
    return jnp.dot(a, b, preferred_element_type=F32)


def _dot_nt(a, b):
    return lax.dot_general(a, b, (((1,), (1,)), ((), ())), preferred_element_type=F32)


def _rms_rows(x, g):
    return x * lax.rsqrt(jnp.mean(x * x, axis=-1, keepdims=True) + EPS) * g


def _mod_row_of_tile(is_ctx, tile_rows):
    if is_ctx:
        return lambda i: CTX_MOD_ROW
    return lambda i: (i * tile_rows) // SEQ


def _ada_body(c_ref, w_ref, b_ref, o_ref):
    c = c_ref[...]
    s = (c * jax.nn.sigmoid(c)).astype(BF16)
    o_ref[0] = _dot(s, w_ref[0].astype(BF16)) + b_ref[0]


def ada_table(cond, ada_w, ada_b):
    tn = 1024
    n = N_MOD * D_MODEL
    out = pl.pallas_call(
        _ada_body,
        out_shape=jax.ShapeDtypeStruct((DEPTH, MOD_ROWS, n), F32),
        grid=(DEPTH, n // tn),
        in_specs=[
            pl.BlockSpec((MOD_ROWS, D_MODEL), lambda l, j: (0, 0)),
            pl.BlockSpec((1, D_MODEL, tn), lambda l, j: (l, 0, j)),
            pl.BlockSpec((1, 1, tn), lambda l, j: (l, 0, j)),
        ],
        out_specs=pl.BlockSpec((1, MOD_ROWS, tn), lambda l, j: (l, 0, j)),
        compiler_params=_cparams(("arbitrary", "arbitrary")),
        name="ada_table",
    )(cond, ada_w, ada_b.reshape(DEPTH, 1, n))
    return out.reshape(DEPTH, MOD_ROWS, N_MOD, D_MODEL)


def _norm_mm_body(x_ref, g_ref, mod_ref, w_ref, *out_and_scratch, shift_idx, relu2, cast_w):
    o_ref, h_ref = out_and_scratch[0], out_and_scratch[-1]

    @pl.when(pl.program_id(1) == 0)
    def _():
        shift = mod_ref[0, shift_idx:shift_idx + 1, :]
        scale = mod_ref[0, shift_idx + 1:shift_idx + 2, :]
        h_ref[...] = (_rms_rows(x_ref[...], g_ref[...]) * (1.0 + scale) + shift).astype(BF16)

    w = w_ref[...]
    if cast_w:
        w = w.astype(BF16)
        out_and_scratch[1][...] = w
    acc = _dot(h_ref[...], w)
    if relu2:
        acc = jnp.maximum(acc, 0.0)
        acc = acc * acc
    o_ref[...] = acc.astype(o_ref.dtype)


def norm_matmul(x, g, mod, w, w_idx, *, shift_idx, is_ctx, out_dtype, relu2=False, col_block0=0, n_cols=None,
                cast_weights=False, name):
    n_rows = x.shape[0]
    tm = MM_TILE
    if cast_weights:
        assert n_rows == tm and col_block0 == 0 and n_cols is None
        tn = MM_TILE
    else:
        tn = MM_TILE * (jnp.dtype(F32).itemsize // jnp.dtype(out_dtype).itemsize)
    col_block0 = col_block0 * MM_TILE // tn
    n_cols = w.shape[2] if n_cols is None else n_cols
    mod_row = _mod_row_of_tile(is_ctx, tm)
    out_shape = jax.ShapeDtypeStruct((n_rows, n_cols), out_dtype)
    out_specs = pl.BlockSpec((tm, tn), lambda i, j: (i, j))
    if cast_weights:
        out_shape = (out_shape, jax.ShapeDtypeStruct((1, D_MODEL, n_cols), BF16))
        out_specs = (out_specs, pl.BlockSpec((None, D_MODEL, tn), lambda i, j: (0, 0, j)))
    return pl.pallas_call(
        functools.partial(_norm_mm_body, shift_idx=shift_idx, relu2=relu2, cast_w=cast_weights),
        out_shape=out_shape,
        grid=(n_rows // tm, n_cols // tn),
        in_specs=[
            pl.BlockSpec((tm, D_MODEL), lambda i, j: (i, 0), pipeline_mode=pl.Buffered(1) if cast_weights else None),
            pl.BlockSpec((1, D_MODEL), lambda i, j: (0, 0)),
            pl.BlockSpec((1, N_MOD, D_MODEL), lambda i, j: (mod_row(i), 0, 0)),
            pl.BlockSpec((None, D_MODEL, tn), lambda i, j: (w_idx, 0, col_block0 + j)),
        ],
        out_specs=out_specs,
        scratch_shapes=[pltpu.VMEM((tm, D_MODEL), BF16)],
        compiler_params=_cparams(("arbitrary", "arbitrary")),
        name=name,
    )(x, g.reshape(1, D_MODEL), mod, w)


def _res_mm_body(*refs, n_x, nk, gate_idx, cast_w):
    x_refs = refs[:n_x]
    w_ref, res_ref, mod_ref, o_ref = refs[n_x:n_x + 4]
    scratch = refs[n_x + 4:]
    if cast_w:
        wb_ref, scratch = scratch[0], scratch[1:]

    def partial_product():
        acc = None
        k0 = 0
        for x_ref in x_refs:
            kw = x_ref.shape[1]
            w = w_ref[k0:k0 + kw, :]
            if cast_w:
                w = w.astype(BF16)
                wb_ref[k0:k0 + kw, :] = w
            d = _dot(x_ref[...], w)
            acc = d if acc is None else acc + d
            k0 += kw
        return acc

    def finish(acc):
        gate = mod_ref[0, gate_idx:gate_idx + 1, :]
        o_ref[...] = res_ref[...] + gate * acc

    if nk == 1:
        finish(partial_product())
    else:
        acc_ref = scratch[0]
        k = pl.program_id(2)

        @pl.when(k == 0)
        def _():
            acc_ref[...] = jnp.zeros_like(acc_ref)

        acc_ref[...] += partial_product()

        @pl.when(k == nk - 1)
        def _():
            finish(acc_ref[...])


def res_matmul(xs, w, w_idx, res, mod, *, gate_idx, is_ctx, cast_weights=False, name):
    n_rows = res.shape[0]
    _, k_dim, n_dim = w.shape
    tk = MM_TK if len(xs) == 1 else k_dim
    nk = k_dim // tk
    if cast_weights:
        assert n_rows == MM_TILE
        tm, tn = MM_TILE, MM_TILE // 2
    elif nk == 1:
        tm, tn = MM_TILE // 2, n_dim
    else:
        tm = tn = MM_TILE
    if len(xs) == 1:
        x_specs = [pl.BlockSpec((tm, tk), lambda i, j, k: (i, k))]
    else:
        x_specs = [pl.BlockSpec((tm, x.shape[1]), lambda i, j, k: (i, 0)) for x in xs]
    mod_row = _mod_row_of_tile(is_ctx, tm)
    out_shape = jax.ShapeDtypeStruct((n_rows, n_dim), F32)
    out_specs = pl.BlockSpec((tm, tn), lambda i, j, k: (i, j))
    if cast_weights:
        out_shape = (out_shape, jax.ShapeDtypeStruct((1, k_dim, n_dim), BF16))
        out_specs = (out_specs, pl.BlockSpec((None, tk, tn), lambda i, j, k: (0, k, j)))
    return pl.pallas_call(
        functools.partial(_res_mm_body, n_x=len(xs), nk=nk, gate_idx=gate_idx, cast_w=cast_weights),
        out_shape=out_shape,
        grid=(n_rows // tm, n_dim // tn, nk),
        in_specs=x_specs + [
            pl.BlockSpec((None, tk, tn), lambda i, j, k: (w_idx, k, j)),
            pl.BlockSpec((tm, tn), lambda i, j, k: (i, j)),
            pl.BlockSpec((1, N_MOD, tn), lambda i, j, k: (mod_row(i), 0, j)),
        ],
        out_specs=out_specs,
        scratch_shapes=[pltpu.VMEM((tm, tn), F32)] if nk > 1 else [],
        compiler_params=_cparams(("arbitrary", "arbitrary", "arbitrary")),
        name=name,
    )(*xs, w, res, mod)


NA_GROUP_ROWS = 4
NA_GROUP_Q = NA_GROUP_ROWS * GRID_W
NA_GROUP_KROWS = NA_GROUP_ROWS + NA_WIN_R
NA_GROUP_K = NA_GROUP_KROWS * GRID_W
NA_GROUPS = GRID_ROWS // NA_GROUP_ROWS
NA_PREP_ROWS = 256
NA_SCORE_LOOKAHEAD = 1


def _na_window_row0(rq):
    return min(max(rq - NA_WIN_R // 2, 0), GRID_ROWS - NA_WIN_R)


def _na_group_key_row0(r):
    return min(max(r - NA_WIN_R // 2, 0), GRID_ROWS - NA_GROUP_KROWS)


def _na_group_signature(r):
    k0 = _na_group_key_row0(r)
    return (k0 - r,) + tuple(_na_window_row0(r + j) - k0 for j in range(NA_GROUP_ROWS))


NA_GROUP_KINDS = tuple(sorted({_na_group_signature(r): r for r in reversed(range(0, GRID_ROWS, NA_GROUP_ROWS))}.values()))


def _na_group_kind(r):
    return [_na_group_signature(k) for k in NA_GROUP_KINDS].index(_na_group_signature(r))


def na_bias_table(rpb):
    w = GRID_W
    col = np.arange(w)
    c0 = np.clip(col - NA_WIN_C // 2, 0, w - NA_WIN_C)
    in_win = (col[None, :] >= c0[:, None]) & (col[None, :] < c0[:, None] + NA_WIN_C)
    padded = jnp.pad(rpb.astype(F32), ((0, 0), (0, 0), (w, w)))
    toep = jnp.stack([padded[:, :, NA_WIN_C - 1 - qc + w:NA_WIN_C - 1 - qc + 2 * w] for qc in range(w)], axis=2)
    toep = jnp.where(in_win[None, None], toep * LOG2E, NEG_BIG)
    masked = jnp.full((NA_HEADS, w, w), NEG_BIG, F32)
    kinds = []
    for r in NA_GROUP_KINDS:
        q_rows = []
        for rq in range(r, r + NA_GROUP_ROWS):
            r0 = _na_window_row0(rq)
            blocks = []
            for kr in range(_na_group_key_row0(r), _na_group_key_row0(r) + NA_GROUP_KROWS):
                blocks.append(toep[:, kr - rq + NA_WIN_R - 1] if r0 <= kr < r0 + NA_WIN_R else masked)
            q_rows.append(jnp.concatenate(blocks, axis=2))
        kinds.append(jnp.concatenate(q_rows, axis=1))
    return jnp.stack(kinds, axis=1)


def _na_body(flag_ref, q_ref, k_ref, v_ref, qc_ref, kc_ref, vc_ref, qg_ref, kg_ref, bias_ref,
             o_ref, oc_ref, qn_ref, knt_ref, vn_ref, kcnt_ref, vcn_ref):
    scale = HEAD_DIM ** -0.5 * LOG2E
    qg = qg_ref[...]
    kg = kg_ref[...]

    def with_ones(v):
        return jnp.concatenate([v.astype(BF16), jnp.ones(v.shape, BF16)], axis=1)

    def normalised(o_ext):
        return (o_ext[:, :HEAD_DIM] / o_ext[:, HEAD_DIM:]).astype(BF16)

    for i in range(SEQ // NA_PREP_ROWS):
        rows = slice(i * NA_PREP_ROWS, (i + 1) * NA_PREP_ROWS)
        qn_ref[rows, :] = (_rms_rows(q_ref[rows, :], qg) * scale).astype(BF16)
        knt_ref[:, rows] = _rms_rows(k_ref[rows, :], kg).T.astype(BF16)
        vn_ref[rows, :] = with_ones(v_ref[rows, :])
    kcnt = _rms_rows(kc_ref[...], kg).T.astype(BF16)
    vcn = with_ones(vc_ref[...])
    kcnt_ref[...] = kcnt
    vcn_ref[...] = vcn

    def group_rows(gi):
        k0 = _na_group_key_row0(gi * NA_GROUP_ROWS) * GRID_W
        return slice(gi * NA_GROUP_Q, (gi + 1) * NA_GROUP_Q), slice(k0, k0 + NA_GROUP_K)

    def scores(gi):
        qrows, krows = group_rows(gi)
        qb = qn_ref[qrows, :]
        return _dot(qb, knt_ref[:, krows]), _dot(qb, kcnt_ref[...])

    def attend(shifted):
        pending = [scores(gi) for gi in range(NA_SCORE_LOOKAHEAD)]
        for gi in range(NA_GROUPS):
            s1, s2 = pending.pop(0)
            if gi + NA_SCORE_LOOKAHEAD < NA_GROUPS:
                pending.append(scores(gi + NA_SCORE_LOOKAHEAD))
            qrows, krows = group_rows(gi)
            s1 = s1 + bias_ref[0, _na_group_kind(gi * NA_GROUP_ROWS)]
            if shifted:
                m = jnp.maximum(jnp.max(s1, axis=-1, keepdims=True), jnp.max(s2, axis=-1, keepdims=True))
                s1, s2 = s1 - m, s2 - m
            p1 = jnp.exp2(s1).astype(BF16)
            p2 = jnp.exp2(s2).astype(BF16)
            o_ref[qrows, :] = normalised(_dot(p1, vn_ref[krows, :]) + _dot(p2, vcn_ref[...]))

        s = _dot((_rms_rows(qc_ref[...], qg) * scale).astype(BF16), kcnt)
        if shifted:
            s = s - jnp.max(s, axis=-1, keepdims=True)
        oc_ref[...] = normalised(_dot(jnp.exp2(s).astype(BF16), vcn))

    scores_bounded = flag_ref[0, 0] > 0.5
    pl.when(scores_bounded)(lambda: attend(False))
    pl.when(jnp.logical_not(scores_bounded))(lambda: attend(True))


def na_attention(p_lat, p_ctx, q_g, k_g, rpb):
    hd = HEAD_DIM
    lat = lambda seg: pl.BlockSpec((SEQ, hd), lambda h, b: (b, seg * NA_HEADS + h))
    ctx = lambda seg: pl.BlockSpec((CTX_LEN, hd), lambda h, b: (b, seg * NA_HEADS + h))
    bias = na_bias_table(rpb)
    score_bound = LOG2E * (hd ** 0.5 * jnp.max(jnp.abs(q_g)) * jnp.max(jnp.abs(k_g)) + jnp.max(jnp.abs(rpb)))
    flag = (score_bound < SOFTMAX_NOSHIFT_LOG2).astype(F32).reshape(1, 1)
    return pl.pallas_call(
        _na_body,
        out_shape=(jax.ShapeDtypeStruct((N_LAT, NA_WIDTH), BF16),
                   jax.ShapeDtypeStruct((N_CTX, NA_WIDTH), BF16)),
        grid=(NA_HEADS, BATCH),
        in_specs=[pl.BlockSpec(memory_space=pltpu.SMEM),
                  lat(0), lat(1), lat(2), ctx(0), ctx(1), ctx(2),
                  pl.BlockSpec((1, hd), lambda h, b: (0, 0)),
                  pl.BlockSpec((1, hd), lambda h, b: (0, 0)),
                  pl.BlockSpec((1, len(NA_GROUP_KINDS), NA_GROUP_Q, NA_GROUP_K), lambda h, b: (h, 0, 0, 0))],
        out_specs=(pl.BlockSpec((SEQ, hd), lambda h, b: (b, h)),
                   pl.BlockSpec((CTX_LEN, hd), lambda h, b: (b, h))),
        scratch_shapes=[pltpu.VMEM((SEQ, hd), BF16), pltpu.VMEM((hd, SEQ), BF16), pltpu.VMEM((SEQ, 2 * hd), BF16),
                        pltpu.VMEM((hd, CTX_LEN), BF16), pltpu.VMEM((CTX_LEN, 2 * hd), BF16)],
        compiler_params=_cparams(("arbitrary", "arbitrary")),
        name="na_attention",
    )(flag, p_lat, p_lat, p_lat, p_ctx, p_ctx, p_ctx, q_g.reshape(1, hd), k_g.reshape(1, hd), bias)


HG_BLK = 128
HG_CPB = HG_BLK // HG_CHUNK
HG_CTX_BLKS = CTX_LEN // HG_BLK
HG_LAT_BLKS = SEQ // HG_BLK
HG_BLKS = HG_CTX_BLKS + HG_LAT_BLKS
HG_SCAN_STEPS = 3


def _hg_ref_rows(reverse):
    b, c = HG_BLK, HG_CHUNK
    if not reverse:
        return {c: [None] + [c * i - 1 for i in range(1, b // c)],
                32: [32 * j + 15 for j in range(b // 32)], 64: [64 * j + 31 for j in range(b // 64)],
                128: [63], "end": [b - 1]}
    return {c: [c * i + c for i in range(b // c - 1)] + [None],
            32: [32 * j + 16 for j in range(b // 32)], 64: [64 * j + 32 for j in range(b // 64)],
            128: [64], "end": [0]}


def _hg_level_codes(reverse):
    t = lax.broadcasted_iota(jnp.int32, (HG_BLK, HG_BLK), 0)
    s = lax.broadcasted_iota(jnp.int32, (HG_BLK, HG_BLK), 1)
    if reverse:
        t, s = s, t
    code = jnp.where((t >> 6) > (s >> 6), 4, 0)
    code = jnp.where(((t >> 6) == (s >> 6)) & ((t >> 5) > (s >> 5)), 3, code)
    code = jnp.where(((t >> 5) == (s >> 5)) & ((t >> 4) > (s >> 4)), 2, code)
    return jnp.where(((t >> 4) == (s >> 4)) & (s <= t), 1, code)


def _hg_body(q_ref, zf_ref, zb_ref, i_ref, g_ref, qc_ref, zfc_ref, zbc_ref, ic_ref, gc_ref,
             lbf_ref, lbb_ref, gn_ref, o_ref, oc_ref,
             qd_s, ka_s, q64_s, k64_s, q128_s, k128_s, qb_s, ke_s, gb_s, v_s, o_s, st_s, x_s, code_s):
    lbs = (lbf_ref[...], lbb_ref[...])
    ri = lax.broadcasted_iota(jnp.int32, (HG_BLK, HG_BLK), 0)
    ci = lax.broadcasted_iota(jnp.int32, (HG_BLK, HG_BLK), 1)
    tri = jnp.where(ci <= ri, 1.0, 0.0).astype(BF16)
    for d in range(2):
        code_s[d] = _hg_level_codes(d == 1)

    def prefix_rows(x):
        hi = x.astype(BF16)
        r1 = x - hi.astype(F32)
        mid = r1.astype(BF16)
        lo = (r1 - mid.astype(F32)).astype(BF16)
        y = _dot(tri, jnp.concatenate([hi, mid, lo], axis=1))
        return y[:, :HEAD_DIM] + y[:, HEAD_DIM:2 * HEAD_DIM] + y[:, 2 * HEAD_DIM:]

    def ref_rows(d, rows):
        group = HG_BLK // len(rows)
        parts = [jnp.zeros((group, HEAD_DIM), F32) if r is None else
                 jnp.broadcast_to(x_s[d, r:r + 1, :], (group, HEAD_DIM)) for r in rows]
        return parts[0] if len(parts) == 1 else jnp.concatenate(parts, axis=0)

    def prep_block(blk, q, zf, zb, v):
        qs = q * jax.nn.sigmoid(q)
        v_s[blk] = v.astype(BF16)
        for d, z in enumerate((zf, zb)):
            f = lbs[d] + (1.0 - lbs[d]) * jax.nn.sigmoid(z)
            logf = jnp.log2(f)
            k = 1.0 - f
            x = prefix_rows(logf)
            x_s[d] = x
            if d == 1:
                x = ref_rows(d, [HG_BLK - 1]) - x + logf
                x_s[d] = x
            rows = _hg_ref_rows(d == 1)
            r16, r32, r64, r128 = (ref_rows(d, rows[g]) for g in (HG_CHUNK, 32, 64, 128))
            x_end = ref_rows(d, rows["end"])
            e32 = jnp.exp2(-jnp.abs(x - r32))
            e64 = jnp.exp2(-jnp.abs(x - r64))
            e128 = jnp.exp2(-jnp.abs(x - r128))
            qd_s[d, blk] = (qs * jnp.exp2(x - r16)).astype(BF16)
            ka_s[d, blk] = jnp.concatenate([(k * jnp.exp2(r16 - x)).T, (k * e32).T], axis=1).astype(BF16)
            q64_s[d, blk] = (qs * e64).astype(BF16)
            k64_s[d, blk] = (k * e64).T.astype(BF16)
            q128_s[d, blk] = (qs * e128).astype(BF16)
            k128_s[d, blk] = (k * e128).T.astype(BF16)
            qb_s[d, blk] = (qs * jnp.exp2(x)).astype(BF16)
            ke_s[d, blk] = (k * jnp.exp2(x_end - x)).T.astype(BF16)
            gb_s[d, blk] = jnp.exp2(x_end).T

    for cb in range(HG_CTX_BLKS):
        rows = slice(cb * HG_BLK, (cb + 1) * HG_BLK)
        prep_block(cb, qc_ref[rows, :], zfc_ref[rows, :], zbc_ref[rows, :], ic_ref[rows, :])

    def prep_lat(n, _):
        rows = pl.ds(pl.multiple_of(n * HG_BLK, HG_BLK), HG_BLK)
        prep_block(HG_CTX_BLKS + n, q_ref[rows, :], zf_ref[rows, :], zb_ref[rows, :], i_ref[rows, :])
        return 0

    lax.fori_loop(0, HG_LAT_BLKS, prep_lat, 0, unroll=2)

    st_s[...] = jnp.zeros_like(st_s)

    def scan_steps(i, _):
        chains = []
        for n in [i * HG_SCAN_STEPS + j for j in range(HG_SCAN_STEPS)]:
            chains.append((0, n))
            chains.append((1, jnp.where(n < HG_CTX_BLKS, HG_CTX_BLKS - 1 - n, HG_BLKS + HG_CTX_BLKS - 1 - n)))
        ready = []
        for d, blk in chains:
            a1 = _dot(qd_s[d, blk], ka_s[d, blk])
            a64 = _dot(q64_s[d, blk], k64_s[d, blk])
            a128 = _dot(q128_s[d, blk], k128_s[d, blk])
            v = v_s[blk]
            inc = _dot(ke_s[d, blk], v)
            code = code_s[d]
            att = jnp.where(code == 1, a1[:, :HG_BLK],
                            jnp.where(code == 2, a1[:, HG_BLK:],
                                      jnp.where(code == 3, a64, jnp.where(code == 4, a128, 0.0))))
            ready.append((jnp.concatenate([att.astype(BF16), qb_s[d, blk]], axis=1), v, inc))
        for (d, blk), (lhs, v, inc) in zip(chains, ready):
            state = st_s[d]
            o_s[d, blk] = _dot(lhs, jnp.concatenate([v, state.astype(BF16)], axis=0))
            st_s[d] = gb_s[d, blk] * state + inc
        return 0

    lax.fori_loop(0, HG_BLKS // HG_SCAN_STEPS, scan_steps, 0)

    gn = gn_ref[...]

    def finish(blk, gate):
        o = o_s[0, blk] + o_s[1, blk]
        return (_rms_rows(o, gn) * (gate * jax.nn.sigmoid(gate))).astype(BF16)

    for cb in range(HG_CTX_BLKS):
        rows = slice(cb * HG_BLK, (cb + 1) * HG_BLK)
        oc_ref[rows, :] = finish(cb, gc_ref[rows, :])

    def fin_lat(n, _):
        rows = pl.ds(pl.multiple_of(n * HG_BLK, HG_BLK), HG_BLK)
        o_ref[rows, :] = finish(HG_CTX_BLKS + n, g_ref[rows, :])
        return 0

    lax.fori_loop(0, HG_LAT_BLKS, fin_lat, 0, unroll=4)


def hgrn2(p_lat, p_ctx, lb_fwd, lb_bwd, gn_g):
    hd = HEAD_DIM
    lat = lambda seg: pl.BlockSpec((SEQ, hd), lambda b, h: (b, seg * HG_HEADS + h))
    ctx = lambda seg: pl.BlockSpec((CTX_LEN, hd), lambda b, h: (b, seg * HG_HEADS + h))
    head_vec = pl.BlockSpec((1, hd), lambda b, h: (0, h))
    dir_bf = lambda width=hd: pltpu.VMEM((2, HG_BLKS, HG_BLK, width), BF16)
    dir_f32 = lambda: pltpu.VMEM((2, HG_BLKS, HG_BLK, hd), F32)
    return pl.pallas_call(
        _hg_body,
        out_shape=(jax.ShapeDtypeStruct((N_LAT, HG_WIDTH), BF16),
                   jax.ShapeDtypeStruct((N_CTX, HG_WIDTH), BF16)),
        grid=(BATCH, HG_HEADS),
        in_specs=[lat(3), lat(4), lat(5), lat(6), lat(7), ctx(3), ctx(4), ctx(5), ctx(6), ctx(7),
                  head_vec, head_vec, pl.BlockSpec((1, hd), lambda b, h: (0, 0))],
        out_specs=(pl.BlockSpec((SEQ, hd), lambda b, h: (b, h)),
                   pl.BlockSpec((CTX_LEN, hd), lambda b, h: (b, h))),
        scratch_shapes=[dir_bf(), dir_bf(2 * hd), dir_bf(), dir_bf(), dir_bf(), dir_bf(), dir_bf(), dir_bf(),
                        dir_f32(), pltpu.VMEM((HG_BLKS, HG_BLK, hd), BF16), dir_f32(),
                        pltpu.VMEM((2, hd, hd), F32), pltpu.VMEM((2, HG_BLK, hd), F32),
                        pltpu.VMEM((2, HG_BLK, HG_BLK), jnp.int32)],
        compiler_params=_cparams(("arbitrary", "arbitrary")),
        name="hgrn2",
    )(p_lat, p_lat, p_lat, p_lat, p_lat, p_ctx, p_ctx, p_ctx, p_ctx, p_ctx,
      lb_fwd.reshape(1, -1), lb_bwd.reshape(1, -1), gn_g.reshape(1, hd))


DF_TQ = 512
DF_NK = SEQ + CTX_LEN
DF_TK = 768
DF_KTILES = DF_NK // DF_TK
DF_SCORE_LOOKAHEAD = 2


def _diff_body(lam_ref, q_ref, k_ref, kc_ref, v_ref, vc_ref, cos_ref, sin_ref, cost_ref, sint_ref,
               qgt_ref, kg_ref, sg_ref, o_ref, qt_s, k0_s, k1_s, vt_s, e_s, *, out_scale):
    dk = DIFF_DK
    kg = kg_ref[...]

    @pl.when(pl.program_id(2) == 0)
    def _():
        ri = lax.broadcasted_iota(jnp.int32, (dk, dk), 0)
        ci = lax.broadcasted_iota(jnp.int32, (dk, dk), 1)
        ones = jnp.ones((dk, dk), BF16)
        swap_halves = jnp.where(ri == ((ci + dk // 2) & (dk - 1)), 1.0, 0.0).astype(BF16)

        def times(x, m):
            hi = x.astype(BF16)
            lo = (x - hi.astype(F32)).astype(BF16)
            return _dot(hi, m) + _dot(lo, m)

        def key_rows(x, c2, s2):
            xn = x * lax.rsqrt(times(x * x, ones) * (1.0 / dk) + EPS) * kg
            return (xn * c2 + times(xn, swap_halves) * s2).astype(BF16)

        def prep(i, _):
            rows = pl.ds(pl.multiple_of(i * CTX_LEN, CTX_LEN), CTX_LEN)
            c2 = cos_ref[rows, :]
            s2 = sin_ref[rows, :]
            kk = k_ref[rows, :]
            k0_s[rows, :] = key_rows(kk[:, :dk], c2, s2)
            k1_s[rows, :] = key_rows(kk[:, dk:], c2, s2)
            vt_s[i] = v_ref[rows, :].T.astype(BF16)
            qq_t = q_ref[rows, :].T
            ct = cost_ref[i]
            st = sint_ref[i]
            for j in range(2):
                xt = qq_t[j * dk:(j + 1) * dk, :]
                inv = lax.rsqrt(jnp.mean(xt * xt, axis=0, keepdims=True) + EPS)
                xn = xt * inv * qgt_ref[...]
                swapped = jnp.concatenate([xn[dk // 2:, :], xn[:dk // 2, :]], axis=0)
                qt_s[j, i] = (xn * ct + swapped * st).astype(BF16)
            return 0

        lax.fori_loop(0, SEQ // CTX_LEN, prep, 0, unroll=4)
        kc = kc_ref[...]
        k0_s[SEQ:, :] = _rms_rows(kc[:, :dk], kg).astype(BF16)
        k1_s[SEQ:, :] = _rms_rows(kc[:, dk:], kg).astype(BF16)
        vt_s[SEQ // CTX_LEN] = vc_ref[...].T.astype(BF16)

    lam = lam_ref[0, 0]
    qpt = DF_TQ // CTX_LEN
    q0 = pl.program_id(2) * qpt
    qt = [jnp.concatenate([qt_s[i, q0 + j] for j in range(qpt)], axis=1) for i in range(2)]
    k_s = (k0_s, k1_s)

    cpt = DF_TK // CTX_LEN
    jobs = [(t, i) for t in range(DF_KTILES) for i in range(2)]

    def scores(job):
        t, i = job
        return _dot(k_s[i][t * DF_TK:(t + 1) * DF_TK, :], qt[i])

    def attend(shifted):
        m = [None, None]
        l = [None, None]
        m_tile = [[None] * DF_KTILES for _ in range(2)]
        pending = [scores(job) for job in jobs[:DF_SCORE_LOOKAHEAD]]
        for n, (t, i) in enumerate(jobs):
            s = pending.pop(0)
            if n + DF_SCORE_LOOKAHEAD < len(jobs):
                pending.append(scores(jobs[n + DF_SCORE_LOOKAHEAD]))
            if shifted:
                tile_max = jnp.max(s, axis=0, keepdims=True)
                m_new = tile_max if t == 0 else jnp.maximum(m[i], tile_max)
                e = jnp.exp2(s - m_new)
            else:
                e = jnp.exp2(s)
            e_s[i, t] = e.astype(BF16)
            tile_sum = jnp.sum(e, axis=0, keepdims=True)
            if t == 0:
                l[i] = tile_sum
            elif shifted:
                l[i] = jnp.exp2(m[i] - m_new) * l[i] + tile_sum
            else:
                l[i] = l[i] + tile_sum
            if shifted:
                m[i] = m_new
                m_tile[i][t] = m_new

        weight = (1.0 / l[0], lam / l[1])
        acc = None
        for t in range(DF_KTILES):
            if shifted:
                f0, f1 = ((jnp.exp2(m_tile[i][t] - m[i]) * weight[i]).astype(BF16) for i in range(2))
            else:
                f0, f1 = (w.astype(BF16) for w in weight)
            a = e_s[0, t] * f0 - e_s[1, t] * f1
            vt = jnp.concatenate([vt_s[t * cpt + j] for j in range(cpt)], axis=1)
            d = _dot(vt, a)
            acc = d if acc is None else acc + d
        o_ref[...] = (_rms_rows(acc.T, sg_ref[...]) * out_scale).astype(o_ref.dtype)

    scores_bounded = lam_ref[0, 1] > 0.5
    pl.when(scores_bounded)(lambda: attend(False))
    pl.when(jnp.logical_not(scores_bounded))(lambda: attend(True))


def diff_attention(p_lat, p_ctx, lam, cos2, sin2, q_g, k_g, subln_g, out_scale):
    nq = SEQ // DF_TQ
    w = 2 * DIFF_DK
    kcol0 = D_MODEL // w
    vcol0 = 2 * D_MODEL // w
    vec = lambda n: pl.BlockSpec((1, n), lambda b, h, i: (0, 0))
    n_chunks = SEQ // CTX_LEN
    chunked_t = lambda tab: tab.reshape(n_chunks, CTX_LEN, DIFF_DK).transpose(0, 2, 1)
    q_gain_t = jnp.broadcast_to((q_g.astype(F32) * (DIFF_DK ** -0.5 * LOG2E))[:, None], (DIFF_DK, CTX_LEN))
    score_bound = LOG2E * DIFF_DK ** 0.5 * jnp.max(jnp.abs(q_g)) * jnp.max(jnp.abs(k_g))
    scalars = jnp.stack([lam.reshape(()), (score_bound < SOFTMAX_NOSHIFT_LOG2).astype(F32)]).reshape(1, 2)
    return pl.pallas_call(
        functools.partial(_diff_body, out_scale=out_scale),
        out_shape=jax.ShapeDtypeStruct((N_LAT, D_MODEL), BF16),
        grid=(BATCH, DIFF_HEADS, nq),
        in_specs=[
            pl.BlockSpec(memory_space=pltpu.SMEM),
            pl.BlockSpec((SEQ, w), lambda b, h, i: (b, h)),
            pl.BlockSpec((SEQ, w), lambda b, h, i: (b, kcol0 + h)),
            pl.BlockSpec((CTX_LEN, w), lambda b, h, i: (b, kcol0 + h)),
            pl.BlockSpec((SEQ, w), lambda b, h, i: (b, vcol0 + h)),
            pl.BlockSpec((CTX_LEN, w), lambda b, h, i: (b, vcol0 + h)),
            pl.BlockSpec((SEQ, DIFF_DK), lambda b, h, i: (0, 0)),
            pl.BlockSpec((SEQ, DIFF_DK), lambda b, h, i: (0, 0)),
            pl.BlockSpec((n_chunks, DIFF_DK, CTX_LEN), lambda b, h, i: (0, 0, 0)),
            pl.BlockSpec((n_chunks, DIFF_DK, CTX_LEN), lambda b, h, i: (0, 0, 0)),
            pl.BlockSpec((DIFF_DK, CTX_LEN), lambda b, h, i: (0, 0)),
            vec(DIFF_DK), vec(DIFF_DV),
        ],
        out_specs=pl.BlockSpec((DF_TQ, w), lambda b, h, i: (b * nq + i, h)),
        scratch_shapes=[pltpu.VMEM((2, SEQ // CTX_LEN, DIFF_DK, CTX_LEN), BF16),
                        pltpu.VMEM((DF_NK, DIFF_DK), BF16), pltpu.VMEM((DF_NK, DIFF_DK), BF16),
                        pltpu.VMEM((DF_NK // CTX_LEN, DIFF_DV, CTX_LEN), BF16),
                        pltpu.VMEM((2, DF_KTILES, DF_TK, DF_TQ), BF16)],
        compiler_params=_cparams(("arbitrary", "arbitrary", "arbitrary")),
        name="diff_attention",
    )(scalars, p_lat, p_lat, p_ctx, p_lat, p_ctx, cos2, sin2, chunked_t(cos2), chunked_t(sin2), q_gain_t,
      k_g.reshape(1, DIFF_DK), subln_g.reshape(1, DIFF_DV))


def _rope_tables():
    t = jnp.arange(SEQ)
    row = (t // GRID_W).astype(F32)
    col = (t % GRID_W).astype(F32)
    n_freq = DIFF_DK // 4
    inv = ROPE_THETA ** (-jnp.arange(n_freq, dtype=F32) / n_freq)
    ang = jnp.concatenate([row[:, None] * inv, col[:, None] * inv], axis=-1)
    cos, sin = jnp.cos(ang), jnp.sin(ang)
    return jnp.concatenate([cos, cos], axis=-1), jnp.concatenate([-sin, sin], axis=-1)


def _mlp_ctx(x, mod, g, w1_f32, w2_f32, layer):
    a, w1 = norm_matmul(x, g, mod, w1_f32, layer, shift_idx=3, is_ctx=True, out_dtype=BF16, relu2=True,
                        cast_weights=True, name="mlp_up_ctx")
    x, w2 = res_matmul([a], w2_f32, layer, x, mod, gate_idx=5, is_ctx=True, cast_weights=True, name="mlp_down_ctx")
    return x, w1, w2


def _mlp(x, mod, g, w1, w2):
    a = norm_matmul(x, g, mod, w1, 0, shift_idx=3, is_ctx=False, out_dtype=BF16, relu2=True, name="mlp_up")
    return res_matmul([a], w2, 0, x, mod, gate_idx=5, is_ctx=False, name="mlp_down")


def _bf16_layer(w, idx):
    return w[idx:idx + 1].astype(BF16)


def kernel(x, c, ctx, c_ctx, ada_w, ada_b, norm_mix_g, norm_mlp_g, mlp_w1, mlp_w2, ev_w_in, ev_w_out, na_q_g, na_k_g, na_rpb, hg_lb_logits, hg_gnorm_g, od_w_in, od_w_out, df_q_g, df_k_g, df_lambda, df_subln_g):
    xl = x.reshape(N_LAT, D_MODEL)
    xc = ctx.reshape(N_CTX, D_MODEL)
    cond = jnp.concatenate([c, c_ctx[None], jnp.zeros((MOD_ROWS - BATCH - 1, D_MODEL), F32)], axis=0)
    mods = ada_table(cond, ada_w, ada_b)
    lb_all = jnp.cumsum(jax.nn.softmax(hg_lb_logits.astype(F32), axis=1), axis=1)
    cos2, sin2 = _rope_tables()

    for l in range(DEPTH):
        need_ctx = l < DEPTH - 1
        mod = mods[l]
        g_mix = norm_mix_g[l]
        if l % 2 == 0:
            e = l // 2
            p_ctx, w_in = norm_matmul(xc, g_mix, mod, ev_w_in, e, shift_idx=0, is_ctx=True, out_dtype=F32,
                                      cast_weights=True, name="even_in_ctx")
            p_lat = norm_matmul(xl, g_mix, mod, w_in, 0, shift_idx=0, is_ctx=False, out_dtype=F32, name="even_in")
            na_lat, na_ctx = na_attention(p_lat, p_ctx, na_q_g[e], na_k_g[e], na_rpb[e])
            hg_lat, hg_ctx = hgrn2(p_lat, p_ctx, lb_all[0, l], lb_all[1, l], hg_gnorm_g[e])
            if need_ctx:
                xc, w_out = res_matmul([na_ctx, hg_ctx], ev_w_out, e, xc, mod, gate_idx=2, is_ctx=True,
                                       cast_weights=True, name="mix_out_ctx")
            else:
                w_out = _bf16_layer(ev_w_out, e)
            xl = res_matmul([na_lat, hg_lat], w_out, 0, xl, mod, gate_idx=2, is_ctx=False, name="mix_out")
        else:
            assert not need_ctx, "an odd layer followed by another layer needs context outputs"
            o = l // 2
            p_ctx, w_in = norm_matmul(xc, g_mix, mod, od_w_in, o, shift_idx=0, is_ctx=True, out_dtype=F32,
                                      cast_weights=True, name="odd_in_ctx")
            p_lat = norm_matmul(xl, g_mix, mod, w_in, 0, shift_idx=0, is_ctx=False, out_dtype=F32, name="odd_in")
            lam_init = 0.8 - 0.6 * math.exp(-0.3 * l)
            lp = df_lambda[o].astype(F32)
            lam = jnp.exp(jnp.sum(lp[0] * lp[1])) - jnp.exp(jnp.sum(lp[2] * lp[3])) + lam_init
            om = diff_attention(p_lat, p_ctx, lam, cos2, sin2, df_q_g[o], df_k_g[o], df_subln_g[o],
                                1.0 - lam_init)
            xl = res_matmul([om], _bf16_layer(od_w_out, o), 0, xl, mod, gate_idx=2, is_ctx=False, name="mix_out")
        if need_ctx:
            xc, w1, w2 = _mlp_ctx(xc, mod, norm_mlp_g[l], mlp_w1, mlp_w2, l)
        else:
            w1, w2 = _bf16_layer(mlp_w1, l), _bf16_layer(mlp_w2, l)
        xl = _mlp(xl, mod, norm_mlp_g[l], w1, w2)
    return xl.reshape(BATCH, SEQ, D_MODEL)
```

```python
import functools
import math

import jax
import jax.numpy as jnp
import numpy as np
from jax import lax
from jax.experimental import pallas as pl
from jax.experimental.pallas import tpu as pltpu

F32 = jnp.float32
BF16 = jnp.bfloat16

D_MODEL = 2048
BATCH = 4
SEQ = 2048
DEPTH = 2
GRID_W = 64
GRID_ROWS = SEQ // GRID_W
CTX_LEN = 256
HEAD_DIM = 128
NA_HEADS = 8
NA_WIN_R = 8
NA_WIN_C = 16
NA_WIDTH = NA_HEADS * HEAD_DIM
HG_HEADS = 8
HG_CHUNK = 16
HG_WIDTH = HG_HEADS * HEAD_DIM
EVEN_IN = 8 * NA_WIDTH
DIFF_HEADS = 8
DIFF_DK = HEAD_DIM
DIFF_DV = 2 * HEAD_DIM
ODD_IN = 3 * D_MODEL
D_FF = 4 * D_MODEL
N_MOD = 6
ROPE_THETA = 10000.0
EPS = 1e-6

N_LAT = BATCH * SEQ
N_CTX = BATCH * CTX_LEN
MOD_ROWS = 8
CTX_MOD_ROW = BATCH
NEG_BIG = -1e30
LOG2E = math.log2(math.e)
SOFTMAX_NOSHIFT_LOG2 = 100.0

VMEM_LIMIT_V7X = 56 * 1024 * 1024
MM_TILE = 1024
MM_TK = 2048
NORM_SLICES = 4


def _cparams(sem, vmem=VMEM_LIMIT_V7X):
    return pltpu.CompilerParams(dimension_semantics=sem, vmem_limit_bytes=vmem)


def _dot(a, b):
    return jnp.dot(a, b, preferred_element_type=F32)


def _dot_nt(a, b):
    return lax.dot_general(a, b, (((1,), (1,)), ((), ())), preferred_element_type=F32)


def _rms_rows(x, g):
    return x * lax.rsqrt(jnp.mean(x * x, axis=-1, keepdims=True) + EPS) * g


def _mod_row_of_tile(is_ctx, tile_rows):
    if is_ctx:
        return lambda i: CTX_MOD_ROW
    return lambda i: (i * tile_rows) // SEQ


def _ada_body(c_ref, w_ref, b_ref, o_ref):
    c = c_ref[...]
    s = (c * jax.nn.sigmoid(c)).astype(BF16)
    o_ref[0] = _dot(s, w_ref[0].astype(BF16)) + b_ref[0]


def ada_table(cond, ada_w, ada_b):
    tn = 1024
    n = N_MOD * D_MODEL
    out = pl.pallas_call(
        _ada_body,
        out_shape=jax.ShapeDtypeStruct((DEPTH, MOD_ROWS, n), F32),
        grid=(DEPTH, n // tn),
        in_specs=[
            pl.BlockSpec((MOD_ROWS, D_MODEL), lambda l, j: (0, 0)),
            pl.BlockSpec((1, D_MODEL, tn), lambda l, j: (l, 0, j)),
            pl.BlockSpec((1, 1, tn), lambda l, j: (l, 0, j)),
        ],
        out_specs=pl.BlockSpec((1, MOD_ROWS, tn), lambda l, j: (l, 0, j)),
        compiler_params=_cparams(("arbitrary", "arbitrary")),
        name="ada_table",
    )(cond, ada_w, ada_b.reshape(DEPTH, 1, n))
    return out.reshape(DEPTH, MOD_ROWS, N_MOD, D_MODEL)


def _norm_mm_body(x_ref, g_ref, mod_ref, w_ref, *out_and_scratch, shift_idx, relu2, cast_w):
    o_ref, h_ref = out_and_scratch[0], out_and_scratch[-1]
    first_col_tile = pl.program_id(1) == 0

    if cast_w:
        wb_ref = out_and_scratch[1]
        wb_ref[...] = w_ref[...].astype(BF16)
    else:
        wb_ref = w_ref

    def project(rows):
        acc = _dot(h_ref[rows, :], wb_ref[...])
        if relu2:
            acc = jnp.maximum(acc, 0.0)
            acc = acc * acc
        o_ref[rows, :] = acc.astype(o_ref.dtype)

    @pl.when(first_col_tile)
    def _():
        shift = mod_ref[0, shift_idx:shift_idx + 1, :]
        scale = mod_ref[0, shift_idx + 1:shift_idx + 2, :]
        n_rows = h_ref.shape[0]
        for r0 in range(0, n_rows, n_rows // NORM_SLICES):
            rows = slice(r0, r0 + n_rows // NORM_SLICES)
            h_ref[rows, :] = (_rms_rows(x_ref[rows, :], g_ref[...]) * (1.0 + scale) + shift).astype(BF16)
            project(rows)

    @pl.when(jnp.logical_not(first_col_tile))
    def _():
        project(slice(None))


def norm_matmul(x, g, mod, w, w_idx, *, shift_idx, is_ctx, out_dtype, relu2=False, col_block0=0, n_cols=None,
                cast_weights=False, name):
    n_rows = x.shape[0]
    tm = MM_TILE
    if cast_weights:
        assert n_rows == tm and col_block0 == 0 and n_cols is None
        tn = MM_TILE
    else:
        tn = MM_TILE * (jnp.dtype(F32).itemsize // jnp.dtype(out_dtype).itemsize)
    col_block0 = col_block0 * MM_TILE // tn
    n_cols = w.shape[2] if n_cols is None else n_cols
    mod_row = _mod_row_of_tile(is_ctx, tm)
    out_shape = jax.ShapeDtypeStruct((n_rows, n_cols), out_dtype)
    out_specs = pl.BlockSpec((tm, tn), lambda i, j: (i, j))
    if cast_weights:
        out_shape = (out_shape, jax.ShapeDtypeStruct((1, D_MODEL, n_cols), BF16))
        out_specs = (out_specs, pl.BlockSpec((None, D_MODEL, tn), lambda i, j: (0, 0, j)))
    return pl.pallas_call(
        functools.partial(_norm_mm_body, shift_idx=shift_idx, relu2=relu2, cast_w=cast_weights),
        out_shape=out_shape,
        grid=(n_rows // tm, n_cols // tn),
        in_specs=[
            pl.BlockSpec((tm, D_MODEL), lambda i, j: (i, 0), pipeline_mode=pl.Buffered(1) if cast_weights else None),
            pl.BlockSpec((1, D_MODEL), lambda i, j: (0, 0)),
            pl.BlockSpec((1, N_MOD, D_MODEL), lambda i, j: (mod_row(i), 0, 0)),
            pl.BlockSpec((None, D_MODEL, tn), lambda i, j: (w_idx, 0, col_block0 + j)),
        ],
        out_specs=out_specs,
        scratch_shapes=[pltpu.VMEM((tm, D_MODEL), BF16)],
        compiler_params=_cparams(("arbitrary", "arbitrary")),
        name=name,
    )(x, g.reshape(1, D_MODEL), mod, w)


def _res_mm_body(*refs, n_x, nk, gate_idx, cast_w):
    x_refs = refs[:n_x]
    w_ref, res_ref, mod_ref, o_ref = refs[n_x:n_x + 4]
    scratch = refs[n_x + 4:]
    if cast_w:
        wb_ref, scratch = scratch[0], scratch[1:]

    def partial_product():
        acc = None
        k0 = 0
        for x_ref in x_refs:
            kw = x_ref.shape[1]
            w = w_ref[k0:k0 + kw, :]
            if cast_w:
                w = w.astype(BF16)
                wb_ref[k0:k0 + kw, :] = w
            d = _dot(x_ref[...], w)
            acc = d if acc is None else acc + d
            k0 += kw
        return acc

    def finish(acc):
        gate = mod_ref[0, gate_idx:gate_idx + 1, :]
        o_ref[...] = res_ref[...] + gate * acc

    if nk == 1:
        finish(partial_product())
    else:
        acc_ref = scratch[0]
        k = pl.program_id(2)

        @pl.when(k == 0)
        def _():
            acc_ref[...] = jnp.zeros_like(acc_ref)

        acc_ref[...] += partial_product()

        @pl.when(k == nk - 1)
        def _():
            finish(acc_ref[...])


def res_matmul(xs, w, w_idx, res, mod, *, gate_idx, is_ctx, cast_weights=False, name):
    n_rows = res.shape[0]
    _, k_dim, n_dim = w.shape
    tk = MM_TK if len(xs) == 1 else k_dim
    nk = k_dim // tk
    if cast_weights:
        assert n_rows == MM_TILE
        tm, tn = MM_TILE, MM_TILE // 2
    elif nk == 1:
        tm, tn = MM_TILE // 2, n_dim
    else:
        tm = tn = MM_TILE
    if len(xs) == 1:
        x_specs = [pl.BlockSpec((tm, tk), lambda i, j, k: (i, k))]
    else:
        x_specs = [pl.BlockSpec((tm, x.shape[1]), lambda i, j, k: (i, 0)) for x in xs]
    mod_row = _mod_row_of_tile(is_ctx, tm)
    out_shape = jax.ShapeDtypeStruct((n_rows, n_dim), F32)
    out_specs = pl.BlockSpec((tm, tn), lambda i, j, k: (i, j))
    if cast_weights:
        out_shape = (out_shape, jax.ShapeDtypeStruct((1, k_dim, n_dim), BF16))
        out_specs = (out_specs, pl.BlockSpec((None, tk, tn), lambda i, j, k: (0, k, j)))
    return pl.pallas_call(
        functools.partial(_res_mm_body, n_x=len(xs), nk=nk, gate_idx=gate_idx, cast_w=cast_weights),
        out_shape=out_shape,
        grid=(n_rows // tm, n_dim // tn, nk),
        in_specs=x_specs + [
            pl.BlockSpec((None, tk, tn), lambda i, j, k: (w_idx, k, j)),
            pl.BlockSpec((tm, tn), lambda i, j, k: (i, j)),
            pl.BlockSpec((1, N_MOD, tn), lambda i, j, k: (mod_row(i), 0, j)),
        ],
        out_specs=out_specs,
        scratch_shapes=[pltpu.VMEM((tm, tn), F32)] if nk > 1 else [],
        compiler_params=_cparams(("arbitrary", "arbitrary", "arbitrary")),
        name=name,
    )(*xs, w, res, mod)


NA_GROUP_ROWS = 4
NA_GROUP_Q = NA_GROUP_ROWS * GRID_W
NA_GROUP_KROWS = NA_GROUP_ROWS + NA_WIN_R
NA_GROUP_K = NA_GROUP_KROWS * GRID_W
NA_GROUPS = GRID_ROWS // NA_GROUP_ROWS
NA_PREP_ROWS = 256
NA_SCORE_LOOKAHEAD = 1


def _na_window_row0(rq):
    return min(max(rq - NA_WIN_R // 2, 0), GRID_ROWS - NA_WIN_R)


def _na_group_key_row0(r):
    return min(max(r - NA_WIN_R // 2, 0), GRID_ROWS - NA_GROUP_KROWS)


def _na_group_signature(r):
    k0 = _na_group_key_row0(r)
    return (k0 - r,) + tuple(_na_window_row0(r + j) - k0 for j in range(NA_GROUP_ROWS))


NA_GROUP_KINDS = tuple(sorted({_na_group_signature(r): r for r in reversed(range(0, GRID_ROWS, NA_GROUP_ROWS))}.values()))


def _na_group_kind(r):
    return [_na_group_signature(k) for k in NA_GROUP_KINDS].index(_na_group_signature(r))


def na_bias_table(rpb):
    w = GRID_W
    col = np.arange(w)
    c0 = np.clip(col - NA_WIN_C // 2, 0, w - NA_WIN_C)
    in_win = (col[None, :] >= c0[:, None]) & (col[None, :] < c0[:, None] + NA_WIN_C)
    padded = jnp.pad(rpb.astype(F32), ((0, 0), (0, 0), (w, w)))
    toep = jnp.stack([padded[:, :, NA_WIN_C - 1 - qc + w:NA_WIN_C - 1 - qc + 2 * w] for qc in range(w)], axis=2)
    toep = jnp.where(in_win[None, None], toep * LOG2E, NEG_BIG)
    masked = jnp.full((NA_HEADS, w, w), NEG_BIG, F32)
    kinds = []
    for r in NA_GROUP_KINDS:
        q_rows = []
        for rq in range(r, r + NA_GROUP_ROWS):
            r0 = _na_window_row0(rq)
            blocks = []
            for kr in range(_na_group_key_row0(r), _na_group_key_row0(r) + NA_GROUP_KROWS):
                blocks.append(toep[:, kr - rq + NA_WIN_R - 1] if r0 <= kr < r0 + NA_WIN_R else masked)
            q_rows.append(jnp.concatenate(blocks, axis=2))
        kinds.append(jnp.concatenate(q_rows, axis=1))
    return jnp.stack(kinds, axis=1)


def _na_body(flag_ref, q_ref, k_ref, v_ref, qc_ref, kc_ref, vc_ref, qg_ref, kg_ref, bias_ref,
             o_ref, oc_ref, qn_ref, knt_ref, vn_ref, kcnt_ref, vcn_ref):
    scale = HEAD_DIM ** -0.5 * LOG2E
    qg = qg_ref[...]
    kg = kg_ref[...]

    def with_ones(v):
        return jnp.concatenate([v.astype(BF16), jnp.ones(v.shape, BF16)], axis=1)

    def normalised(o_ext):
        return (o_ext[:, :HEAD_DIM] / o_ext[:, HEAD_DIM:]).astype(BF16)

    for i in range(SEQ // NA_PREP_ROWS):
        rows = slice(i * NA_PREP_ROWS, (i + 1) * NA_PREP_ROWS)
        qn_ref[rows, :] = (_rms_rows(q_ref[rows, :], qg) * scale).astype(BF16)
        knt_ref[:, rows] = _rms_rows(k_ref[rows, :], kg).T.astype(BF16)
        vn_ref[rows, :] = with_ones(v_ref[rows, :])
    kcnt = _rms_rows(kc_ref[...], kg).T.astype(BF16)
    vcn = with_ones(vc_ref[...])
    kcnt_ref[...] = kcnt
    vcn_ref[...] = vcn

    def group_rows(gi):
        k0 = _na_group_key_row0(gi * NA_GROUP_ROWS) * GRID_W
        return slice(gi * NA_GROUP_Q, (gi + 1) * NA_GROUP_Q), slice(k0, k0 + NA_GROUP_K)

    def scores(gi):
        qrows, krows = group_rows(gi)
        qb = qn_ref[qrows, :]
        return _dot(qb, knt_ref[:, krows]), _dot(qb, kcnt_ref[...])

    def attend(shifted):
        pending = [scores(gi) for gi in range(NA_SCORE_LOOKAHEAD)]
        for gi in range(NA_GROUPS):
            s1, s2 = pending.pop(0)
            if gi + NA_SCORE_LOOKAHEAD < NA_GROUPS:
                pending.append(scores(gi + NA_SCORE_LOOKAHEAD))
            qrows, krows = group_rows(gi)
            s1 = s1 + bias_ref[0, _na_group_kind(gi * NA_GROUP_ROWS)]
            if shifted:
                m = jnp.maximum(jnp.max(s1, axis=-1, keepdims=True), jnp.max(s2, axis=-1, keepdims=True))
                s1, s2 = s1 - m, s2 - m
            p1 = jnp.exp2(s1).astype(BF16)
            p2 = jnp.exp2(s2).astype(BF16)
            o_ref[qrows, :] = normalised(_dot(p1, vn_ref[krows, :]) + _dot(p2, vcn_ref[...]))

        s = _dot((_rms_rows(qc_ref[...], qg) * scale).astype(BF16), kcnt)
        if shifted:
            s = s - jnp.max(s, axis=-1, keepdims=True)
        oc_ref[...] = normalised(_dot(jnp.exp2(s).astype(BF16), vcn))

    scores_bounded = flag_ref[0, 0] > 0.5
    pl.when(scores_bounded)(lambda: attend(False))
    pl.when(jnp.logical_not(scores_bounded))(lambda: attend(True))


def na_attention(p_lat, p_ctx, q_g, k_g, rpb):
    hd = HEAD_DIM
    lat = lambda seg: pl.BlockSpec((SEQ, hd), lambda h, b: (b, seg * NA_HEADS + h))
    ctx = lambda seg: pl.BlockSpec((CTX_LEN, hd), lambda h, b: (b, seg * NA_HEADS + h))
    bias = na_bias_table(rpb)
    score_bound = LOG2E * (hd ** 0.5 * jnp.max(jnp.abs(q_g)) * jnp.max(jnp.abs(k_g)) + jnp.max(jnp.abs(rpb)))
    flag = (score_bound < SOFTMAX_NOSHIFT_LOG2).astype(F32).reshape(1, 1)
    return pl.pallas_call(
        _na_body,
        out_shape=(jax.ShapeDtypeStruct((N_LAT, NA_WIDTH), BF16),
                   jax.ShapeDtypeStruct((N_CTX, NA_WIDTH), BF16)),
        grid=(NA_HEADS, BATCH),
        in_specs=[pl.BlockSpec(memory_space=pltpu.SMEM),
                  lat(0), lat(1), lat(2), ctx(0), ctx(1), ctx(2),
                  pl.BlockSpec((1, hd), lambda h, b: (0, 0)),
                  pl.BlockSpec((1, hd), lambda h, b: (0, 0)),
                  pl.BlockSpec((1, len(NA_GROUP_KINDS), NA_GROUP_Q, NA_GROUP_K), lambda h, b: (h, 0, 0, 0))],
        out_specs=(pl.BlockSpec((SEQ, hd), lambda h, b: (b, h)),
                   pl.BlockSpec((CTX_LEN, hd), lambda h, b: (b, h))),
        scratch_shapes=[pltpu.VMEM((SEQ, hd), BF16), pltpu.VMEM((hd, SEQ), BF16), pltpu.VMEM((SEQ, 2 * hd), BF16),
                        pltpu.VMEM((hd, CTX_LEN), BF16), pltpu.VMEM((CTX_LEN, 2 * hd), BF16)],
        compiler_params=_cparams(("arbitrary", "arbitrary")),
        name="na_attention",
    )(flag, p_lat, p_lat, p_lat, p_ctx, p_ctx, p_ctx, q_g.reshape(1, hd), k_g.reshape(1, hd), bias)


HG_BLK = 128
HG_CPB = HG_BLK // HG_CHUNK
HG_CTX_BLKS = CTX_LEN // HG_BLK
HG_LAT_BLKS = SEQ // HG_BLK
HG_BLKS = HG_CTX_BLKS + HG_LAT_BLKS
HG_SCAN_STEPS = 3


def _hg_ref_rows(reverse):
    b, c = HG_BLK, HG_CHUNK
    if not reverse:
        return {c: [None] + [c * i - 1 for i in range(1, b // c)],
                32: [32 * j + 15 for j in range(b // 32)], 64: [64 * j + 31 for j in range(b // 64)],
                128: [63], "end": [b - 1]}
    return {c: [c * i + c for i in range(b // c - 1)] + [None],
            32: [32 * j + 16 for j in range(b // 32)], 64: [64 * j + 32 for j in range(b // 64)],
            128: [64], "end": [0]}


def _hg_level_codes(reverse):
    t = lax.broadcasted_iota(jnp.int32, (HG_BLK, HG_BLK), 0)
    s = lax.broadcasted_iota(jnp.int32, (HG_BLK, HG_BLK), 1)
    if reverse:
        t, s = s, t
    code = jnp.where((t >> 6) > (s >> 6), 4, 0)
    code = jnp.where(((t >> 6) == (s >> 6)) & ((t >> 5) > (s >> 5)), 3, code)
    code = jnp.where(((t >> 5) == (s >> 5)) & ((t >> 4) > (s >> 4)), 2, code)
    return jnp.where(((t >> 4) == (s >> 4)) & (s <= t), 1, code)


def _hg_body(q_ref, zf_ref, zb_ref, i_ref, g_ref, qc_ref, zfc_ref, zbc_ref, ic_ref, gc_ref,
             lbf_ref, lbb_ref, gn_ref, o_ref, oc_ref,
             qd_s, ka_s, q64_s, k64_s, q128_s, k128_s, qb_s, ke_s, gb_s, v_s, o_s, st_s, x_s, code_s):
    lbs = (lbf_ref[...], lbb_ref[...])
    ri = lax.broadcasted_iota(jnp.int32, (HG_BLK, HG_BLK), 0)
    ci = lax.broadcasted_iota(jnp.int32, (HG_BLK, HG_BLK), 1)
    tri = jnp.where(ci <= ri, 1.0, 0.0).astype(BF16)
    later_half = {g: jnp.where((ri & (g // 2)) != 0, 1.0, -1.0) for g in (32, 64, 128)}
    for d in range(2):
        code_s[d] = _hg_level_codes(d == 1)

    def prefix_rows(x):
        hi = x.astype(BF16)
        r1 = x - hi.astype(F32)
        mid = r1.astype(BF16)
        lo = (r1 - mid.astype(F32)).astype(BF16)
        y = _dot(tri, jnp.concatenate([hi, mid, lo], axis=1))
        return y[:, :HEAD_DIM] + y[:, HEAD_DIM:2 * HEAD_DIM] + y[:, 2 * HEAD_DIM:]

    def ref_rows(d, rows):
        group = HG_BLK // len(rows)
        parts = [jnp.zeros((group, HEAD_DIM), F32) if r is None else
                 jnp.broadcast_to(x_s[d, r:r + 1, :], (group, HEAD_DIM)) for r in rows]
        return parts[0] if len(parts) == 1 else jnp.concatenate(parts, axis=0)

    def prep_block(blk, q, zf, zb, v):
        qs = q * jax.nn.sigmoid(q)
        v_s[blk] = v.astype(BF16)
        for d, z in enumerate((zf, zb)):
            f = lbs[d] + (1.0 - lbs[d]) * jax.nn.sigmoid(z)
            logf = jnp.log2(f)
            k = 1.0 - f
            x = prefix_rows(logf)
            x_s[d] = x
            if d == 1:
                x = ref_rows(d, [HG_BLK - 1]) - x + logf
                x_s[d] = x
            rows = _hg_ref_rows(d == 1)
            r16, r32, r64, r128 = (ref_rows(d, rows[g]) for g in (HG_CHUNK, 32, 64, 128))
            x_end = ref_rows(d, rows["end"])
            sign = -1.0 if d == 1 else 1.0
            e32, e64, e128 = (jnp.exp2((x - r) * (sign * later_half[g]))
                              for g, r in ((32, r32), (64, r64), (128, r128)))
            qd_s[d, blk] = (qs * jnp.exp2(x - r16)).astype(BF16)
            ka_s[d, blk] = jnp.concatenate([(k * jnp.exp2(r16 - x)).T, (k * e32).T], axis=1).astype(BF16)
            q64_s[d, blk] = (qs * e64).astype(BF16)
            k64_s[d, blk] = (k * e64).T.astype(BF16)
            q128_s[d, blk] = (qs * e128).astype(BF16)
            k128_s[d, blk] = (k * e128).T.astype(BF16)
            qb_s[d, blk] = (qs * jnp.exp2(x)).astype(BF16)
            ke_s[d, blk] = (k * jnp.exp2(x_end - x)).T.astype(BF16)
            gb_s[d, blk] = jnp.exp2(x_end).T

    for cb in range(HG_CTX_BLKS):
        rows = slice(cb * HG_BLK, (cb + 1) * HG_BLK)
        prep_block(cb, qc_ref[rows, :], zfc_ref[rows, :], zbc_ref[rows, :], ic_ref[rows, :])

    def prep_lat(n, _):
        rows = pl.ds(pl.multiple_of(n * HG_BLK, HG_BLK), HG_BLK)
        prep_block(HG_CTX_BLKS + n, q_ref[rows, :], zf_ref[rows, :], zb_ref[rows, :], i_ref[rows, :])
        return 0

    lax.fori_loop(0, HG_LAT_BLKS, prep_lat, 0, unroll=2)

    st_s[...] = jnp.zeros_like(st_s)

    def scan_steps(i, _):
        chains = []
        for n in [i * HG_SCAN_STEPS + j for j in range(HG_SCAN_STEPS)]:
            chains.append((0, n))
            chains.append((1, jnp.where(n < HG_CTX_BLKS, HG_CTX_BLKS - 1 - n, HG_BLKS + HG_CTX_BLKS - 1 - n)))
        ready = []
        for d, blk in chains:
            a1 = _dot(qd_s[d, blk], ka_s[d, blk])
            a64 = _dot(q64_s[d, blk], k64_s[d, blk])
            a128 = _dot(q128_s[d, blk], k128_s[d, blk])
            v = v_s[blk]
            inc = _dot(ke_s[d, blk], v)
            code = code_s[d]
            att = jnp.where(code == 1, a1[:, :HG_BLK],
                            jnp.where(code == 2, a1[:, HG_BLK:],
                                      jnp.where(code == 3, a64, jnp.where(code == 4, a128, 0.0))))
            ready.append((jnp.concatenate([att.astype(BF16), qb_s[d, blk]], axis=1), v, inc))
        for (d, blk), (lhs, v, inc) in zip(chains, ready):
            state = st_s[d]
            o_s[d, blk] = _dot(lhs, jnp.concatenate([v, state.astype(BF16)], axis=0))
            st_s[d] = gb_s[d, blk] * state + inc
        return 0

    lax.fori_loop(0, HG_BLKS // HG_SCAN_STEPS, scan_steps, 0)

    gn = gn_ref[...]

    def finish(blk, gate):
        o = o_s[0, blk] + o_s[1, blk]
        return (_rms_rows(o, gn) * (gate * jax.nn.sigmoid(gate))).astype(BF16)

    for cb in range(HG_CTX_BLKS):
        rows = slice(cb * HG_BLK, (cb + 1) * HG_BLK)
        oc_ref[rows, :] = finish(cb, gc_ref[rows, :])

    def fin_lat(n, _):
        rows = pl.ds(pl.multiple_of(n * HG_BLK, HG_BLK), HG_BLK)
        o_ref[rows, :] = finish(HG_CTX_BLKS + n, g_ref[rows, :])
        return 0

    lax.fori_loop(0, HG_LAT_BLKS, fin_lat, 0, unroll=4)


def hgrn2(p_lat, p_ctx, lb_fwd, lb_bwd, gn_g):
    hd = HEAD_DIM
    lat = lambda seg: pl.BlockSpec((SEQ, hd), lambda b, h: (b, seg * HG_HEADS + h))
    ctx = lambda seg: pl.BlockSpec((CTX_LEN, hd), lambda b, h: (b, seg * HG_HEADS + h))
    head_vec = pl.BlockSpec((1, hd), lambda b, h: (0, h))
    dir_bf = lambda width=hd: pltpu.VMEM((2, HG_BLKS, HG_BLK, width), BF16)
    dir_f32 = lambda: pltpu.VMEM((2, HG_BLKS, HG_BLK, hd), F32)
    return pl.pallas_call(
        _hg_body,
        out_shape=(jax.ShapeDtypeStruct((N_LAT, HG_WIDTH), BF16),
                   jax.ShapeDtypeStruct((N_CTX, HG_WIDTH), BF16)),
        grid=(BATCH, HG_HEADS),
        in_specs=[lat(3), lat(4), lat(5), lat(6), lat(7), ctx(3), ctx(4), ctx(5), ctx(6), ctx(7),
                  head_vec, head_vec, pl.BlockSpec((1, hd), lambda b, h: (0, 0))],
        out_specs=(pl.BlockSpec((SEQ, hd), lambda b, h: (b, h)),
                   pl.BlockSpec((CTX_LEN, hd), lambda b, h: (b, h))),
        scratch_shapes=[dir_bf(), dir_bf(2 * hd), dir_bf(), dir_bf(), dir_bf(), dir_bf(), dir_bf(), dir_bf(),
                        dir_f32(), pltpu.VMEM((HG_BLKS, HG_BLK, hd), BF16), dir_f32(),
                        pltpu.VMEM((2, hd, hd), F32), pltpu.VMEM((2, HG_BLK, hd), F32),
                        pltpu.VMEM((2, HG_BLK, HG_BLK), jnp.int32)],
        compiler_params=_cparams(("arbitrary", "arbitrary")),
        name="hgrn2",
    )(p_lat, p_lat, p_lat, p_lat, p_lat, p_ctx, p_ctx, p_ctx, p_ctx, p_ctx,
      lb_fwd.reshape(1, -1), lb_bwd.reshape(1, -1), gn_g.reshape(1, hd))


DF_TQ = 512
DF_NK = SEQ + CTX_LEN
DF_TK = 768
DF_KTILES = DF_NK // DF_TK
DF_SCORE_LOOKAHEAD = 2


def _diff_body(lam_ref, q_ref, k_ref, kc_ref, v_ref, vc_ref, cos_ref, sin_ref, cost_ref, sint_ref,
               qgt_ref, kg_ref, sg_ref, o_ref, qt_s, k0_s, k1_s, vt_s, e_s, *, out_scale):
    dk = DIFF_DK
    kg = kg_ref[...]

    @pl.when(pl.program_id(2) == 0)
    def _():
        ri = lax.broadcasted_iota(jnp.int32, (dk, dk), 0)
        ci = lax.broadcasted_iota(jnp.int32, (dk, dk), 1)
        ones = jnp.ones((dk, dk), BF16)
        swap_halves = jnp.where(ri == ((ci + dk // 2) & (dk - 1)), 1.0, 0.0).astype(BF16)

        def times(x, m):
            hi = x.astype(BF16)
            lo = (x - hi.astype(F32)).astype(BF16)
            return _dot(hi, m) + _dot(lo, m)

        def key_rows(x, c2, s2):
            xn = x * lax.rsqrt(times(x * x, ones) * (1.0 / dk) + EPS) * kg
            return (xn * c2 + times(xn, swap_halves) * s2).astype(BF16)

        def prep(i, _):
            rows = pl.ds(pl.multiple_of(i * CTX_LEN, CTX_LEN), CTX_LEN)
            c2 = cos_ref[rows, :]
            s2 = sin_ref[rows, :]
            kk = k_ref[rows, :]
            k0_s[rows, :] = key_rows(kk[:, :dk], c2, s2)
            k1_s[rows, :] = key_rows(kk[:, dk:], c2, s2)
            vt_s[i] = v_ref[rows, :].T.astype(BF16)
            qq_t = q_ref[rows, :].T
            ct = cost_ref[i]
            st = sint_ref[i]
            for j in range(2):
                xt = qq_t[j * dk:(j + 1) * dk, :]
                inv = lax.rsqrt(jnp.mean(xt * xt, axis=0, keepdims=True) + EPS)
                xn = xt * inv * qgt_ref[...]
                swapped = jnp.concatenate([xn[dk // 2:, :], xn[:dk // 2, :]], axis=0)
                qt_s[j, i] = (xn * ct + swapped * st).astype(BF16)
            return 0

        lax.fori_loop(0, SEQ // CTX_LEN, prep, 0, unroll=4)
        kc = kc_ref[...]
        k0_s[SEQ:, :] = _rms_rows(kc[:, :dk], kg).astype(BF16)
        k1_s[SEQ:, :] = _rms_rows(kc[:, dk:], kg).astype(BF16)
        vt_s[SEQ // CTX_LEN] = vc_ref[...].T.astype(BF16)

    lam = lam_ref[0, 0]
    qpt = DF_TQ // CTX_LEN
    q0 = pl.program_id(2) * qpt
    qt = [jnp.concatenate([qt_s[i, q0 + j] for j in range(qpt)], axis=1) for i in range(2)]
    k_s = (k0_s, k1_s)

    cpt = DF_TK // CTX_LEN
    jobs = [(t, i) for t in range(DF_KTILES) for i in range(2)]

    def scores(job):
        t, i = job
        return _dot(k_s[i][t * DF_TK:(t + 1) * DF_TK, :], qt[i])

    def attend(shifted):
        m = [None, None]
        l = [None, None]
        m_tile = [[None] * DF_KTILES for _ in range(2)]
        pending = [scores(job) for job in jobs[:DF_SCORE_LOOKAHEAD]]
        for n, (t, i) in enumerate(jobs):
            s = pending.pop(0)
            if n + DF_SCORE_LOOKAHEAD < len(jobs):
                pending.append(scores(jobs[n + DF_SCORE_LOOKAHEAD]))
            if shifted:
                tile_max = jnp.max(s, axis=0, keepdims=True)
                m_new = tile_max if t == 0 else jnp.maximum(m[i], tile_max)
                e = jnp.exp2(s - m_new)
            else:
                e = jnp.exp2(s)
            e_s[i, t] = e.astype(BF16)
            tile_sum = jnp.sum(e, axis=0, keepdims=True)
            if t == 0:
                l[i] = tile_sum
            elif shifted:
                l[i] = jnp.exp2(m[i] - m_new) * l[i] + tile_sum
            else:
                l[i] = l[i] + tile_sum
            if shifted:
                m[i] = m_new
                m_tile[i][t] = m_new

        weight = (1.0 / l[0], lam / l[1])
        acc = None
        for t in range(DF_KTILES):
            if shifted:
                f0, f1 = ((jnp.exp2(m_tile[i][t] - m[i]) * weight[i]).astype(BF16) for i in range(2))
            else:
                f0, f1 = (w.astype(BF16) for w in weight)
            a = e_s[0, t] * f0 - e_s[1, t] * f1
            vt = jnp.concatenate([vt_s[t * cpt + j] for j in range(cpt)], axis=1)
            d = _dot(vt, a)
            acc = d if acc is None else acc + d
        o_ref[...] = (_rms_rows(acc.T, sg_ref[...]) * out_scale).astype(o_ref.dtype)

    scores_bounded = lam_ref[0, 1] > 0.5
    pl.when(scores_bounded)(lambda: attend(False))
    pl.when(jnp.logical_not(scores_bounded))(lambda: attend(True))


def diff_attention(p_lat, p_ctx, lam, cos2, sin2, q_g, k_g, subln_g, out_scale):
    nq = SEQ // DF_TQ
    w = 2 * DIFF_DK
    kcol0 = D_MODEL // w
    vcol0 = 2 * D_MODEL // w
    vec = lambda n: pl.BlockSpec((1, n), lambda b, h, i: (0, 0))
    n_chunks = SEQ // CTX_LEN
    chunked_t = lambda tab: tab.reshape(n_chunks, CTX_LEN, DIFF_DK).transpose(0, 2, 1)
    q_gain_t = jnp.broadcast_to((q_g.astype(F32) * (DIFF_DK ** -0.5 * LOG2E))[:, None], (DIFF_DK, CTX_LEN))
    score_bound = LOG2E * DIFF_DK ** 0.5 * jnp.max(jnp.abs(q_g)) * jnp.max(jnp.abs(k_g))
    scalars = jnp.stack([lam.reshape(()), (score_bound < SOFTMAX_NOSHIFT_LOG2).astype(F32)]).reshape(1, 2)
    return pl.pallas_call(
        functools.partial(_diff_body, out_scale=out_scale),
        out_shape=jax.ShapeDtypeStruct((N_LAT, D_MODEL), BF16),
        grid=(BATCH, DIFF_HEADS, nq),
        in_specs=[
            pl.BlockSpec(memory_space=pltpu.SMEM),
            pl.BlockSpec((SEQ, w), lambda b, h, i: (b, h)),
            pl.BlockSpec((SEQ, w), lambda b, h, i: (b, kcol0 + h)),
            pl.BlockSpec((CTX_LEN, w), lambda b, h, i: (b, kcol0 + h)),
            pl.BlockSpec((SEQ, w), lambda b, h, i: (b, vcol0 + h)),
            pl.BlockSpec((CTX_LEN, w), lambda b, h, i: (b, vcol0 + h)),
            pl.BlockSpec((SEQ, DIFF_DK), lambda b, h, i: (0, 0)),
            pl.BlockSpec((SEQ, DIFF_DK), lambda b, h, i: (0, 0)),
            pl.BlockSpec((n_chunks, DIFF_DK, CTX_LEN), lambda b, h, i: (0, 0, 0)),
            pl.BlockSpec((n_chunks, DIFF_DK, CTX_LEN), lambda b, h, i: (0, 0, 0)),
            pl.BlockSpec((DIFF_DK, CTX_LEN), lambda b, h, i: (0, 0)),
            vec(DIFF_DK), vec(DIFF_DV),
        ],
        out_specs=pl.BlockSpec((DF_TQ, w), lambda b, h, i: (b * nq + i, h)),
        scratch_shapes=[pltpu.VMEM((2, SEQ // CTX_LEN, DIFF_DK, CTX_LEN), BF16),
                        pltpu.VMEM((DF_NK, DIFF_DK), BF16), pltpu.VMEM((DF_NK, DIFF_DK), BF16),
                        pltpu.VMEM((DF_NK // CTX_LEN, DIFF_DV, CTX_LEN), BF16),
                        pltpu.VMEM((2, DF_KTILES, DF_TK, DF_TQ), BF16)],
        compiler_params=_cparams(("arbitrary", "arbitrary", "arbitrary")),
        name="diff_attention",
    )(scalars, p_lat, p_lat, p_ctx, p_lat, p_ctx, cos2, sin2, chunked_t(cos2), chunked_t(sin2), q_gain_t,
      k_g.reshape(1, DIFF_DK), subln_g.reshape(1, DIFF_DV))


def _rope_tables():
    t = jnp.arange(SEQ)
    row = (t // GRID_W).astype(F32)
    col = (t % GRID_W).astype(F32)
    n_freq = DIFF_DK // 4
    inv = ROPE_THETA ** (-jnp.arange(n_freq, dtype=F32) / n_freq)
    ang = jnp.concatenate([row[:, None] * inv, col[:, None] * inv], axis=-1)
    cos, sin = jnp.cos(ang), jnp.sin(ang)
    return jnp.concatenate([cos, cos], axis=-1), jnp.concatenate([-sin, sin], axis=-1)


def _mlp_ctx(x, mod, g, w1_f32, w2_f32, layer):
    a, w1 = norm_matmul(x, g, mod, w1_f32, layer, shift_idx=3, is_ctx=True, out_dtype=BF16, relu2=True,
                        cast_weights=True, name="mlp_up_ctx")
    x, w2 = res_matmul([a], w2_f32, layer, x, mod, gate_idx=5, is_ctx=True, cast_weights=True, name="mlp_down_ctx")
    return x, w1, w2


def _mlp(x, mod, g, w1, w2):
    a = norm_matmul(x, g, mod, w1, 0, shift_idx=3, is_ctx=False, out_dtype=BF16, relu2=True, name="mlp_up")
    return res_matmul([a], w2, 0, x, mod, gate_idx=5, is_ctx=False, name="mlp_down")


def _bf16_layer(w, idx):
    return w[idx:idx + 1].astype(BF16)


def kernel(x, c, ctx, c_ctx, ada_w, ada_b, norm_mix_g, norm_mlp_g, mlp_w1, mlp_w2, ev_w_in, ev_w_out, na_q_g, na_k_g, na_rpb, hg_lb_logits, hg_gnorm_g, od_w_in, od_w_out, df_q_g, df_k_g, df_lambda, df_subln_g):
    xl = x.reshape(N_LAT, D_MODEL)
    xc = ctx.reshape(N_CTX, D_MODEL)
    cond = jnp.concatenate([c, c_ctx[None], jnp.zeros((MOD_ROWS - BATCH - 1, D_MODEL), F32)], axis=0)
    mods = ada_table(cond, ada_w, ada_b)
    lb_all = jnp.cumsum(jax.nn.softmax(hg_lb_logits.astype(F32), axis=1), axis=1)
    cos2, sin2 = _rope_tables()

    for l in range(DEPTH):
        need_ctx = l < DEPTH - 1
        mod = mods[l]
        g_mix = norm_mix_g[l]
        if l % 2 == 0:
            e = l // 2
            p_ctx, w_in = norm_matmul(xc, g_mix, mod, ev_w_in, e, shift_idx=0, is_ctx=True, out_dtype=F32,
                                      cast_weights=True, name="even_in_ctx")
            p_lat = norm_matmul(xl, g_mix, mod, w_in, 0, shift_idx=0, is_ctx=False, out_dtype=F32, name="even_in")
            na_lat, na_ctx = na_attention(p_lat, p_ctx, na_q_g[e], na_k_g[e], na_rpb[e])
            hg_lat, hg_ctx = hgrn2(p_lat, p_ctx, lb_all[0, l], lb_all[1, l], hg_gnorm_g[e])
            if need_ctx:
                xc, w_out = res_matmul([na_ctx, hg_ctx], ev_w_out, e, xc, mod, gate_idx=2, is_ctx=True,
                                       cast_weights=True, name="mix_out_ctx")
            else:
                w_out = _bf16_layer(ev_w_out, e)
            xl = res_matmul([na_lat, hg_lat], w_out, 0, xl, mod, gate_idx=2, is_ctx=False, name="mix_out")
        else:
            assert not need_ctx, "an odd layer followed by another layer needs context outputs"
            o = l // 2
            p_ctx, w_in = norm_matmul(xc, g_mix, mod, od_w_in, o, shift_idx=0, is_ctx=True, out_dtype=F32,
                                      cast_weights=True, name="odd_in_ctx")
            p_lat = norm_matmul(xl, g_mix, mod, w_in, 0, shift_idx=0, is_ctx=False, out_dtype=F32, name="odd_in")
            lam_init = 0.8 - 0.6 * math.exp(-0.3 * l)
            lp = df_lambda[o].astype(F32)
            lam = jnp.exp(jnp.sum(lp[0] * lp[1])) - jnp.exp(jnp.sum(lp[2] * lp[3])) + lam_init
            om = diff_attention(p_lat, p_ctx, lam, cos2, sin2, df_q_g[o], df_k_g[o], df_subln_g[o],
                                1.0 - lam_init)
            xl = res_matmul([om], _bf16_layer(od_w_out, o), 0, xl, mod, gate_idx=2, is_ctx=False, name="mix_out")
        if need_ctx:
            xc, w1, w2 = _mlp_ctx(xc, mod, norm_mlp_g[l], mlp_w1, mlp_w2, l)
        else:
            w1, w2 = _bf16_layer(mlp_w1, l), _bf16_layer(mlp_w2, l)
        xl = _mlp(xl, mod, norm_mlp_g[l], w1, w2)
    return xl.reshape(BATCH, SEQ, D_MODEL)
```

```python
import functools
import math

import jax
import jax.numpy as jnp
import numpy as np
from jax import lax
from jax.experimental import pallas as pl
from jax.experimental.pallas import tpu as pltpu

F32 = jnp.float32
BF16 = jnp.bfloat16

D_MODEL = 2048
BATCH = 4
SEQ = 2048
DEPTH = 2
GRID_W = 64
GRID_ROWS = SEQ // GRID_W
CTX_LEN = 256
HEAD_DIM = 128
NA_HEADS = 8
NA_WIN_R = 8
NA_WIN_C = 16
NA_WIDTH = NA_HEADS * HEAD_DIM
HG_HEADS = 8
HG_CHUNK = 16
HG_WIDTH = HG_HEADS * HEAD_DIM
EVEN_IN = 8 * NA_WIDTH
DIFF_HEADS = 8
DIFF_DK = HEAD_DIM
DIFF_DV = 2 * HEAD_DIM
ODD_IN = 3 * D_MODEL
D_FF = 4 * D_MODEL
N_MOD = 6
ROPE_THETA = 10000.0
EPS = 1e-6

N_LAT = BATCH * SEQ
N_CTX = BATCH * CTX_LEN
MOD_ROWS = 8
CTX_MOD_ROW = BATCH
NEG_BIG = -1e30
LOG2E = math.log2(math.e)
SOFTMAX_NOSHIFT_LOG2 = 100.0

VMEM_LIMIT_V7X = 56 * 1024 * 1024
MM_TILE = 1024
MM_TK = 2048
ROW_SLICES = 4


def _cparams(sem, vmem=VMEM_LIMIT_V7X):
    return pltpu.CompilerParams(dimension_semantics=sem, vmem_limit_bytes=vmem)


def _dot(a, b):
    return jnp.dot(a, b, preferred_element_type=F32)


def _dot_nt(a, b):
    return lax.dot_general(a, b, (((1,), (1,)), ((), ())), preferred_element_type=F32)


def _rms_rows(x, g):
    return x * lax.rsqrt(jnp.mean(x * x, axis=-1, keepdims=True) + EPS) * g


def _mod_row_of_tile(is_ctx, tile_rows):
    if is_ctx:
        return lambda i: CTX_MOD_ROW
    return lambda i: (i * tile_rows) // SEQ


def _ada_body(c_ref, w_ref, b_ref, o_ref):
    c = c_ref[...]
    s = (c * jax.nn.sigmoid(c)).astype(BF16)
    o_ref[0] = _dot(s, w_ref[0].astype(BF16)) + b_ref[0]


def ada_table(cond, ada_w, ada_b):
    tn = 1024
    n = N_MOD * D_MODEL
    out = pl.pallas_call(
        _ada_body,
        out_shape=jax.ShapeDtypeStruct((DEPTH, MOD_ROWS, n), F32),
        grid=(DEPTH, n // tn),
        in_specs=[
            pl.BlockSpec((MOD_ROWS, D_MODEL), lambda l, j: (0, 0)),
            pl.BlockSpec((1, D_MODEL, tn), lambda l, j: (l, 0, j)),
            pl.BlockSpec((1, 1, tn), lambda l, j: (l, 0, j)),
        ],
        out_specs=pl.BlockSpec((1, MOD_ROWS, tn), lambda l, j: (l, 0, j)),
        compiler_params=_cparams(("arbitrary", "arbitrary")),
        name="ada_table",
    )(cond, ada_w, ada_b.reshape(DEPTH, 1, n))
    return out.reshape(DEPTH, MOD_ROWS, N_MOD, D_MODEL)


def _norm_mm_body(x_ref, g_ref, mod_ref, w_ref, *out_and_scratch, shift_idx, relu2, cast_w):
    o_ref, h_ref = out_and_scratch[0], out_and_scratch[-1]
    first_col_tile = pl.program_id(1) == 0

    if cast_w:
        wb_ref = out_and_scratch[1]
        wb_ref[...] = w_ref[...].astype(BF16)
    else:
        wb_ref = w_ref

    def project(rows):
        acc = _dot(h_ref[rows, :], wb_ref[...])
        if relu2:
            acc = jnp.maximum(acc, 0.0)
            acc = acc * acc
        o_ref[rows, :] = acc.astype(o_ref.dtype)

    @pl.when(first_col_tile)
    def _():
        shift = mod_ref[0, shift_idx:shift_idx + 1, :]
        scale = mod_ref[0, shift_idx + 1:shift_idx + 2, :]
        n_rows = h_ref.shape[0]
        for r0 in range(0, n_rows, n_rows // ROW_SLICES):
            rows = slice(r0, r0 + n_rows // ROW_SLICES)
            h_ref[rows, :] = (_rms_rows(x_ref[rows, :], g_ref[...]) * (1.0 + scale) + shift).astype(BF16)
            project(rows)

    @pl.when(jnp.logical_not(first_col_tile))
    def _():
        project(slice(None))


def norm_matmul(x, g, mod, w, w_idx, *, shift_idx, is_ctx, out_dtype, relu2=False, col_block0=0, n_cols=None,
                cast_weights=False, name):
    n_rows = x.shape[0]
    tm = MM_TILE
    if cast_weights:
        assert n_rows == tm and col_block0 == 0 and n_cols is None
        tn = MM_TILE
    else:
        tn = MM_TILE * (jnp.dtype(F32).itemsize // jnp.dtype(out_dtype).itemsize)
    col_block0 = col_block0 * MM_TILE // tn
    n_cols = w.shape[2] if n_cols is None else n_cols
    mod_row = _mod_row_of_tile(is_ctx, tm)
    out_shape = jax.ShapeDtypeStruct((n_rows, n_cols), out_dtype)
    out_specs = pl.BlockSpec((tm, tn), lambda i, j: (i, j))
    if cast_weights:
        out_shape = (out_shape, jax.ShapeDtypeStruct((1, D_MODEL, n_cols), BF16))
        out_specs = (out_specs, pl.BlockSpec((None, D_MODEL, tn), lambda i, j: (0, 0, j)))
    return pl.pallas_call(
        functools.partial(_norm_mm_body, shift_idx=shift_idx, relu2=relu2, cast_w=cast_weights),
        out_shape=out_shape,
        grid=(n_rows // tm, n_cols // tn),
        in_specs=[
            pl.BlockSpec((tm, D_MODEL), lambda i, j: (i, 0), pipeline_mode=pl.Buffered(1) if cast_weights else None),
            pl.BlockSpec((1, D_MODEL), lambda i, j: (0, 0)),
            pl.BlockSpec((1, N_MOD, D_MODEL), lambda i, j: (mod_row(i), 0, 0)),
            pl.BlockSpec((None, D_MODEL, tn), lambda i, j: (w_idx, 0, col_block0 + j)),
        ],
        out_specs=out_specs,
        scratch_shapes=[pltpu.VMEM((tm, D_MODEL), BF16)],
        compiler_params=_cparams(("arbitrary", "arbitrary")),
        name=name,
    )(x, g.reshape(1, D_MODEL), mod, w)


def _res_mm_body(*refs, n_x, nk, gate_idx, cast_w):
    x_refs = refs[:n_x]
    w_ref, res_ref, mod_ref, o_ref = refs[n_x:n_x + 4]
    scratch = refs[n_x + 4:]
    if cast_w:
        wb_ref, scratch = scratch[0], scratch[1:]

        wb_ref[...] = w_ref[...].astype(BF16)
    else:
        wb_ref = w_ref

    def partial_product(rows):
        acc = None
        k0 = 0
        for x_ref in x_refs:
            kw = x_ref.shape[1]
            d = _dot(x_ref[rows, :], wb_ref[k0:k0 + kw, :])
            acc = d if acc is None else acc + d
            k0 += kw
        return acc

    def finish(carried):
        gate = mod_ref[0, gate_idx:gate_idx + 1, :]
        n_rows = o_ref.shape[0]
        for r0 in range(0, n_rows, n_rows // ROW_SLICES):
            rows = slice(r0, r0 + n_rows // ROW_SLICES)
            acc = partial_product(rows)
            if carried is not None:
                acc = carried[rows, :] + acc
            o_ref[rows, :] = res_ref[rows, :] + gate * acc

    if nk == 1:
        finish(None)
    else:
        acc_ref = scratch[0]
        k = pl.program_id(2)

        @pl.when(k == 0)
        def _():
            acc_ref[...] = partial_product(slice(None))

        @pl.when((k > 0) & (k < nk - 1))
        def _():
            acc_ref[...] += partial_product(slice(None))

        @pl.when(k == nk - 1)
        def _():
            finish(acc_ref)


def res_matmul(xs, w, w_idx, res, mod, *, gate_idx, is_ctx, cast_weights=False, name):
    n_rows = res.shape[0]
    _, k_dim, n_dim = w.shape
    tk = MM_TK if len(xs) == 1 else k_dim
    nk = k_dim // tk
    if cast_weights:
        assert n_rows == MM_TILE
        tm, tn = MM_TILE, MM_TILE // 2
    elif nk == 1:
        tm, tn = MM_TILE // 2, n_dim
    else:
        tm = tn = MM_TILE
    if len(xs) == 1:
        x_specs = [pl.BlockSpec((tm, tk), lambda i, j, k: (i, k))]
    else:
        x_specs = [pl.BlockSpec((tm, x.shape[1]), lambda i, j, k: (i, 0)) for x in xs]
    mod_row = _mod_row_of_tile(is_ctx, tm)
    out_shape = jax.ShapeDtypeStruct((n_rows, n_dim), F32)
    out_specs = pl.BlockSpec((tm, tn), lambda i, j, k: (i, j))
    if cast_weights:
        out_shape = (out_shape, jax.ShapeDtypeStruct((1, k_dim, n_dim), BF16))
        out_specs = (out_specs, pl.BlockSpec((None, tk, tn), lambda i, j, k: (0, k, j)))
    return pl.pallas_call(
        functools.partial(_res_mm_body, n_x=len(xs), nk=nk, gate_idx=gate_idx, cast_w=cast_weights),
        out_shape=out_shape,
        grid=(n_rows // tm, n_dim // tn, nk),
        in_specs=x_specs + [
            pl.BlockSpec((None, tk, tn), lambda i, j, k: (w_idx, k, j)),
            pl.BlockSpec((tm, tn), lambda i, j, k: (i, j)),
            pl.BlockSpec((1, N_MOD, tn), lambda i, j, k: (mod_row(i), 0, j)),
        ],
        out_specs=out_specs,
        scratch_shapes=[pltpu.VMEM((tm, tn), F32)] if nk > 1 else [],
        compiler_params=_cparams(("arbitrary", "arbitrary", "arbitrary")),
        name=name,
    )(*xs, w, res, mod)


NA_GROUP_ROWS = 4
NA_GROUP_Q = NA_GROUP_ROWS * GRID_W
NA_GROUP_KROWS = NA_GROUP_ROWS + NA_WIN_R
NA_GROUP_K = NA_GROUP_KROWS * GRID_W
NA_GROUPS = GRID_ROWS // NA_GROUP_ROWS
NA_PREP_ROWS = 256
NA_SCORE_LOOKAHEAD = 1


def _na_window_row0(rq):
    return min(max(rq - NA_WIN_R // 2, 0), GRID_ROWS - NA_WIN_R)


def _na_group_key_row0(r):
    return min(max(r - NA_WIN_R // 2, 0), GRID_ROWS - NA_GROUP_KROWS)


def _na_group_signature(r):
    k0 = _na_group_key_row0(r)
    return (k0 - r,) + tuple(_na_window_row0(r + j) - k0 for j in range(NA_GROUP_ROWS))


NA_GROUP_KINDS = tuple(sorted({_na_group_signature(r): r for r in reversed(range(0, GRID_ROWS, NA_GROUP_ROWS))}.values()))


def _na_group_kind(r):
    return [_na_group_signature(k) for k in NA_GROUP_KINDS].index(_na_group_signature(r))


def na_bias_table(rpb):
    w = GRID_W
    col = np.arange(w)
    c0 = np.clip(col - NA_WIN_C // 2, 0, w - NA_WIN_C)
    in_win = (col[None, :] >= c0[:, None]) & (col[None, :] < c0[:, None] + NA_WIN_C)
    padded = jnp.pad(rpb.astype(F32), ((0, 0), (0, 0), (w, w)))
    toep = jnp.stack([padded[:, :, NA_WIN_C - 1 - qc + w:NA_WIN_C - 1 - qc + 2 * w] for qc in range(w)], axis=2)
    toep = jnp.where(in_win[None, None], toep * LOG2E, NEG_BIG)
    masked = jnp.full((NA_HEADS, w, w), NEG_BIG, F32)
    kinds = []
    for r in NA_GROUP_KINDS:
        q_rows = []
        for rq in range(r, r + NA_GROUP_ROWS):
            r0 = _na_window_row0(rq)
            blocks = []
            for kr in range(_na_group_key_row0(r), _na_group_key_row0(r) + NA_GROUP_KROWS):
                blocks.append(toep[:, kr - rq + NA_WIN_R - 1] if r0 <= kr < r0 + NA_WIN_R else masked)
            q_rows.append(jnp.concatenate(blocks, axis=2))
        kinds.append(jnp.concatenate(q_rows, axis=1))
    return jnp.stack(kinds, axis=1)


def _na_body(flag_ref, q_ref, k_ref, v_ref, qc_ref, kc_ref, vc_ref, qg_ref, kg_ref, bias_ref,
             o_ref, oc_ref, qn_ref, knt_ref, vn_ref, kcnt_ref, vcn_ref):
    scale = HEAD_DIM ** -0.5 * LOG2E
    qg = qg_ref[...]
    kg = kg_ref[...]

    def with_ones(v):
        return jnp.concatenate([v.astype(BF16), jnp.ones(v.shape, BF16)], axis=1)

    def normalised(o_ext):
        return (o_ext[:, :HEAD_DIM] / o_ext[:, HEAD_DIM:]).astype(BF16)

    for i in range(SEQ // NA_PREP_ROWS):
        rows = slice(i * NA_PREP_ROWS, (i + 1) * NA_PREP_ROWS)
        qn_ref[rows, :] = (_rms_rows(q_ref[rows, :], qg) * scale).astype(BF16)
        knt_ref[:, rows] = _rms_rows(k_ref[rows, :], kg).T.astype(BF16)
        vn_ref[rows, :] = with_ones(v_ref[rows, :])
    kcnt = _rms_rows(kc_ref[...], kg).T.astype(BF16)
    vcn = with_ones(vc_ref[...])
    kcnt_ref[...] = kcnt
    vcn_ref[...] = vcn

    def group_rows(gi):
        k0 = _na_group_key_row0(gi * NA_GROUP_ROWS) * GRID_W
        return slice(gi * NA_GROUP_Q, (gi + 1) * NA_GROUP_Q), slice(k0, k0 + NA_GROUP_K)

    def scores(gi):
        qrows, krows = group_rows(gi)
        qb = qn_ref[qrows, :]
        return _dot(qb, knt_ref[:, krows]), _dot(qb, kcnt_ref[...])

    def attend(shifted):
        pending = [scores(gi) for gi in range(NA_SCORE_LOOKAHEAD)]
        for gi in range(NA_GROUPS):
            s1, s2 = pending.pop(0)
            if gi + NA_SCORE_LOOKAHEAD < NA_GROUPS:
                pending.append(scores(gi + NA_SCORE_LOOKAHEAD))
            qrows, krows = group_rows(gi)
            s1 = s1 + bias_ref[0, _na_group_kind(gi * NA_GROUP_ROWS)]
            if shifted:
                m = jnp.maximum(jnp.max(s1, axis=-1, keepdims=True), jnp.max(s2, axis=-1, keepdims=True))
                s1, s2 = s1 - m, s2 - m
            p1 = jnp.exp2(s1).astype(BF16)
            p2 = jnp.exp2(s2).astype(BF16)
            o_ref[qrows, :] = normalised(_dot(p1, vn_ref[krows, :]) + _dot(p2, vcn_ref[...]))

        s = _dot((_rms_rows(qc_ref[...], qg) * scale).astype(BF16), kcnt)
        if shifted:
            s = s - jnp.max(s, axis=-1, keepdims=True)
        oc_ref[...] = normalised(_dot(jnp.exp2(s).astype(BF16), vcn))

    scores_bounded = flag_ref[0, 0] > 0.5
    pl.when(scores_bounded)(lambda: attend(False))
    pl.when(jnp.logical_not(scores_bounded))(lambda: attend(True))


def na_attention(p_lat, p_ctx, q_g, k_g, rpb):
    hd = HEAD_DIM
    lat = lambda seg: pl.BlockSpec((SEQ, hd), lambda h, b: (b, seg * NA_HEADS + h))
    ctx = lambda seg: pl.BlockSpec((CTX_LEN, hd), lambda h, b: (b, seg * NA_HEADS + h))
    bias = na_bias_table(rpb)
    score_bound = LOG2E * (hd ** 0.5 * jnp.max(jnp.abs(q_g)) * jnp.max(jnp.abs(k_g)) + jnp.max(jnp.abs(rpb)))
    flag = (score_bound < SOFTMAX_NOSHIFT_LOG2).astype(F32).reshape(1, 1)
    return pl.pallas_call(
        _na_body,
        out_shape=(jax.ShapeDtypeStruct((N_LAT, NA_WIDTH), BF16),
                   jax.ShapeDtypeStruct((N_CTX, NA_WIDTH), BF16)),
        grid=(NA_HEADS, BATCH),
        in_specs=[pl.BlockSpec(memory_space=pltpu.SMEM),
                  lat(0), lat(1), lat(2), ctx(0), ctx(1), ctx(2),
                  pl.BlockSpec((1, hd), lambda h, b: (0, 0)),
                  pl.BlockSpec((1, hd), lambda h, b: (0, 0)),
                  pl.BlockSpec((1, len(NA_GROUP_KINDS), NA_GROUP_Q, NA_GROUP_K), lambda h, b: (h, 0, 0, 0))],
        out_specs=(pl.BlockSpec((SEQ, hd), lambda h, b: (b, h)),
                   pl.BlockSpec((CTX_LEN, hd), lambda h, b: (b, h))),
        scratch_shapes=[pltpu.VMEM((SEQ, hd), BF16), pltpu.VMEM((hd, SEQ), BF16), pltpu.VMEM((SEQ, 2 * hd), BF16),
                        pltpu.VMEM((hd, CTX_LEN), BF16), pltpu.VMEM((CTX_LEN, 2 * hd), BF16)],
        compiler_params=_cparams(("arbitrary", "arbitrary")),
        name="na_attention",
    )(flag, p_lat, p_lat, p_lat, p_ctx, p_ctx, p_ctx, q_g.reshape(1, hd), k_g.reshape(1, hd), bias)


HG_BLK = 128
HG_CPB = HG_BLK // HG_CHUNK
HG_CTX_BLKS = CTX_LEN // HG_BLK
HG_LAT_BLKS = SEQ // HG_BLK
HG_BLKS = HG_CTX_BLKS + HG_LAT_BLKS
HG_SCAN_STEPS = 3


def _hg_ref_rows(reverse):
    b, c = HG_BLK, HG_CHUNK
    if not reverse:
        return {c: [None] + [c * i - 1 for i in range(1, b // c)],
                32: [32 * j + 15 for j in range(b // 32)], 64: [64 * j + 31 for j in range(b // 64)],
                128: [63], "end": [b - 1]}
    return {c: [c * i + c for i in range(b // c - 1)] + [None],
            32: [32 * j + 16 for j in range(b // 32)], 64: [64 * j + 32 for j in range(b // 64)],
            128: [64], "end": [0]}


def _hg_level_codes(reverse):
    t = lax.broadcasted_iota(jnp.int32, (HG_BLK, HG_BLK), 0)
    s = lax.broadcasted_iota(jnp.int32, (HG_BLK, HG_BLK), 1)
    if reverse:
        t, s = s, t
    code = jnp.where((t >> 6) > (s >> 6), 4, 0)
    code = jnp.where(((t >> 6) == (s >> 6)) & ((t >> 5) > (s >> 5)), 3, code)
    code = jnp.where(((t >> 5) == (s >> 5)) & ((t >> 4) > (s >> 4)), 2, code)
    return jnp.where(((t >> 4) == (s >> 4)) & (s <= t), 1, code)


def _hg_body(q_ref, zf_ref, zb_ref, i_ref, g_ref, qc_ref, zfc_ref, zbc_ref, ic_ref, gc_ref,
             lbf_ref, lbb_ref, gn_ref, o_ref, oc_ref,
             qd_s, ka_s, q64_s, k64_s, q128_s, k128_s, qb_s, ke_s, gb_s, v_s, o_s, st_s, x_s, code_s):
    lbs = (lbf_ref[...], lbb_ref[...])
    ri = lax.broadcasted_iota(jnp.int32, (HG_BLK, HG_BLK), 0)
    ci = lax.broadcasted_iota(jnp.int32, (HG_BLK, HG_BLK), 1)
    tri = jnp.where(ci <= ri, 1.0, 0.0).astype(BF16)
    later_half = {g: jnp.where((ri & (g // 2)) != 0, 1.0, -1.0) for g in (32, 64, 128)}
    for d in range(2):
        code_s[d] = _hg_level_codes(d == 1)

    def prefix_rows(x):
        hi = x.astype(BF16)
        r1 = x - hi.astype(F32)
        mid = r1.astype(BF16)
        lo = (r1 - mid.astype(F32)).astype(BF16)
        y = _dot(tri, jnp.concatenate([hi, mid, lo], axis=1))
        return y[:, :HEAD_DIM] + y[:, HEAD_DIM:2 * HEAD_DIM] + y[:, 2 * HEAD_DIM:]

    def ref_rows(d, rows):
        group = HG_BLK // len(rows)
        parts = [jnp.zeros((group, HEAD_DIM), F32) if r is None else
                 jnp.broadcast_to(x_s[d, r:r + 1, :], (group, HEAD_DIM)) for r in rows]
        return parts[0] if len(parts) == 1 else jnp.concatenate(parts, axis=0)

    def prep_block(blk, q, zf, zb, v):
        qs = q * jax.nn.sigmoid(q)
        v_s[blk] = v.astype(BF16)
        for d, z in enumerate((zf, zb)):
            f = lbs[d] + (1.0 - lbs[d]) * jax.nn.sigmoid(z)
            logf = jnp.log2(f)
            k = 1.0 - f
            x = prefix_rows(logf)
            x_s[d] = x
            if d == 1:
                x = ref_rows(d, [HG_BLK - 1]) - x + logf
                x_s[d] = x
            rows = _hg_ref_rows(d == 1)
            r16, r32, r64, r128 = (ref_rows(d, rows[g]) for g in (HG_CHUNK, 32, 64, 128))
            x_end = ref_rows(d, rows["end"])
            sign = -1.0 if d == 1 else 1.0
            e32, e64, e128 = (jnp.exp2((x - r) * (sign * later_half[g]))
                              for g, r in ((32, r32), (64, r64), (128, r128)))
            qd_s[d, blk] = (qs * jnp.exp2(x - r16)).astype(BF16)
            ka_s[d, blk] = jnp.concatenate([(k * jnp.exp2(r16 - x)).T, (k * e32).T], axis=1).astype(BF16)
            q64_s[d, blk] = (qs * e64).astype(BF16)
            k64_s[d, blk] = (k * e64).T.astype(BF16)
            q128_s[d, blk] = (qs * e128).astype(BF16)
            k128_s[d, blk] = (k * e128).T.astype(BF16)
            qb_s[d, blk] = (qs * jnp.exp2(x)).astype(BF16)
            ke_s[d, blk] = (k * jnp.exp2(x_end - x)).T.astype(BF16)
            gb_s[d, blk] = jnp.exp2(x_end).T

    for cb in range(HG_CTX_BLKS):
        rows = slice(cb * HG_BLK, (cb + 1) * HG_BLK)
        prep_block(cb, qc_ref[rows, :], zfc_ref[rows, :], zbc_ref[rows, :], ic_ref[rows, :])

    def prep_lat(n, _):
        rows = pl.ds(pl.multiple_of(n * HG_BLK, HG_BLK), HG_BLK)
        prep_block(HG_CTX_BLKS + n, q_ref[rows, :], zf_ref[rows, :], zb_ref[rows, :], i_ref[rows, :])
        return 0

    lax.fori_loop(0, HG_LAT_BLKS, prep_lat, 0, unroll=2)

    st_s[...] = jnp.zeros_like(st_s)

    def scan_steps(i, _):
        chains = []
        for n in [i * HG_SCAN_STEPS + j for j in range(HG_SCAN_STEPS)]:
            chains.append((0, n))
            chains.append((1, jnp.where(n < HG_CTX_BLKS, HG_CTX_BLKS - 1 - n, HG_BLKS + HG_CTX_BLKS - 1 - n)))
        ready = []
        for d, blk in chains:
            a1 = _dot(qd_s[d, blk], ka_s[d, blk])
            a64 = _dot(q64_s[d, blk], k64_s[d, blk])
            a128 = _dot(q128_s[d, blk], k128_s[d, blk])
            v = v_s[blk]
            inc = _dot(ke_s[d, blk], v)
            code = code_s[d]
            att = jnp.where(code == 1, a1[:, :HG_BLK],
                            jnp.where(code == 2, a1[:, HG_BLK:],
                                      jnp.where(code == 3, a64, jnp.where(code == 4, a128, 0.0))))
            ready.append((jnp.concatenate([att.astype(BF16), qb_s[d, blk]], axis=1), v, inc))
        for (d, blk), (lhs, v, inc) in zip(chains, ready):
            state = st_s[d]
            o_s[d, blk] = _dot(lhs, jnp.concatenate([v, state.astype(BF16)], axis=0))
            st_s[d] = gb_s[d, blk] * state + inc
        return 0

    lax.fori_loop(0, HG_BLKS // HG_SCAN_STEPS, scan_steps, 0)

    gn = gn_ref[...]

    def finish(blk, gate):
        o = o_s[0, blk] + o_s[1, blk]
        return (_rms_rows(o, gn) * (gate * jax.nn.sigmoid(gate))).astype(BF16)

    for cb in range(HG_CTX_BLKS):
        rows = slice(cb * HG_BLK, (cb + 1) * HG_BLK)
        oc_ref[rows, :] = finish(cb, gc_ref[rows, :])

    def fin_lat(n, _):
        rows = pl.ds(pl.multiple_of(n * HG_BLK, HG_BLK), HG_BLK)
        o_ref[rows, :] = finish(HG_CTX_BLKS + n, g_ref[rows, :])
        return 0

    lax.fori_loop(0, HG_LAT_BLKS, fin_lat, 0, unroll=4)


def hgrn2(p_lat, p_ctx, lb_fwd, lb_bwd, gn_g):
    hd = HEAD_DIM
    lat = lambda seg: pl.BlockSpec((SEQ, hd), lambda b, h: (b, seg * HG_HEADS + h))
    ctx = lambda seg: pl.BlockSpec((CTX_LEN, hd), lambda b, h: (b, seg * HG_HEADS + h))
    head_vec = pl.BlockSpec((1, hd), lambda b, h: (0, h))
    dir_bf = lambda width=hd: pltpu.VMEM((2, HG_BLKS, HG_BLK, width), BF16)
    dir_f32 = lambda: pltpu.VMEM((2, HG_BLKS, HG_BLK, hd), F32)
    return pl.pallas_call(
        _hg_body,
        out_shape=(jax.ShapeDtypeStruct((N_LAT, HG_WIDTH), BF16),
                   jax.ShapeDtypeStruct((N_CTX, HG_WIDTH), BF16)),
        grid=(BATCH, HG_HEADS),
        in_specs=[lat(3), lat(4), lat(5), lat(6), lat(7), ctx(3), ctx(4), ctx(5), ctx(6), ctx(7),
                  head_vec, head_vec, pl.BlockSpec((1, hd), lambda b, h: (0, 0))],
        out_specs=(pl.BlockSpec((SEQ, hd), lambda b, h: (b, h)),
                   pl.BlockSpec((CTX_LEN, hd), lambda b, h: (b, h))),
        scratch_shapes=[dir_bf(), dir_bf(2 * hd), dir_bf(), dir_bf(), dir_bf(), dir_bf(), dir_bf(), dir_bf(),
                        dir_f32(), pltpu.VMEM((HG_BLKS, HG_BLK, hd), BF16), dir_f32(),
                        pltpu.VMEM((2, hd, hd), F32), pltpu.VMEM((2, HG_BLK, hd), F32),
                        pltpu.VMEM((2, HG_BLK, HG_BLK), jnp.int32)],
        compiler_params=_cparams(("arbitrary", "arbitrary")),
        name="hgrn2",
    )(p_lat, p_lat, p_lat, p_lat, p_lat, p_ctx, p_ctx, p_ctx, p_ctx, p_ctx,
      lb_fwd.reshape(1, -1), lb_bwd.reshape(1, -1), gn_g.reshape(1, hd))


DF_TQ = 512
DF_NK = SEQ + CTX_LEN
DF_TK = 768
DF_KTILES = DF_NK // DF_TK
DF_SCORE_LOOKAHEAD = 2


def _diff_body(lam_ref, q_ref, k_ref, kc_ref, v_ref, vc_ref, cos_ref, sin_ref, cost_ref, sint_ref,
               qgt_ref, kg_ref, sg_ref, o_ref, qt_s, k0_s, k1_s, vt_s, e_s, *, out_scale):
    dk = DIFF_DK
    kg = kg_ref[...]

    @pl.when(pl.program_id(2) == 0)
    def _():
        ri = lax.broadcasted_iota(jnp.int32, (dk, dk), 0)
        ci = lax.broadcasted_iota(jnp.int32, (dk, dk), 1)
        ones = jnp.ones((dk, dk), BF16)
        swap_halves = jnp.where(ri == ((ci + dk // 2) & (dk - 1)), 1.0, 0.0).astype(BF16)

        def times(x, m):
            hi = x.astype(BF16)
            lo = (x - hi.astype(F32)).astype(BF16)
            return _dot(hi, m) + _dot(lo, m)

        def key_rows(x, c2, s2):
            xn = x * lax.rsqrt(times(x * x, ones) * (1.0 / dk) + EPS) * kg
            return (xn * c2 + times(xn, swap_halves) * s2).astype(BF16)

        def prep(i, _):
            rows = pl.ds(pl.multiple_of(i * CTX_LEN, CTX_LEN), CTX_LEN)
            c2 = cos_ref[rows, :]
            s2 = sin_ref[rows, :]
            kk = k_ref[rows, :]
            k0_s[rows, :] = key_rows(kk[:, :dk], c2, s2)
            k1_s[rows, :] = key_rows(kk[:, dk:], c2, s2)
            vt_s[i] = v_ref[rows, :].T.astype(BF16)
            qq_t = q_ref[rows, :].T
            ct = cost_ref[i]
            st = sint_ref[i]
            for j in range(2):
                xt = qq_t[j * dk:(j + 1) * dk, :]
                inv = lax.rsqrt(jnp.mean(xt * xt, axis=0, keepdims=True) + EPS)
                xn = xt * inv * qgt_ref[...]
                swapped = jnp.concatenate([xn[dk // 2:, :], xn[:dk // 2, :]], axis=0)
                qt_s[j, i] = (xn * ct + swapped * st).astype(BF16)
            return 0

        lax.fori_loop(0, SEQ // CTX_LEN, prep, 0, unroll=4)
        kc = kc_ref[...]
        k0_s[SEQ:, :] = _rms_rows(kc[:, :dk], kg).astype(BF16)
        k1_s[SEQ:, :] = _rms_rows(kc[:, dk:], kg).astype(BF16)
        vt_s[SEQ // CTX_LEN] = vc_ref[...].T.astype(BF16)

    lam = lam_ref[0, 0]
    qpt = DF_TQ // CTX_LEN
    q0 = pl.program_id(2) * qpt
    qt = [jnp.concatenate([qt_s[i, q0 + j] for j in range(qpt)], axis=1) for i in range(2)]
    k_s = (k0_s, k1_s)

    cpt = DF_TK // CTX_LEN
    jobs = [(t, i) for t in range(DF_KTILES) for i in range(2)]

    def scores(job):
        t, i = job
        return _dot(k_s[i][t * DF_TK:(t + 1) * DF_TK, :], qt[i])

    def attend(shifted):
        m = [None, None]
        l = [None, None]
        m_tile = [[None] * DF_KTILES for _ in range(2)]
        pending = [scores(job) for job in jobs[:DF_SCORE_LOOKAHEAD]]
        for n, (t, i) in enumerate(jobs):
            s = pending.pop(0)
            if n + DF_SCORE_LOOKAHEAD < len(jobs):
                pending.append(scores(jobs[n + DF_SCORE_LOOKAHEAD]))
            if shifted:
                tile_max = jnp.max(s, axis=0, keepdims=True)
                m_new = tile_max if t == 0 else jnp.maximum(m[i], tile_max)
                e = jnp.exp2(s - m_new)
            else:
                e = jnp.exp2(s)
            e_s[i, t] = e.astype(BF16)
            tile_sum = jnp.sum(e, axis=0, keepdims=True)
            if t == 0:
                l[i] = tile_sum
            elif shifted:
                l[i] = jnp.exp2(m[i] - m_new) * l[i] + tile_sum
            else:
                l[i] = l[i] + tile_sum
            if shifted:
                m[i] = m_new
                m_tile[i][t] = m_new

        weight = (1.0 / l[0], lam / l[1])
        acc = None
        for t in range(DF_KTILES):
            if shifted:
                f0, f1 = ((jnp.exp2(m_tile[i][t] - m[i]) * weight[i]).astype(BF16) for i in range(2))
            else:
                f0, f1 = (w.astype(BF16) for w in weight)
            a = e_s[0, t] * f0 - e_s[1, t] * f1
            vt = jnp.concatenate([vt_s[t * cpt + j] for j in range(cpt)], axis=1)
            d = _dot(vt, a)
            acc = d if acc is None else acc + d
        o_ref[...] = (_rms_rows(acc.T, sg_ref[...]) * out_scale).astype(o_ref.dtype)

    scores_bounded = lam_ref[0, 1] > 0.5
    pl.when(scores_bounded)(lambda: attend(False))
    pl.when(jnp.logical_not(scores_bounded))(lambda: attend(True))


def diff_attention(p_lat, p_ctx, lam, cos2, sin2, q_g, k_g, subln_g, out_scale):
    nq = SEQ // DF_TQ
    w = 2 * DIFF_DK
    kcol0 = D_MODEL // w
    vcol0 = 2 * D_MODEL // w
    vec = lambda n: pl.BlockSpec((1, n), lambda b, h, i: (0, 0))
    n_chunks = SEQ // CTX_LEN
    chunked_t = lambda tab: tab.reshape(n_chunks, CTX_LEN, DIFF_DK).transpose(0, 2, 1)
    q_gain_t = jnp.broadcast_to((q_g.astype(F32) * (DIFF_DK ** -0.5 * LOG2E))[:, None], (DIFF_DK, CTX_LEN))
    score_bound = LOG2E * DIFF_DK ** 0.5 * jnp.max(jnp.abs(q_g)) * jnp.max(jnp.abs(k_g))
    scalars = jnp.stack([lam.reshape(()), (score_bound < SOFTMAX_NOSHIFT_LOG2).astype(F32)]).reshape(1, 2)
    return pl.pallas_call(
        functools.partial(_diff_body, out_scale=out_scale),
        out_shape=jax.ShapeDtypeStruct((N_LAT, D_MODEL), BF16),
        grid=(BATCH, DIFF_HEADS, nq),
        in_specs=[
            pl.BlockSpec(memory_space=pltpu.SMEM),
            pl.BlockSpec((SEQ, w), lambda b, h, i: (b, h)),
            pl.BlockSpec((SEQ, w), lambda b, h, i: (b, kcol0 + h)),
            pl.BlockSpec((CTX_LEN, w), lambda b, h, i: (b, kcol0 + h)),
            pl.BlockSpec((SEQ, w), lambda b, h, i: (b, vcol0 + h)),
            pl.BlockSpec((CTX_LEN, w), lambda b, h, i: (b, vcol0 + h)),
            pl.BlockSpec((SEQ, DIFF_DK), lambda b, h, i: (0, 0)),
            pl.BlockSpec((SEQ, DIFF_DK), lambda b, h, i: (0, 0)),
            pl.BlockSpec((n_chunks, DIFF_DK, CTX_LEN), lambda b, h, i: (0, 0, 0)),
            pl.BlockSpec((n_chunks, DIFF_DK, CTX_LEN), lambda b, h, i: (0, 0, 0)),
            pl.BlockSpec((DIFF_DK, CTX_LEN), lambda b, h, i: (0, 0)),
            vec(DIFF_DK), vec(DIFF_DV),
        ],
        out_specs=pl.BlockSpec((DF_TQ, w), lambda b, h, i: (b * nq + i, h)),
        scratch_shapes=[pltpu.VMEM((2, SEQ // CTX_LEN, DIFF_DK, CTX_LEN), BF16),
                        pltpu.VMEM((DF_NK, DIFF_DK), BF16), pltpu.VMEM((DF_NK, DIFF_DK), BF16),
                        pltpu.VMEM((DF_NK // CTX_LEN, DIFF_DV, CTX_LEN), BF16),
                        pltpu.VMEM((2, DF_KTILES, DF_TK, DF_TQ), BF16)],
        compiler_params=_cparams(("arbitrary", "arbitrary", "arbitrary")),
        name="diff_attention",
    )(scalars, p_lat, p_lat, p_ctx, p_lat, p_ctx, cos2, sin2, chunked_t(cos2), chunked_t(sin2), q_gain_t,
      k_g.reshape(1, DIFF_DK), subln_g.reshape(1, DIFF_DV))


def _rope_tables():
    t = jnp.arange(SEQ)
    row = (t // GRID_W).astype(F32)
    col = (t % GRID_W).astype(F32)
    n_freq = DIFF_DK // 4
    inv = ROPE_THETA ** (-jnp.arange(n_freq, dtype=F32) / n_freq)
    ang = jnp.concatenate([row[:, None] * inv, col[:, None] * inv], axis=-1)
    cos, sin = jnp.cos(ang), jnp.sin(ang)
    return jnp.concatenate([cos, cos], axis=-1), jnp.concatenate([-sin, sin], axis=-1)


def _mlp_ctx(x, mod, g, w1_f32, w2_f32, layer):
    a, w1 = norm_matmul(x, g, mod, w1_f32, layer, shift_idx=3, is_ctx=True, out_dtype=BF16, relu2=True,
                        cast_weights=True, name="mlp_up_ctx")
    x, w2 = res_matmul([a], w2_f32, layer, x, mod, gate_idx=5, is_ctx=True, cast_weights=True, name="mlp_down_ctx")
    return x, w1, w2


def _mlp(x, mod, g, w1, w2):
    a = norm_matmul(x, g, mod, w1, 0, shift_idx=3, is_ctx=False, out_dtype=BF16, relu2=True, name="mlp_up")
    return res_matmul([a], w2, 0, x, mod, gate_idx=5, is_ctx=False, name="mlp_down")


def _bf16_layer(w, idx):
    return w[idx:idx + 1].astype(BF16)


def kernel(x, c, ctx, c_ctx, ada_w, ada_b, norm_mix_g, norm_mlp_g, mlp_w1, mlp_w2, ev_w_in, ev_w_out, na_q_g, na_k_g, na_rpb, hg_lb_logits, hg_gnorm_g, od_w_in, od_w_out, df_q_g, df_k_g, df_lambda, df_subln_g):
    xl = x.reshape(N_LAT, D_MODEL)
    xc = ctx.reshape(N_CTX, D_MODEL)
    cond = jnp.concatenate([c, c_ctx[None], jnp.zeros((MOD_ROWS - BATCH - 1, D_MODEL), F32)], axis=0)
    mods = ada_table(cond, ada_w, ada_b)
    lb_all = jnp.cumsum(jax.nn.softmax(hg_lb_logits.astype(F32), axis=1), axis=1)
    cos2, sin2 = _rope_tables()

    for l in range(DEPTH):
        need_ctx = l < DEPTH - 1
        mod = mods[l]
        g_mix = norm_mix_g[l]
        if l % 2 == 0:
            e = l // 2
            p_ctx, w_in = norm_matmul(xc, g_mix, mod, ev_w_in, e, shift_idx=0, is_ctx=True, out_dtype=F32,
                                      cast_weights=True, name="even_in_ctx")
            p_lat = norm_matmul(xl, g_mix, mod, w_in, 0, shift_idx=0, is_ctx=False, out_dtype=F32, name="even_in")
            na_lat, na_ctx = na_attention(p_lat, p_ctx, na_q_g[e], na_k_g[e], na_rpb[e])
            hg_lat, hg_ctx = hgrn2(p_lat, p_ctx, lb_all[0, l], lb_all[1, l], hg_gnorm_g[e])
            if need_ctx:
                xc, w_out = res_matmul([na_ctx, hg_ctx], ev_w_out, e, xc, mod, gate_idx=2, is_ctx=True,
                                       cast_weights=True, name="mix_out_ctx")
            else:
                w_out = _bf16_layer(ev_w_out, e)
            xl = res_matmul([na_lat, hg_lat], w_out, 0, xl, mod, gate_idx=2, is_ctx=False, name="mix_out")
        else:
            assert not need_ctx, "an odd layer followed by another layer needs context outputs"
            o = l // 2
            p_ctx, w_in = norm_matmul(xc, g_mix, mod, od_w_in, o, shift_idx=0, is_ctx=True, out_dtype=F32,
                                      cast_weights=True, name="odd_in_ctx")
            p_lat = norm_matmul(xl, g_mix, mod, w_in, 0, shift_idx=0, is_ctx=False, out_dtype=F32, name="odd_in")
            lam_init = 0.8 - 0.6 * math.exp(-0.3 * l)
            lp = df_lambda[o].astype(F32)
            lam = jnp.exp(jnp.sum(lp[0] * lp[1])) - jnp.exp(jnp.sum(lp[2] * lp[3])) + lam_init
            om = diff_attention(p_lat, p_ctx, lam, cos2, sin2, df_q_g[o], df_k_g[o], df_subln_g[o],
                                1.0 - lam_init)
            xl = res_matmul([om], _bf16_layer(od_w_out, o), 0, xl, mod, gate_idx=2, is_ctx=False, name="mix_out")
        if need_ctx:
            xc, w1, w2 = _mlp_ctx(xc, mod, norm_mlp_g[l], mlp_w1, mlp_w2, l)
        else:
            w1, w2 = _bf16_layer(mlp_w1, l), _bf16_layer(mlp_w2, l)
        xl = _mlp(xl, mod, norm_mlp_g[l], w1, w2)
    return xl.reshape(BATCH, SEQ, D_MODEL)
```

```python
import functools
import math

import jax
import jax.numpy as jnp
import numpy as np
from jax import lax
from jax.experimental import pallas as pl
from jax.experimental.pallas import tpu as pltpu

F32 = jnp.float32
BF16 = jnp.bfloat16

D_MODEL = 2048
BATCH = 4
SEQ = 2048
DEPTH = 2
GRID_W = 64
GRID_ROWS = SEQ // GRID_W
CTX_LEN = 256
HEAD_DIM = 128
NA_HEADS = 8
NA_WIN_R = 8
NA_WIN_C = 16
NA_WIDTH = NA_HEADS * HEAD_DIM
HG_HEADS = 8
HG_CHUNK = 16
HG_WIDTH = HG_HEADS * HEAD_DIM
EVEN_IN = 8 * NA_WIDTH
DIFF_HEADS = 8
DIFF_DK = HEAD_DIM
DIFF_DV = 2 * HEAD_DIM
ODD_IN = 3 * D_MODEL
D_FF = 4 * D_MODEL
N_MOD = 6
ROPE_THETA = 10000.0
EPS = 1e-6

N_LAT = BATCH * SEQ
N_CTX = BATCH * CTX_LEN
MOD_ROWS = 8
CTX_MOD_ROW = BATCH
NEG_BIG = -1e30
LOG2E = math.log2(math.e)
SOFTMAX_NOSHIFT_LOG2 = 100.0

VMEM_LIMIT_V7X = 56 * 1024 * 1024
MM_TILE = 1024
MM_TK = 2048
ROW_SLICES = 4


def _cparams(sem, vmem=VMEM_LIMIT_V7X):
    return pltpu.CompilerParams(dimension_semantics=sem, vmem_limit_bytes=vmem)


def _dot(a, b):
    return jnp.dot(a, b, preferred_element_type=F32)


def _dot_nt(a, b):
    return lax.dot_general(a, b, (((1,), (1,)), ((), ())), preferred_element_type=F32)


def _rms_rows(x, g):
    return x * lax.rsqrt(jnp.mean(x * x, axis=-1, keepdims=True) + EPS) * g


def _mod_row_of_tile(is_ctx, tile_rows):
    if is_ctx:
        return lambda i: CTX_MOD_ROW
    return lambda i: (i * tile_rows) // SEQ


def _ada_body(c_ref, w_ref, b_ref, o_ref):
    c = c_ref[...]
    s = (c * jax.nn.sigmoid(c)).astype(BF16)
    o_ref[0] = _dot(s, w_ref[0].astype(BF16)) + b_ref[0]


def ada_table(cond, ada_w, ada_b):
    tn = 1024
    n = N_MOD * D_MODEL
    out = pl.pallas_call(
        _ada_body,
        out_shape=jax.ShapeDtypeStruct((DEPTH, MOD_ROWS, n), F32),
        grid=(DEPTH, n // tn),
        in_specs=[
            pl.BlockSpec((MOD_ROWS, D_MODEL), lambda l, j: (0, 0)),
            pl.BlockSpec((1, D_MODEL, tn), lambda l, j: (l, 0, j)),
            pl.BlockSpec((1, 1, tn), lambda l, j: (l, 0, j)),
        ],
        out_specs=pl.BlockSpec((1, MOD_ROWS, tn), lambda l, j: (l, 0, j)),
        compiler_params=_cparams(("arbitrary", "arbitrary")),
        name="ada_table",
    )(cond, ada_w, ada_b.reshape(DEPTH, 1, n))
    return out.reshape(DEPTH, MOD_ROWS, N_MOD, D_MODEL)


def _norm_mm_body(x_ref, g_ref, mod_ref, w_ref, *out_and_scratch, shift_idx, relu2, cast_w):
    o_ref, h_ref = out_and_scratch[0], out_and_scratch[-1]
    first_col_tile = pl.program_id(1) == 0

    if cast_w:
        wb_ref = out_and_scratch[1]
        wb_ref[...] = w_ref[...].astype(BF16)
    else:
        wb_ref = w_ref

    def project(rows):
        acc = _dot(h_ref[rows, :], wb_ref[...])
        if relu2:
            acc = jnp.maximum(acc, 0.0)
            acc = acc * acc
        o_ref[rows, :] = acc.astype(o_ref.dtype)

    @pl.when(first_col_tile)
    def _():
        shift = mod_ref[0, shift_idx:shift_idx + 1, :]
        scale = mod_ref[0, shift_idx + 1:shift_idx + 2, :]
        n_rows = h_ref.shape[0]
        for r0 in range(0, n_rows, n_rows // ROW_SLICES):
            rows = slice(r0, r0 + n_rows // ROW_SLICES)
            h_ref[rows, :] = (_rms_rows(x_ref[rows, :], g_ref[...]) * (1.0 + scale) + shift).astype(BF16)
            project(rows)

    @pl.when(jnp.logical_not(first_col_tile))
    def _():
        project(slice(None))


def norm_matmul(x, g, mod, w, w_idx, *, shift_idx, is_ctx, out_dtype, relu2=False, col_block0=0, n_cols=None,
                cast_weights=False, name):
    n_rows = x.shape[0]
    tm = MM_TILE
    if cast_weights:
        assert n_rows == tm and col_block0 == 0 and n_cols is None
        tn = MM_TILE
    else:
        tn = MM_TILE * (jnp.dtype(F32).itemsize // jnp.dtype(out_dtype).itemsize)
    col_block0 = col_block0 * MM_TILE // tn
    n_cols = w.shape[2] if n_cols is None else n_cols
    mod_row = _mod_row_of_tile(is_ctx, tm)
    out_shape = jax.ShapeDtypeStruct((n_rows, n_cols), out_dtype)
    out_specs = pl.BlockSpec((tm, tn), lambda i, j: (i, j))
    if cast_weights:
        out_shape = (out_shape, jax.ShapeDtypeStruct((1, D_MODEL, n_cols), BF16))
        out_specs = (out_specs, pl.BlockSpec((None, D_MODEL, tn), lambda i, j: (0, 0, j)))
    return pl.pallas_call(
        functools.partial(_norm_mm_body, shift_idx=shift_idx, relu2=relu2, cast_w=cast_weights),
        out_shape=out_shape,
        grid=(n_rows // tm, n_cols // tn),
        in_specs=[
            pl.BlockSpec((tm, D_MODEL), lambda i, j: (i, 0), pipeline_mode=pl.Buffered(1) if cast_weights else None),
            pl.BlockSpec((1, D_MODEL), lambda i, j: (0, 0)),
            pl.BlockSpec((1, N_MOD, D_MODEL), lambda i, j: (mod_row(i), 0, 0)),
            pl.BlockSpec((None, D_MODEL, tn), lambda i, j: (w_idx, 0, col_block0 + j)),
        ],
        out_specs=out_specs,
        scratch_shapes=[pltpu.VMEM((tm, D_MODEL), BF16)],
        compiler_params=_cparams(("arbitrary", "arbitrary")),
        name=name,
    )(x, g.reshape(1, D_MODEL), mod, w)


def _res_mm_body(*refs, n_x, nk, gate_idx, cast_w):
    x_refs = refs[:n_x]
    w_ref, res_ref, mod_ref, o_ref = refs[n_x:n_x + 4]
    scratch = refs[n_x + 4:]
    if cast_w:
        wb_ref, scratch = scratch[0], scratch[1:]

        wb_ref[...] = w_ref[...].astype(BF16)
    else:
        wb_ref = w_ref

    def partial_product(rows):
        acc = None
        k0 = 0
        for x_ref in x_refs:
            kw = x_ref.shape[1]
            d = _dot(x_ref[rows, :], wb_ref[k0:k0 + kw, :])
            acc = d if acc is None else acc + d
            k0 += kw
        return acc

    def finish(carried):
        gate = mod_ref[0, gate_idx:gate_idx + 1, :]
        n_rows = o_ref.shape[0]
        n_slices = 1 if carried is None else ROW_SLICES
        for r0 in range(0, n_rows, n_rows // n_slices):
            rows = slice(r0, r0 + n_rows // n_slices)
            acc = partial_product(rows)
            if carried is not None:
                acc = carried[rows, :] + acc
            o_ref[rows, :] = res_ref[rows, :] + gate * acc

    if nk == 1:
        finish(None)
    else:
        acc_ref = scratch[0]
        k = pl.program_id(2)

        @pl.when(k == 0)
        def _():
            acc_ref[...] = partial_product(slice(None))

        @pl.when((k > 0) & (k < nk - 1))
        def _():
            acc_ref[...] += partial_product(slice(None))

        @pl.when(k == nk - 1)
        def _():
            finish(acc_ref)


def res_matmul(xs, w, w_idx, res, mod, *, gate_idx, is_ctx, cast_weights=False, name):
    n_rows = res.shape[0]
    _, k_dim, n_dim = w.shape
    tk = MM_TK if len(xs) == 1 else k_dim
    nk = k_dim // tk
    if cast_weights:
        assert n_rows == MM_TILE
        tm, tn = MM_TILE, MM_TILE // 2
    elif nk == 1:
        tm, tn = MM_TILE // 2, n_dim
    else:
        tm = tn = MM_TILE
    if len(xs) == 1:
        x_specs = [pl.BlockSpec((tm, tk), lambda i, j, k: (i, k))]
    else:
        x_specs = [pl.BlockSpec((tm, x.shape[1]), lambda i, j, k: (i, 0)) for x in xs]
    mod_row = _mod_row_of_tile(is_ctx, tm)
    out_shape = jax.ShapeDtypeStruct((n_rows, n_dim), F32)
    out_specs = pl.BlockSpec((tm, tn), lambda i, j, k: (i, j))
    if cast_weights:
        out_shape = (out_shape, jax.ShapeDtypeStruct((1, k_dim, n_dim), BF16))
        out_specs = (out_specs, pl.BlockSpec((None, tk, tn), lambda i, j, k: (0, k, j)))
    return pl.pallas_call(
        functools.partial(_res_mm_body, n_x=len(xs), nk=nk, gate_idx=gate_idx, cast_w=cast_weights),
        out_shape=out_shape,
        grid=(n_rows // tm, n_dim // tn, nk),
        in_specs=x_specs + [
            pl.BlockSpec((None, tk, tn), lambda i, j, k: (w_idx, k, j)),
            pl.BlockSpec((tm, tn), lambda i, j, k: (i, j)),
            pl.BlockSpec((1, N_MOD, tn), lambda i, j, k: (mod_row(i), 0, j)),
        ],
        out_specs=out_specs,
        scratch_shapes=[pltpu.VMEM((tm, tn), F32)] if nk > 1 else [],
        compiler_params=_cparams(("arbitrary", "arbitrary", "arbitrary")),
        name=name,
    )(*xs, w, res, mod)


NA_GROUP_ROWS = 4
NA_GROUP_Q = NA_GROUP_ROWS * GRID_W
NA_GROUP_KROWS = NA_GROUP_ROWS + NA_WIN_R
NA_GROUP_K = NA_GROUP_KROWS * GRID_W
NA_GROUPS = GRID_ROWS // NA_GROUP_ROWS
NA_PREP_ROWS = 256
NA_SCORE_LOOKAHEAD = 1


def _na_window_row0(rq):
    return min(max(rq - NA_WIN_R // 2, 0), GRID_ROWS - NA_WIN_R)


def _na_group_key_row0(r):
    return min(max(r - NA_WIN_R // 2, 0), GRID_ROWS - NA_GROUP_KROWS)


def _na_group_signature(r):
    k0 = _na_group_key_row0(r)
    return (k0 - r,) + tuple(_na_window_row0(r + j) - k0 for j in range(NA_GROUP_ROWS))


NA_GROUP_KINDS = tuple(sorted({_na_group_signature(r): r for r in reversed(range(0, GRID_ROWS, NA_GROUP_ROWS))}.values()))


def _na_group_kind(r):
    return [_na_group_signature(k) for k in NA_GROUP_KINDS].index(_na_group_signature(r))


def na_bias_table(rpb):
    w = GRID_W
    col = np.arange(w)
    c0 = np.clip(col - NA_WIN_C // 2, 0, w - NA_WIN_C)
    in_win = (col[None, :] >= c0[:, None]) & (col[None, :] < c0[:, None] + NA_WIN_C)
    padded = jnp.pad(rpb.astype(F32), ((0, 0), (0, 0), (w, w)))
    toep = jnp.stack([padded[:, :, NA_WIN_C - 1 - qc + w:NA_WIN_C - 1 - qc + 2 * w] for qc in range(w)], axis=2)
    toep = jnp.where(in_win[None, None], toep * LOG2E, NEG_BIG)
    masked = jnp.full((NA_HEADS, w, w), NEG_BIG, F32)
    kinds = []
    for r in NA_GROUP_KINDS:
        q_rows = []
        for rq in range(r, r + NA_GROUP_ROWS):
            r0 = _na_window_row0(rq)
            blocks = []
            for kr in range(_na_group_key_row0(r), _na_group_key_row0(r) + NA_GROUP_KROWS):
                blocks.append(toep[:, kr - rq + NA_WIN_R - 1] if r0 <= kr < r0 + NA_WIN_R else masked)
            q_rows.append(jnp.concatenate(blocks, axis=2))
        kinds.append(jnp.concatenate(q_rows, axis=1))
    return jnp.stack(kinds, axis=1)


def _na_body(flag_ref, q_ref, k_ref, v_ref, qc_ref, kc_ref, vc_ref, qg_ref, kg_ref, bias_ref,
             o_ref, oc_ref, qn_ref, knt_ref, vn_ref, kcnt_ref, vcn_ref):
    scale = HEAD_DIM ** -0.5 * LOG2E
    qg = qg_ref[...]
    kg = kg_ref[...]

    def with_ones(v):
        return jnp.concatenate([v.astype(BF16), jnp.ones(v.shape, BF16)], axis=1)

    def normalised(o_ext):
        return (o_ext[:, :HEAD_DIM] / o_ext[:, HEAD_DIM:]).astype(BF16)

    for i in range(SEQ // NA_PREP_ROWS):
        rows = slice(i * NA_PREP_ROWS, (i + 1) * NA_PREP_ROWS)
        qn_ref[rows, :] = (_rms_rows(q_ref[rows, :], qg) * scale).astype(BF16)
        knt_ref[:, rows] = _rms_rows(k_ref[rows, :], kg).T.astype(BF16)
        vn_ref[rows, :] = with_ones(v_ref[rows, :])
    kcnt = _rms_rows(kc_ref[...], kg).T.astype(BF16)
    vcn = with_ones(vc_ref[...])
    kcnt_ref[...] = kcnt
    vcn_ref[...] = vcn

    def group_rows(gi):
        k0 = _na_group_key_row0(gi * NA_GROUP_ROWS) * GRID_W
        return slice(gi * NA_GROUP_Q, (gi + 1) * NA_GROUP_Q), slice(k0, k0 + NA_GROUP_K)

    def scores(gi):
        qrows, krows = group_rows(gi)
        qb = qn_ref[qrows, :]
        return _dot(qb, knt_ref[:, krows]), _dot(qb, kcnt_ref[...])

    def attend(shifted):
        pending = [scores(gi) for gi in range(NA_SCORE_LOOKAHEAD)]
        for gi in range(NA_GROUPS):
            s1, s2 = pending.pop(0)
            if gi + NA_SCORE_LOOKAHEAD < NA_GROUPS:
                pending.append(scores(gi + NA_SCORE_LOOKAHEAD))
            qrows, krows = group_rows(gi)
            s1 = s1 + bias_ref[0, _na_group_kind(gi * NA_GROUP_ROWS)]
            if shifted:
                m = jnp.maximum(jnp.max(s1, axis=-1, keepdims=True), jnp.max(s2, axis=-1, keepdims=True))
                s1, s2 = s1 - m, s2 - m
            p1 = jnp.exp2(s1).astype(BF16)
            p2 = jnp.exp2(s2).astype(BF16)
            o_ref[qrows, :] = normalised(_dot(p1, vn_ref[krows, :]) + _dot(p2, vcn_ref[...]))

        s = _dot((_rms_rows(qc_ref[...], qg) * scale).astype(BF16), kcnt)
        if shifted:
            s = s - jnp.max(s, axis=-1, keepdims=True)
        oc_ref[...] = normalised(_dot(jnp.exp2(s).astype(BF16), vcn))

    scores_bounded = flag_ref[0, 0] > 0.5
    pl.when(scores_bounded)(lambda: attend(False))
    pl.when(jnp.logical_not(scores_bounded))(lambda: attend(True))


def na_attention(p_lat, p_ctx, q_g, k_g, rpb):
    hd = HEAD_DIM
    lat = lambda seg: pl.BlockSpec((SEQ, hd), lambda h, b: (b, seg * NA_HEADS + h))
    ctx = lambda seg: pl.BlockSpec((CTX_LEN, hd), lambda h, b: (b, seg * NA_HEADS + h))
    bias = na_bias_table(rpb)
    score_bound = LOG2E * (hd ** 0.5 * jnp.max(jnp.abs(q_g)) * jnp.max(jnp.abs(k_g)) + jnp.max(jnp.abs(rpb)))
    flag = (score_bound < SOFTMAX_NOSHIFT_LOG2).astype(F32).reshape(1, 1)
    return pl.pallas_call(
        _na_body,
        out_shape=(jax.ShapeDtypeStruct((N_LAT, NA_WIDTH), BF16),
                   jax.ShapeDtypeStruct((N_CTX, NA_WIDTH), BF16)),
        grid=(NA_HEADS, BATCH),
        in_specs=[pl.BlockSpec(memory_space=pltpu.SMEM),
                  lat(0), lat(1), lat(2), ctx(0), ctx(1), ctx(2),
                  pl.BlockSpec((1, hd), lambda h, b: (0, 0)),
                  pl.BlockSpec((1, hd), lambda h, b: (0, 0)),
                  pl.BlockSpec((1, len(NA_GROUP_KINDS), NA_GROUP_Q, NA_GROUP_K), lambda h, b: (h, 0, 0, 0))],
        out_specs=(pl.BlockSpec((SEQ, hd), lambda h, b: (b, h)),
                   pl.BlockSpec((CTX_LEN, hd), lambda h, b: (b, h))),
        scratch_shapes=[pltpu.VMEM((SEQ, hd), BF16), pltpu.VMEM((hd, SEQ), BF16), pltpu.VMEM((SEQ, 2 * hd), BF16),
                        pltpu.VMEM((hd, CTX_LEN), BF16), pltpu.VMEM((CTX_LEN, 2 * hd), BF16)],
        compiler_params=_cparams(("arbitrary", "arbitrary")),
        name="na_attention",
    )(flag, p_lat, p_lat, p_lat, p_ctx, p_ctx, p_ctx, q_g.reshape(1, hd), k_g.reshape(1, hd), bias)


HG_BLK = 128
HG_CPB = HG_BLK // HG_CHUNK
HG_CTX_BLKS = CTX_LEN // HG_BLK
HG_LAT_BLKS = SEQ // HG_BLK
HG_BLKS = HG_CTX_BLKS + HG_LAT_BLKS
HG_SCAN_STEPS = 6


def _hg_ref_rows(reverse):
    b, c = HG_BLK, HG_CHUNK
    if not reverse:
        return {c: [None] + [c * i - 1 for i in range(1, b // c)],
                32: [32 * j + 15 for j in range(b // 32)], 64: [64 * j + 31 for j in range(b // 64)],
                128: [63], "end": [b - 1]}
    return {c: [c * i + c for i in range(b // c - 1)] + [None],
            32: [32 * j + 16 for j in range(b // 32)], 64: [64 * j + 32 for j in range(b // 64)],
            128: [64], "end": [0]}


def _hg_level_codes(reverse):
    t = lax.broadcasted_iota(jnp.int32, (HG_BLK, HG_BLK), 0)
    s = lax.broadcasted_iota(jnp.int32, (HG_BLK, HG_BLK), 1)
    if reverse:
        t, s = s, t
    code = jnp.where((t >> 6) > (s >> 6), 4, 0)
    code = jnp.where(((t >> 6) == (s >> 6)) & ((t >> 5) > (s >> 5)), 3, code)
    code = jnp.where(((t >> 5) == (s >> 5)) & ((t >> 4) > (s >> 4)), 2, code)
    return jnp.where(((t >> 4) == (s >> 4)) & (s <= t), 1, code)


def _hg_body(q_ref, zf_ref, zb_ref, i_ref, g_ref, qc_ref, zfc_ref, zbc_ref, ic_ref, gc_ref,
             lbf_ref, lbb_ref, gn_ref, o_ref, oc_ref,
             qd_s, ka_s, q64_s, k64_s, q128_s, k128_s, qb_s, ke_s, gb_s, v_s, o_s, st_s, x_s, code_s):
    lbs = (lbf_ref[...], lbb_ref[...])
    ri = lax.broadcasted_iota(jnp.int32, (HG_BLK, HG_BLK), 0)
    ci = lax.broadcasted_iota(jnp.int32, (HG_BLK, HG_BLK), 1)
    tri = jnp.where(ci <= ri, 1.0, 0.0).astype(BF16)
    later_half = {g: jnp.where((ri & (g // 2)) != 0, 1.0, -1.0) for g in (32, 64, 128)}
    for d in range(2):
        code_s[d] = _hg_level_codes(d == 1)

    def prefix_rows(x):
        hi = x.astype(BF16)
        r1 = x - hi.astype(F32)
        mid = r1.astype(BF16)
        lo = (r1 - mid.astype(F32)).astype(BF16)
        y = _dot(tri, jnp.concatenate([hi, mid, lo], axis=1))
        return y[:, :HEAD_DIM] + y[:, HEAD_DIM:2 * HEAD_DIM] + y[:, 2 * HEAD_DIM:]

    def ref_rows(d, rows):
        group = HG_BLK // len(rows)
        parts = [jnp.zeros((group, HEAD_DIM), F32) if r is None else
                 jnp.broadcast_to(x_s[d, r:r + 1, :], (group, HEAD_DIM)) for r in rows]
        return parts[0] if len(parts) == 1 else jnp.concatenate(parts, axis=0)

    def prep_block(blk, q, zf, zb, v):
        qs = q * jax.nn.sigmoid(q)
        v_s[blk] = v.astype(BF16)
        for d, z in enumerate((zf, zb)):
            f = lbs[d] + (1.0 - lbs[d]) * jax.nn.sigmoid(z)
            logf = jnp.log2(f)
            k = 1.0 - f
            x = prefix_rows(logf)
            x_s[d] = x
            if d == 1:
                x = ref_rows(d, [HG_BLK - 1]) - x + logf
                x_s[d] = x
            rows = _hg_ref_rows(d == 1)
            r16, r32, r64, r128 = (ref_rows(d, rows[g]) for g in (HG_CHUNK, 32, 64, 128))
            x_end = ref_rows(d, rows["end"])
            sign = -1.0 if d == 1 else 1.0
            e32, e64, e128 = (jnp.exp2((x - r) * (sign * later_half[g]))
                              for g, r in ((32, r32), (64, r64), (128, r128)))
            qd_s[d, blk] = (qs * jnp.exp2(x - r16)).astype(BF16)
            ka_s[d, blk] = jnp.concatenate([(k * jnp.exp2(r16 - x)).T, (k * e32).T], axis=1).astype(BF16)
            q64_s[d, blk] = (qs * e64).astype(BF16)
            k64_s[d, blk] = (k * e64).T.astype(BF16)
            q128_s[d, blk] = (qs * e128).astype(BF16)
            k128_s[d, blk] = (k * e128).T.astype(BF16)
            qb_s[d, blk] = (qs * jnp.exp2(x)).astype(BF16)
            ke_s[d, blk] = (k * jnp.exp2(x_end - x)).T.astype(BF16)
            gb_s[d, blk] = jnp.exp2(x_end).T

    for cb in range(HG_CTX_BLKS):
        rows = slice(cb * HG_BLK, (cb + 1) * HG_BLK)
        prep_block(cb, qc_ref[rows, :], zfc_ref[rows, :], zbc_ref[rows, :], ic_ref[rows, :])

    def prep_lat(n, _):
        rows = pl.ds(pl.multiple_of(n * HG_BLK, HG_BLK), HG_BLK)
        prep_block(HG_CTX_BLKS + n, q_ref[rows, :], zf_ref[rows, :], zb_ref[rows, :], i_ref[rows, :])
        return 0

    lax.fori_loop(0, HG_LAT_BLKS, prep_lat, 0, unroll=2)

    st_s[...] = jnp.zeros_like(st_s)

    def scan_steps(i, _):
        chains = []
        for n in [i * HG_SCAN_STEPS + j for j in range(HG_SCAN_STEPS)]:
            chains.append((0, n))
            chains.append((1, jnp.where(n < HG_CTX_BLKS, HG_CTX_BLKS - 1 - n, HG_BLKS + HG_CTX_BLKS - 1 - n)))
        ready = []
        for d, blk in chains:
            a1 = _dot(qd_s[d, blk], ka_s[d, blk])
            a64 = _dot(q64_s[d, blk], k64_s[d, blk])
            a128 = _dot(q128_s[d, blk], k128_s[d, blk])
            v = v_s[blk]
            inc = _dot(ke_s[d, blk], v)
            code = code_s[d]
            att = jnp.where(code == 1, a1[:, :HG_BLK],
                            jnp.where(code == 2, a1[:, HG_BLK:],
                                      jnp.where(code == 3, a64, jnp.where(code == 4, a128, 0.0))))
            ready.append((jnp.concatenate([att.astype(BF16), qb_s[d, blk]], axis=1), v, inc))
        for (d, blk), (lhs, v, inc) in zip(chains, ready):
            state = st_s[d]
            o_s[d, blk] = _dot(lhs, jnp.concatenate([v, state.astype(BF16)], axis=0))
            st_s[d] = gb_s[d, blk] * state + inc
        return 0

    lax.fori_loop(0, HG_BLKS // HG_SCAN_STEPS, scan_steps, 0)

    gn = gn_ref[...]

    def finish(blk, gate):
        o = o_s[0, blk] + o_s[1, blk]
        return (_rms_rows(o, gn) * (gate * jax.nn.sigmoid(gate))).astype(BF16)

    for cb in range(HG_CTX_BLKS):
        rows = slice(cb * HG_BLK, (cb + 1) * HG_BLK)
        oc_ref[rows, :] = finish(cb, gc_ref[rows, :])

    def fin_lat(n, _):
        rows = pl.ds(pl.multiple_of(n * HG_BLK, HG_BLK), HG_BLK)
        o_ref[rows, :] = finish(HG_CTX_BLKS + n, g_ref[rows, :])
        return 0

    lax.fori_loop(0, HG_LAT_BLKS, fin_lat, 0, unroll=4)


def hgrn2(p_lat, p_ctx, lb_fwd, lb_bwd, gn_g):
    hd = HEAD_DIM
    lat = lambda seg: pl.BlockSpec((SEQ, hd), lambda b, h: (b, seg * HG_HEADS + h))
    ctx = lambda seg: pl.BlockSpec((CTX_LEN, hd), lambda b, h: (b, seg * HG_HEADS + h))
    head_vec = pl.BlockSpec((1, hd), lambda b, h: (0, h))
    dir_bf = lambda width=hd: pltpu.VMEM((2, HG_BLKS, HG_BLK, width), BF16)
    dir_f32 = lambda: pltpu.VMEM((2, HG_BLKS, HG_BLK, hd), F32)
    return pl.pallas_call(
        _hg_body,
        out_shape=(jax.ShapeDtypeStruct((N_LAT, HG_WIDTH), BF16),
                   jax.ShapeDtypeStruct((N_CTX, HG_WIDTH), BF16)),
        grid=(BATCH, HG_HEADS),
        in_specs=[lat(3), lat(4), lat(5), lat(6), lat(7), ctx(3), ctx(4), ctx(5), ctx(6), ctx(7),
                  head_vec, head_vec, pl.BlockSpec((1, hd), lambda b, h: (0, 0))],
        out_specs=(pl.BlockSpec((SEQ, hd), lambda b, h: (b, h)),
                   pl.BlockSpec((CTX_LEN, hd), lambda b, h: (b, h))),
        scratch_shapes=[dir_bf(), dir_bf(2 * hd), dir_bf(), dir_bf(), dir_bf(), dir_bf(), dir_bf(), dir_bf(),
                        dir_f32(), pltpu.VMEM((HG_BLKS, HG_BLK, hd), BF16), dir_f32(),
                        pltpu.VMEM((2, hd, hd), F32), pltpu.VMEM((2, HG_BLK, hd), F32),
                        pltpu.VMEM((2, HG_BLK, HG_BLK), jnp.int32)],
        compiler_params=_cparams(("arbitrary", "arbitrary")),
        name="hgrn2",
    )(p_lat, p_lat, p_lat, p_lat, p_lat, p_ctx, p_ctx, p_ctx, p_ctx, p_ctx,
      lb_fwd.reshape(1, -1), lb_bwd.reshape(1, -1), gn_g.reshape(1, hd))


DF_TQ = 1024
DF_NK = SEQ + CTX_LEN
DF_TK = 768
DF_KTILES = DF_NK // DF_TK
DF_SCORE_LOOKAHEAD = 2


def _diff_body(lam_ref, q_ref, k_ref, kc_ref, v_ref, vc_ref, cos_ref, sin_ref, cost_ref, sint_ref,
               qgt_ref, kg_ref, sg_ref, o_ref, qt_s, k0_s, k1_s, vt_s, e_s, *, out_scale):
    dk = DIFF_DK
    kg = kg_ref[...]

    @pl.when(pl.program_id(2) == 0)
    def _():
        ri = lax.broadcasted_iota(jnp.int32, (dk, dk), 0)
        ci = lax.broadcasted_iota(jnp.int32, (dk, dk), 1)
        ones = jnp.ones((dk, dk), BF16)
        swap_halves = jnp.where(ri == ((ci + dk // 2) & (dk - 1)), 1.0, 0.0).astype(BF16)

        def times(x, m):
            hi = x.astype(BF16)
            lo = (x - hi.astype(F32)).astype(BF16)
            return _dot(hi, m) + _dot(lo, m)

        def key_rows(x, c2, s2):
            xn = x * lax.rsqrt(times(x * x, ones) * (1.0 / dk) + EPS) * kg
            return (xn * c2 + times(xn, swap_halves) * s2).astype(BF16)

        def prep(i, _):
            rows = pl.ds(pl.multiple_of(i * CTX_LEN, CTX_LEN), CTX_LEN)
            c2 = cos_ref[rows, :]
            s2 = sin_ref[rows, :]
            kk = k_ref[rows, :]
            k0_s[rows, :] = key_rows(kk[:, :dk], c2, s2)
            k1_s[rows, :] = key_rows(kk[:, dk:], c2, s2)
            vt_s[i] = v_ref[rows, :].T.astype(BF16)
            qq_t = q_ref[rows, :].T
            ct = cost_ref[i]
            st = sint_ref[i]
            for j in range(2):
                xt = qq_t[j * dk:(j + 1) * dk, :]
                inv = lax.rsqrt(jnp.mean(xt * xt, axis=0, keepdims=True) + EPS)
                xn = xt * inv * qgt_ref[...]
                swapped = jnp.concatenate([xn[dk // 2:, :], xn[:dk // 2, :]], axis=0)
                qt_s[j, i] = (xn * ct + swapped * st).astype(BF16)
            return 0

        lax.fori_loop(0, SEQ // CTX_LEN, prep, 0, unroll=4)
        kc = kc_ref[...]
        k0_s[SEQ:, :] = _rms_rows(kc[:, :dk], kg).astype(BF16)
        k1_s[SEQ:, :] = _rms_rows(kc[:, dk:], kg).astype(BF16)
        vt_s[SEQ // CTX_LEN] = vc_ref[...].T.astype(BF16)

    lam = lam_ref[0, 0]
    qpt = DF_TQ // CTX_LEN
    q0 = pl.program_id(2) * qpt
    qt = [jnp.concatenate([qt_s[i, q0 + j] for j in range(qpt)], axis=1) for i in range(2)]
    k_s = (k0_s, k1_s)

    cpt = DF_TK // CTX_LEN
    jobs = [(t, i) for t in range(DF_KTILES) for i in range(2)]

    def scores(job):
        t, i = job
        return _dot(k_s[i][t * DF_TK:(t + 1) * DF_TK, :], qt[i])

    def attend(shifted):
        m = [None, None]
        l = [None, None]
        m_tile = [[None] * DF_KTILES for _ in range(2)]
        pending = [scores(job) for job in jobs[:DF_SCORE_LOOKAHEAD]]
        for n, (t, i) in enumerate(jobs):
            s = pending.pop(0)
            if n + DF_SCORE_LOOKAHEAD < len(jobs):
                pending.append(scores(jobs[n + DF_SCORE_LOOKAHEAD]))
            if shifted:
                tile_max = jnp.max(s, axis=0, keepdims=True)
                m_new = tile_max if t == 0 else jnp.maximum(m[i], tile_max)
                e = jnp.exp2(s - m_new)
            else:
                e = jnp.exp2(s)
            e_s[i, t] = e.astype(BF16)
            tile_sum = jnp.sum(e, axis=0, keepdims=True)
            if t == 0:
                l[i] = tile_sum
            elif shifted:
                l[i] = jnp.exp2(m[i] - m_new) * l[i] + tile_sum
            else:
                l[i] = l[i] + tile_sum
            if shifted:
                m[i] = m_new
                m_tile[i][t] = m_new

        weight = (1.0 / l[0], lam / l[1])
        acc = None
        for t in range(DF_KTILES):
            if shifted:
                f0, f1 = ((jnp.exp2(m_tile[i][t] - m[i]) * weight[i]).astype(BF16) for i in range(2))
            else:
                f0, f1 = (w.astype(BF16) for w in weight)
            a = e_s[0, t] * f0 - e_s[1, t] * f1
            vt = jnp.concatenate([vt_s[t * cpt + j] for j in range(cpt)], axis=1)
            d = _dot(vt, a)
            acc = d if acc is None else acc + d
        o_ref[...] = (_rms_rows(acc.T, sg_ref[...]) * out_scale).astype(o_ref.dtype)

    scores_bounded = lam_ref[0, 1] > 0.5
    pl.when(scores_bounded)(lambda: attend(False))
    pl.when(jnp.logical_not(scores_bounded))(lambda: attend(True))


def diff_attention(p_lat, p_ctx, lam, cos2, sin2, q_g, k_g, subln_g, out_scale):
    nq = SEQ // DF_TQ
    w = 2 * DIFF_DK
    kcol0 = D_MODEL // w
    vcol0 = 2 * D_MODEL // w
    vec = lambda n: pl.BlockSpec((1, n), lambda b, h, i: (0, 0))
    n_chunks = SEQ // CTX_LEN
    chunked_t = lambda tab: tab.reshape(n_chunks, CTX_LEN, DIFF_DK).transpose(0, 2, 1)
    q_gain_t = jnp.broadcast_to((q_g.astype(F32) * (DIFF_DK ** -0.5 * LOG2E))[:, None], (DIFF_DK, CTX_LEN))
    score_bound = LOG2E * DIFF_DK ** 0.5 * jnp.max(jnp.abs(q_g)) * jnp.max(jnp.abs(k_g))
    scalars = jnp.stack([lam.reshape(()), (score_bound < SOFTMAX_NOSHIFT_LOG2).astype(F32)]).reshape(1, 2)
    return pl.pallas_call(
        functools.partial(_diff_body, out_scale=out_scale),
        out_shape=jax.ShapeDtypeStruct((N_LAT, D_MODEL), BF16),
        grid=(BATCH, DIFF_HEADS, nq),
        in_specs=[
            pl.BlockSpec(memory_space=pltpu.SMEM),
            pl.BlockSpec((SEQ, w), lambda b, h, i: (b, h)),
            pl.BlockSpec((SEQ, w), lambda b, h, i: (b, kcol0 + h)),
            pl.BlockSpec((CTX_LEN, w), lambda b, h, i: (b, kcol0 + h)),
            pl.BlockSpec((SEQ, w), lambda b, h, i: (b, vcol0 + h)),
            pl.BlockSpec((CTX_LEN, w), lambda b, h, i: (b, vcol0 + h)),
            pl.BlockSpec((SEQ, DIFF_DK), lambda b, h, i: (0, 0)),
            pl.BlockSpec((SEQ, DIFF_DK), lambda b, h, i: (0, 0)),
            pl.BlockSpec((n_chunks, DIFF_DK, CTX_LEN), lambda b, h, i: (0, 0, 0)),
            pl.BlockSpec((n_chunks, DIFF_DK, CTX_LEN), lambda b, h, i: (0, 0, 0)),
            pl.BlockSpec((DIFF_DK, CTX_LEN), lambda b, h, i: (0, 0)),
            vec(DIFF_DK), vec(DIFF_DV),
        ],
        out_specs=pl.BlockSpec((DF_TQ, w), lambda b, h, i: (b * nq + i, h)),
        scratch_shapes=[pltpu.VMEM((2, SEQ // CTX_LEN, DIFF_DK, CTX_LEN), BF16),
                        pltpu.VMEM((DF_NK, DIFF_DK), BF16), pltpu.VMEM((DF_NK, DIFF_DK), BF16),
                        pltpu.VMEM((DF_NK // CTX_LEN, DIFF_DV, CTX_LEN), BF16),
                        pltpu.VMEM((2, DF_KTILES, DF_TK, DF_TQ), BF16)],
        compiler_params=_cparams(("arbitrary", "arbitrary", "arbitrary")),
        name="diff_attention",
    )(scalars, p_lat, p_lat, p_ctx, p_lat, p_ctx, cos2, sin2, chunked_t(cos2), chunked_t(sin2), q_gain_t,
      k_g.reshape(1, DIFF_DK), subln_g.reshape(1, DIFF_DV))


def _rope_tables():
    t = jnp.arange(SEQ)
    row = (t // GRID_W).astype(F32)
    col = (t % GRID_W).astype(F32)
    n_freq = DIFF_DK // 4
    inv = ROPE_THETA ** (-jnp.arange(n_freq, dtype=F32) / n_freq)
    ang = jnp.concatenate([row[:, None] * inv, col[:, None] * inv], axis=-1)
    cos, sin = jnp.cos(ang), jnp.sin(ang)
    return jnp.concatenate([cos, cos], axis=-1), jnp.concatenate([-sin, sin], axis=-1)


def _mlp_ctx(x, mod, g, w1_f32, w2_f32, layer):
    a, w1 = norm_matmul(x, g, mod, w1_f32, layer, shift_idx=3, is_ctx=True, out_dtype=BF16, relu2=True,
                        cast_weights=True, name="mlp_up_ctx")
    x, w2 = res_matmul([a], w2_f32, layer, x, mod, gate_idx=5, is_ctx=True, cast_weights=True, name="mlp_down_ctx")
    return x, w1, w2


def _mlp(x, mod, g, w1, w2):
    a = norm_matmul(x, g, mod, w1, 0, shift_idx=3, is_ctx=False, out_dtype=BF16, relu2=True, name="mlp_up")
    return res_matmul([a], w2, 0, x, mod, gate_idx=5, is_ctx=False, name="mlp_down")


def _bf16_layer(w, idx):
    return w[idx:idx + 1].astype(BF16)


def kernel(x, c, ctx, c_ctx, ada_w, ada_b, norm_mix_g, norm_mlp_g, mlp_w1, mlp_w2, ev_w_in, ev_w_out, na_q_g, na_k_g, na_rpb, hg_lb_logits, hg_gnorm_g, od_w_in, od_w_out, df_q_g, df_k_g, df_lambda, df_subln_g):
    xl = x.reshape(N_LAT, D_MODEL)
    xc = ctx.reshape(N_CTX, D_MODEL)
    cond = jnp.concatenate([c, c_ctx[None], jnp.zeros((MOD_ROWS - BATCH - 1, D_MODEL), F32)], axis=0)
    mods = ada_table(cond, ada_w, ada_b)
    lb_all = jnp.cumsum(jax.nn.softmax(hg_lb_logits.astype(F32), axis=1), axis=1)
    cos2, sin2 = _rope_tables()

    for l in range(DEPTH):
        need_ctx = l < DEPTH - 1
        mod = mods[l]
        g_mix = norm_mix_g[l]
        if l % 2 == 0:
            e = l // 2
            p_ctx, w_in = norm_matmul(xc, g_mix, mod, ev_w_in, e, shift_idx=0, is_ctx=True, out_dtype=F32,
                                      cast_weights=True, name="even_in_ctx")
            p_lat = norm_matmul(xl, g_mix, mod, w_in, 0, shift_idx=0, is_ctx=False, out_dtype=F32, name="even_in")
            na_lat, na_ctx = na_attention(p_lat, p_ctx, na_q_g[e], na_k_g[e], na_rpb[e])
            hg_lat, hg_ctx = hgrn2(p_lat, p_ctx, lb_all[0, l], lb_all[1, l], hg_gnorm_g[e])
            if need_ctx:
                xc, w_out = res_matmul([na_ctx, hg_ctx], ev_w_out, e, xc, mod, gate_idx=2, is_ctx=True,
                                       cast_weights=True, name="mix_out_ctx")
            else:
                w_out = _bf16_layer(ev_w_out, e)
            xl = res_matmul([na_lat, hg_lat], w_out, 0, xl, mod, gate_idx=2, is_ctx=False, name="mix_out")
        else:
            assert not need_ctx, "an odd layer followed by another layer needs context outputs"
            o = l // 2
            p_ctx, w_in = norm_matmul(xc, g_mix, mod, od_w_in, o, shift_idx=0, is_ctx=True, out_dtype=F32,
                                      cast_weights=True, name="odd_in_ctx")
            p_lat = norm_matmul(xl, g_mix, mod, w_in, 0, shift_idx=0, is_ctx=False, out_dtype=F32, name="odd_in")
            lam_init = 0.8 - 0.6 * math.exp(-0.3 * l)
            lp = df_lambda[o].astype(F32)
            lam = jnp.exp(jnp.sum(lp[0] * lp[1])) - jnp.exp(jnp.sum(lp[2] * lp[3])) + lam_init
            om = diff_attention(p_lat, p_ctx, lam, cos2, sin2, df_q_g[o], df_k_g[o], df_subln_g[o],
                                1.0 - lam_init)
            xl = res_matmul([om], _bf16_layer(od_w_out, o), 0, xl, mod, gate_idx=2, is_ctx=False, name="mix_out")
        if need_ctx:
            xc, w1, w2 = _mlp_ctx(xc, mod, norm_mlp_g[l], mlp_w1, mlp_w2, l)
        else:
            w1, w2 = _bf16_layer(mlp_w1, l), _bf16_layer(mlp_w2, l)
        xl = _mlp(xl, mod, norm_mlp_g[l], w1, w2)
    return xl.reshape(BATCH, SEQ, D_MODEL)
```
